```python
import math
import jax, jax.numpy as jnp
from jax import lax
import numpy as np

D_MODEL = 1024
BATCH = 2
SEQ = 16384
DEPTH = 2
DEC_BATCH = 16
DEC_SEQ = 16
PAST_LEN = 2048

CHUNK = 64
H_A = 8
DA = 64
DVA = 2 * DA
H_B = 8
DB = 64
BAND_CHUNKS = 8
BAND_PAST = BAND_CHUNKS * CHUNK
BAND = BAND_PAST + CHUNK
REL_CLIP = 128
D_FF = 4 * D_MODEL
Q_BLOCK = 128
EPS = 1e-6
NEG_INF = -1e30
W_A_QK = H_A * 2 * DA
W_A_V = H_A * DVA
W_B = H_B * DB
IN_COLS = 2 * W_A_QK + W_A_V + 3 * W_B
SPLITS = (W_A_QK, 2 * W_A_QK, 2 * W_A_QK + W_A_V, 2 * W_A_QK + W_A_V + W_B, 2 * W_A_QK + W_A_V + 2 * W_B)

kernel_name = 'hybrid_diffattn_chunkband_stream_step'


def rms(x, g):
    xf = x.astype(jnp.float32)
    return xf * lax.rsqrt(jnp.mean(xf * xf, axis=-1, keepdims=True) + EPS) * g.astype(jnp.float32)


def alibi_slopes():
    return jnp.exp2(-8.0 * jnp.arange(1, H_A + 1, dtype=jnp.float32) / H_A)


def project(h, w_in, qn_a, kn_a, qn_b, kn_b):
    b, t = h.shape[:2]
    z = h @ w_in
    qa, ka, va, qb, kb, vb = jnp.split(z, SPLITS, axis=-1)
    qa = rms(qa.reshape(b, t, H_A, 2, DA), qn_a)
    ka = rms(ka.reshape(b, t, H_A, 2, DA), kn_a).reshape(b, t, H_A, 2 * DA)
    va = va.reshape(b, t, H_A, DVA)
    qb = rms(qb.reshape(b, t, H_B, DB), qn_b)
    kb = rms(kb.reshape(b, t, H_B, DB), kn_b)
    vb = vb.reshape(b, t, H_B, DB)
    return qa, ka, va, qb, kb, vb


def diff_attention(qa, k, v, q_pos, k_pos, lam):
    q1 = qa[..., 0, :]
    q2 = qa[..., 1, :]
    k1 = k[..., :DA]
    k2 = k[..., DA:]
    visible = (k_pos[None, :] // CHUNK) <= (q_pos[:, None] // CHUNK)
    dist = jnp.abs(q_pos[:, None] - k_pos[None, :]).astype(jnp.float32)
    bias = jnp.where(visible[None], -alibi_slopes()[:, None, None] * dist[None], NEG_INF)
    scale = DA ** -0.5
    p1 = jax.nn.softmax(jnp.einsum('bqhd,bkhd->bhqk', q1, k1) * scale + bias, axis=-1)
    p2 = jax.nn.softmax(jnp.einsum('bqhd,bkhd->bhqk', q2, k2) * scale + bias, axis=-1)
    return jnp.einsum('bhqk,bkhe->bqhe', p1 - lam * p2, v)


def band_attention(q, k, v, q_pos, k_pos, rel_bias):
    dchunk = q_pos[:, :, None] // CHUNK - k_pos[:, None, :] // CHUNK
    visible = (k_pos[:, None, :] >= 0) & (dchunk >= 0) & (dchunk <= BAND_CHUNKS)
    rel = jnp.clip(q_pos[:, :, None] - k_pos[:, None, :], -REL_CLIP, REL_CLIP) + REL_CLIP
    bias = jnp.where(visible[None], rel_bias.astype(jnp.float32)[:, rel], NEG_INF)
    s = jnp.einsum('bnqhd,bnkhd->bhnqk', q, k) * (DB ** -0.5) + bias
    p = jax.nn.softmax(s, axis=-1)
    return jnp.einsum('bhnqk,bnkhd->bnqhd', p, v)


def merge_and_ffn(x, h, oa, ob, lam_init, subln_g, w_br_a, w_br_b, w_gate, w_out, norm2_g, w_ff1, w_ff2):
    b, t = x.shape[:2]
    ya = (rms(oa, subln_g) * (1.0 - lam_init)).reshape(b, t, W_A_V) @ w_br_a
    yb = ob.reshape(b, t, W_B) @ w_br_b
    ga, gb = jnp.split(jax.nn.sigmoid(h @ w_gate), 2, axis=-1)
    x = x + (ga * ya + gb * yb) @ w_out
    u = jnp.maximum(rms(x, norm2_g) @ w_ff1, 0.0)
    return x + (u * u) @ w_ff2


def setup_inputs(seed: int = 0) -> dict:
    key = jax.random.key(seed)
    ks = jax.random.split(key, 25)
    f32 = jnp.float32
    nrm = lambda k, shape, scale: jax.random.normal(k, shape, f32) * scale
    gain = lambda k, shape: 1.0 + 0.02 * jax.random.normal(k, shape, f32)
    b_keep = min(BAND_PAST, PAST_LEN)
    return {
        'x_prompt': nrm(ks[0], (BATCH, SEQ, D_MODEL), 1.0),
        'x_sample': nrm(ks[1], (DEC_BATCH, DEC_SEQ, D_MODEL), 1.0),
        'cache_a_k': nrm(ks[2], (DEPTH, DEC_BATCH, PAST_LEN, H_A, 2 * DA), 1.0),
        'cache_a_v': nrm(ks[3], (DEPTH, DEC_BATCH, PAST_LEN, H_A, DVA), 1.0),
        'cache_b_k': nrm(ks[4], (DEPTH, DEC_BATCH, b_keep, H_B, DB), 1.0),
        'cache_b_v': nrm(ks[5], (DEPTH, DEC_BATCH, b_keep, H_B, DB), 1.0),
        'norm1_g': gain(ks[6], (DEPTH, D_MODEL)),
        'w_in': nrm(ks[7], (DEPTH, D_MODEL, IN_COLS), D_MODEL ** -0.5),
        'qn_a_g': gain(ks[8], (DEPTH, DA)),
        'kn_a_g': gain(ks[9], (DEPTH, DA)),
        'qn_b_g': gain(ks[10], (DEPTH, DB)),
        'kn_b_g': gain(ks[11], (DEPTH, DB)),
        'lam_q1': nrm(ks[12], (DEPTH, DA), 0.1),
        'lam_k1': nrm(ks[13], (DEPTH, DA), 0.1),
        'lam_q2': nrm(ks[14], (DEPTH, DA), 0.1),
        'lam_k2': nrm(ks[15], (DEPTH, DA), 0.1),
        'subln_a_g': gain(ks[16], (DEPTH, DVA)),
        'rel_bias_b': nrm(ks[17], (DEPTH, H_B, 2 * REL_CLIP + 1), 0.5),
        'w_br_a': nrm(ks[18], (DEPTH, W_A_V, D_MODEL), W_A_V ** -0.5),
        'w_br_b': nrm(ks[19], (DEPTH, W_B, D_MODEL), W_B ** -0.5),
        'w_gate': nrm(ks[20], (DEPTH, D_MODEL, 2 * D_MODEL), D_MODEL ** -0.5),
        'w_out': nrm(ks[21], (DEPTH, D_MODEL, D_MODEL), D_MODEL ** -0.5),
        'norm2_g': gain(ks[22], (DEPTH, D_MODEL)),
        'w_ff1': nrm(ks[23], (DEPTH, D_MODEL, D_FF), D_MODEL ** -0.5),
        'w_ff2': nrm(ks[24], (DEPTH, D_FF, D_MODEL), D_FF ** -0.5),
    }


def reference(x_prompt, x_sample, cache_a_k, cache_a_v, cache_b_k, cache_b_v,
              norm1_g, w_in, qn_a_g, kn_a_g, qn_b_g, kn_b_g,
              lam_q1, lam_k1, lam_q2, lam_k2, subln_a_g, rel_bias_b,
              w_br_a, w_br_b, w_gate, w_out, norm2_g, w_ff1, w_ff2):
    f32 = jnp.float32
    bp, s = x_prompt.shape[:2]
    t = x_sample.shape[1]
    p_len = cache_a_k.shape[2]
    b_keep = cache_b_k.shape[2]
    nb = s // Q_BLOCK
    nc = s // CHUNK
    keep_p = min(BAND_PAST, s)

    pos_p = jnp.arange(s)
    pos_blocks = pos_p.reshape(nb, Q_BLOCK)
    band_idx = jnp.arange(nc)[:, None] * CHUNK + jnp.arange(BAND)[None, :]
    band_kpos = band_idx - BAND_PAST
    pos_chunks = pos_p.reshape(nc, CHUNK)
    q_pos_s = p_len + jnp.arange(t)
    k_pos_a_s = jnp.arange(p_len + t)
    k_pos_b_s = jnp.arange(p_len - b_keep, p_len + t)
    pad_rows = lambda r: jnp.pad(r, ((0, 0), (BAND_PAST, 0), (0, 0), (0, 0)))

    xp = x_prompt.astype(f32)
    xs = x_sample.astype(f32)
    ak_p, av_p, bk_p, bv_p = [], [], [], []
    ak_s, av_s, bk_s, bv_s = [], [], [], []
    for l in range(DEPTH):
        lam_init = 0.8 - 0.6 * math.exp(-0.3 * l)
        lam = (jnp.exp(jnp.sum(lam_q1[l].astype(f32) * lam_k1[l].astype(f32)))
               - jnp.exp(jnp.sum(lam_q2[l].astype(f32) * lam_k2[l].astype(f32))) + lam_init)
        w_in_l = w_in[l].astype(f32)
        tail = functools_partial_args = (lam_init, subln_a_g[l], w_br_a[l].astype(f32), w_br_b[l].astype(f32),
                                         w_gate[l].astype(f32), w_out[l].astype(f32), norm2_g[l],
                                         w_ff1[l].astype(f32), w_ff2[l].astype(f32))

        hp = rms(xp, norm1_g[l])
        qa, ka, va, qb, kb, vb = project(hp, w_in_l, qn_a_g[l], kn_a_g[l], qn_b_g[l], kn_b_g[l])
        qa_blocks = qa.reshape(bp, nb, Q_BLOCK, H_A, 2, DA).swapaxes(0, 1)
        oa = lax.map(lambda a: diff_attention(a[0], ka, va, a[1], pos_p, lam), (qa_blocks, pos_blocks))
        oa = oa.swapaxes(0, 1).reshape(bp, s, H_A, DVA)
        ob = band_attention(qb.reshape(bp, nc, CHUNK, H_B, DB), pad_rows(kb)[:, band_idx],
                            pad_rows(vb)[:, band_idx], pos_chunks, band_kpos, rel_bias_b[l])
        ob = ob.reshape(bp, s, H_B, DB)
        xp = merge_and_ffn(xp, hp, oa, ob, *tail)
        ak_p.append(ka)
        av_p.append(va)
        bk_p.append(kb[:, s - keep_p:])
        bv_p.append(vb[:, s - keep_p:])

        hs = rms(xs, norm1_g[l])
        qa2, ka2, va2, qb2, kb2, vb2 = project(hs, w_in_l, qn_a_g[l], kn_a_g[l], qn_b_g[l], kn_b_g[l])
        ka_all = jnp.concatenate([cache_a_k[l].astype(f32), ka2], axis=1)
        va_all = jnp.concatenate([cache_a_v[l].astype(f32), va2], axis=1)
        oa2 = diff_attention(qa2, ka_all, va_all, q_pos_s, k_pos_a_s, lam)
        kb_all = jnp.concatenate([cache_b_k[l].astype(f32), kb2], axis=1)
        vb_all = jnp.concatenate([cache_b_v[l].astype(f32), vb2], axis=1)
        ob2 = band_attention(qb2[:, None], kb_all[:, None], vb_all[:, None],
                             q_pos_s[None], k_pos_b_s[None], rel_bias_b[l])[:, 0]
        xs = merge_and_ffn(xs, hs, oa2, ob2, *tail)
        ak_s.append(ka2)
        av_s.append(va2)
        bk_s.append(kb2)
        bv_s.append(vb2)

    cdt = cache_a_k.dtype
    y_prompt = xp.astype(x_prompt.dtype)
    y_sample = xs.astype(x_sample.dtype)
    return (y_prompt, y_sample,
            jnp.stack(ak_p).astype(cdt), jnp.stack(av_p).astype(cdt),
            jnp.stack(bk_p).astype(cdt), jnp.stack(bv_p).astype(cdt),
            jnp.stack(ak_s).astype(cdt), jnp.stack(av_s).astype(cdt),
            jnp.stack(bk_s).astype(cdt), jnp.stack(bv_s).astype(cdt))
```

```python
import functools
import math

import numpy as np
import jax
import jax.numpy as jnp
from jax import lax
from jax.experimental import pallas as pl
from jax.experimental.pallas import tpu as pltpu

F32 = jnp.float32
BF16 = jnp.bfloat16

D_MODEL = 1024
CHUNK = 64
H_A = 8
DA = 64
DVA = 2 * DA
H_B = 8
DB = 64
BAND_CHUNKS = 8
BAND_PAST = BAND_CHUNKS * CHUNK
REL_CLIP = 128
D_FF = 4 * D_MODEL
EPS = 1e-6
NEG_INF = -1e30
W_A = H_A * 2 * DA
W_B = H_B * DB

LANES = 128
NORM_CHUNK = 256
VMEM_LIMIT = 56 * 1024 * 1024

PROJ_ROWS = 256
ATT_BLOCK = 512
BAND_BLOCK = 512


def _rms_rows(x, gain_row):
    return x * lax.rsqrt(jnp.mean(x * x, axis=-1, keepdims=True) + EPS) * gain_row


def _dot(a, b):
    return jnp.dot(a, b, preferred_element_type=F32)


def _dot_nt(a, b):
    return lax.dot_general(a, b, (((1,), (1,)), ((), ())), preferred_element_type=F32)


def _head_norm(z, group_ones, gain_row):
    outs = []
    for c in range(z.shape[1] // NORM_CHUNK):
        zc = z[:, c * NORM_CHUNK:(c + 1) * NORM_CHUNK]
        ss = _dot((zc * zc).astype(BF16), group_ones)
        outs.append(zc * lax.rsqrt(ss * (1.0 / DA) + EPS))
    return jnp.concatenate(outs, axis=1) * gain_row


def _proj_kernel(x_ref, g1_ref, w_ref, ones_ref, gqa_ref, gka_ref, gqb_ref, gkb_ref,
                 qa_ref, kaf_ref, kab_ref, vaf_ref, vab_ref,
                 qb_ref, kbf_ref, kbb_ref, vbf_ref, vbb_ref):
    hb = _rms_rows(x_ref[...], g1_ref[...]).astype(BF16)
    ones = ones_ref[...]

    qa = _head_norm(_dot(hb, w_ref[:, 0:W_A]), ones, gqa_ref[...])
    qa_ref[...] = qa.astype(BF16)
    ka = _head_norm(_dot(hb, w_ref[:, W_A:2 * W_A]), ones, gka_ref[...])
    kaf_ref[...] = ka
    kab_ref[...] = ka.astype(BF16)
    va = _dot(hb, w_ref[:, 2 * W_A:3 * W_A])
    vaf_ref[...] = va
    vab_ref[...] = va.astype(BF16)

    o = 3 * W_A
    qb = _head_norm(_dot(hb, w_ref[:, o:o + W_B]), ones, gqb_ref[...])
    qb_ref[...] = qb.astype(BF16)
    kb = _head_norm(_dot(hb, w_ref[:, o + W_B:o + 2 * W_B]), ones, gkb_ref[...])
    kbf_ref[...] = kb
    kbb_ref[...] = kb.astype(BF16)
    vb = _dot(hb, w_ref[:, o + 2 * W_B:o + 3 * W_B])
    vbf_ref[...] = vb
    vbb_ref[...] = vb.astype(BF16)


def _const_spec(shape):
    return pl.BlockSpec(shape, lambda *_: (0,) * len(shape), pipeline_mode=pl.Buffered(1))


def _projection(x2d, g1, w_bf, ones, gqa, gka, gqb, gkb):
    n = x2d.shape[0]
    tm = min(PROJ_ROWS, n)
    assert n % tm == 0
    row = lambda width: pl.BlockSpec((tm, width), lambda i: (i, 0))
    out_shapes = []
    out_specs = []
    for width, dtypes in ((W_A, (BF16,)), (W_A, (F32, BF16)), (W_A, (F32, BF16)),
                          (W_B, (BF16,)), (W_B, (F32, BF16)), (W_B, (F32, BF16))):
        for dt in dtypes:
            out_shapes.append(jax.ShapeDtypeStruct((n, width), dt))
            out_specs.append(row(width))
    return pl.pallas_call(
        _proj_kernel,
        grid=(n // tm,),
        in_specs=[row(D_MODEL), _const_spec((1, D_MODEL)), _const_spec(w_bf.shape),
                  _const_spec(ones.shape), _const_spec((1, W_A)), _const_spec((1, W_A)),
                  _const_spec((1, W_B)), _const_spec((1, W_B))],
        out_specs=out_specs,
        out_shape=out_shapes,
        compiler_params=pltpu.CompilerParams(dimension_semantics=("parallel",),
                                             vmem_limit_bytes=VMEM_LIMIT),
        name="projection",
    )(x2d, g1, w_bf, ones, gqa, gka, gqb, gkb)


def _split_maps(q):
    lane = lax.broadcasted_iota(jnp.int32, q.shape, 1)
    zero = jnp.zeros_like(q)
    return jnp.concatenate([jnp.where(lane < DA, q, zero), jnp.where(lane >= DA, q, zero)], axis=0)


def _lambda(lamv, lam_init):
    a = jnp.sum(lamv[0:1, :] * lamv[1:2, :], axis=1, keepdims=True)
    b = jnp.sum(lamv[2:3, :] * lamv[3:4, :], axis=1, keepdims=True)
    return jnp.exp(a) - jnp.exp(b) + lam_init


def _dattn_kernel(qi_ref, kj_ref, slopes_ref, lamv_ref, q_ref, k_ref, v_ref, o_ref,
                  qz_sc, m_sc, l_sc, acc_sc, *, blk, lam_init):
    h = pl.program_id(1)
    p = pl.program_id(2)
    i = qi_ref[p]
    j = kj_ref[p]
    slope = slopes_ref[h]

    @pl.when(j == 0)
    def _init():
        qz_sc[...] = _split_maps(q_ref[0])
        m_sc[...] = jnp.full(m_sc.shape, -jnp.inf, F32)
        l_sc[...] = jnp.zeros(l_sc.shape, F32)
        acc_sc[...] = jnp.zeros(acc_sc.shape, F32)

    def update(s):
        m_prev = m_sc[...]
        m_new = jnp.maximum(m_prev, jnp.max(s, axis=1, keepdims=True))
        alpha = jnp.exp(m_prev - m_new)
        pm = jnp.exp(s - m_new)
        l_sc[...] = alpha * l_sc[...] + jnp.sum(pm, axis=1, keepdims=True)
        acc_sc[...] = alpha * acc_sc[...] + _dot(pm.astype(BF16), v_ref[0])
        m_sc[...] = m_new

    @pl.when(j < i)
    def _below():
        s = _dot_nt(qz_sc[...], k_ref[0])
        col = lax.broadcasted_iota(jnp.int32, (1, blk), 1) + (j - i) * blk
        update(s + slope * col.astype(F32))

    @pl.when(j == i)
    def _diagonal():
        s = _dot_nt(qz_sc[...], k_ref[0])
        qi = lax.broadcasted_iota(jnp.int32, (blk, blk), 0)
        ki = lax.broadcasted_iota(jnp.int32, (blk, blk), 1)
        visible = (ki // CHUNK) <= (qi // CHUNK)
        bias = slope * (qi - jnp.abs(qi - ki)).astype(F32)
        s = s.reshape(2, blk, blk) + bias[None]
        s = jnp.where(visible[None], s, NEG_INF).reshape(2 * blk, blk)
        update(s)
        out = acc_sc[...] / l_sc[...]
        lam = _lambda(lamv_ref[...], lam_init)
        o_ref[0] = out[0:blk] - lam * out[blk:2 * blk]


def _diff_attention_prompt(qa, kab, vab, slopes, lamv, lam_init):
    b, s, _ = qa.shape
    blk = min(ATT_BLOCK, s)
    assert s % blk == 0 and blk % CHUNK == 0
    nq = s // blk
    pairs = [(i, j) for i in range(nq) for j in range(i + 1)]
    qi = jnp.asarray(np.array([p[0] for p in pairs], np.int32))
    kj = jnp.asarray(np.array([p[1] for p in pairs], np.int32))
    grid_spec = pltpu.PrefetchScalarGridSpec(
        num_scalar_prefetch=2,
        grid=(b, H_A, len(pairs)),
        in_specs=[
            pl.BlockSpec(memory_space=pltpu.SMEM),
            pl.BlockSpec((4, DA), lambda bb, h, p, qi_, kj_: (0, 0)),
            pl.BlockSpec((1, blk, DVA), lambda bb, h, p, qi_, kj_: (bb, qi_[p], h)),
            pl.BlockSpec((1, blk, DVA), lambda bb, h, p, qi_, kj_: (bb, kj_[p], h)),
            pl.BlockSpec((1, blk, DVA), lambda bb, h, p, qi_, kj_: (bb, kj_[p], h)),
        ],
        out_specs=pl.BlockSpec((1, blk, DVA), lambda bb, h, p, qi_, kj_: (bb, qi_[p], h)),
        scratch_shapes=[pltpu.VMEM((2 * blk, DVA), BF16), pltpu.VMEM((2 * blk, 1), F32),
                        pltpu.VMEM((2 * blk, 1), F32), pltpu.VMEM((2 * blk, DVA), F32)],
    )
    return pl.pallas_call(
        functools.partial(_dattn_kernel, blk=blk, lam_init=lam_init),
        grid_spec=grid_spec,
        out_shape=jax.ShapeDtypeStruct((b, s, W_A), F32),
        compiler_params=pltpu.CompilerParams(
            dimension_semantics=("parallel", "parallel", "arbitrary"), vmem_limit_bytes=VMEM_LIMIT),
        name="diff_attention",
    )(qi, kj, slopes, lamv, qa, kab, vab)


def _dattn_sample_kernel(slopes_ref, lamv_ref, q_ref, kc_ref, kn_ref, vc_ref, vn_ref, o_ref,
                         kpad_sc, vpad_sc, *, t, p_len, lam_init):
    h = pl.program_id(1)
    slope = slopes_ref[h]
    qz = _split_maps(q_ref[0])
    kpad_sc[...] = jnp.zeros(kpad_sc.shape, BF16)
    vpad_sc[...] = jnp.zeros(vpad_sc.shape, BF16)
    kpad_sc[0:t, :] = kn_ref[0]
    vpad_sc[0:t, :] = vn_ref[0]

    def bias(k0, nk, n_valid):
        qpos = lax.broadcasted_iota(jnp.int32, (t, nk), 0) + p_len
        kidx = lax.broadcasted_iota(jnp.int32, (t, nk), 1)
        kpos = kidx + k0
        visible = ((kpos // CHUNK) <= (qpos // CHUNK)) & (kidx < n_valid)
        return jnp.where(visible, -slope * jnp.abs(qpos - kpos).astype(F32), NEG_INF)

    def scores(k, bias_t):
        s = _dot_nt(qz, k)
        return (s.reshape(2, t, -1) + bias_t[None]).reshape(2 * t, -1)

    s_c = scores(kc_ref[0].astype(BF16), bias(0, p_len, p_len))
    s_n = scores(kpad_sc[...], bias(p_len, LANES, t))
    m = jnp.maximum(jnp.max(s_c, axis=1, keepdims=True), jnp.max(s_n, axis=1, keepdims=True))
    p_c = jnp.exp(s_c - m)
    p_n = jnp.exp(s_n - m)
    l = jnp.sum(p_c, axis=1, keepdims=True) + jnp.sum(p_n, axis=1, keepdims=True)
    out = (_dot(p_c.astype(BF16), vc_ref[0].astype(BF16)) + _dot(p_n.astype(BF16), vpad_sc[...])) / l
    lam = _lambda(lamv_ref[...], lam_init)
    o_ref[0] = out[0:t] - lam * out[t:2 * t]


def _diff_attention_sample(qa, kab, vab, cache_k, cache_v, slopes, lamv, lam_init):
    b, t, _ = qa.shape
    p_len = cache_k.shape[1]
    assert t <= LANES
    new = pl.BlockSpec((1, t, DVA), lambda bb, h: (bb, 0, h))
    old = pl.BlockSpec((1, p_len, DVA), lambda bb, h: (bb, 0, h))
    return pl.pallas_call(
        functools.partial(_dattn_sample_kernel, t=t, p_len=p_len, lam_init=lam_init),
        grid=(b, H_A),
        in_specs=[pl.BlockSpec(memory_space=pltpu.SMEM), pl.BlockSpec((4, DA), lambda bb, h: (0, 0)),
                  new, old, new, old, new],
        out_specs=new,
        out_shape=jax.ShapeDtypeStruct((b, t, W_A), F32),
        scratch_shapes=[pltpu.VMEM((LANES, DVA), BF16), pltpu.VMEM((LANES, DVA), BF16)],
        compiler_params=pltpu.CompilerParams(dimension_semantics=("parallel", "parallel"),
                                             vmem_limit_bytes=VMEM_LIMIT),
        name="diff_attention_sample",
    )(slopes, lamv, qa, cache_k, kab, cache_v, vab)


def _pair_masks(q):
    lane = lax.broadcasted_iota(jnp.int32, q.shape, 1)
    zero = jnp.zeros_like(q)
    return lane, (jnp.where(lane < DB, q, zero), jnp.where(lane >= DB, q, zero))


def _battn_kernel(q_ref, kp_ref, kc_ref, vp_ref, vc_ref, bias_ref, o_ref, *, blk):
    i = pl.program_id(2)
    lane, qz = _pair_masks(q_ref[0])
    outs = []
    for e in range(2):
        bias = bias_ref[e]
        s_p = jnp.where(i > 0, _dot_nt(qz[e], kp_ref[0]) + bias[:, 0:blk], NEG_INF)
        s_c = _dot_nt(qz[e], kc_ref[0]) + bias[:, blk:2 * blk]
        m = jnp.maximum(jnp.max(s_p, axis=1, keepdims=True), jnp.max(s_c, axis=1, keepdims=True))
        p_p = jnp.exp(s_p - m)
        p_c = jnp.exp(s_c - m)
        l = jnp.sum(p_p, axis=1, keepdims=True) + jnp.sum(p_c, axis=1, keepdims=True)
        outs.append((_dot(p_p.astype(BF16), vp_ref[0]) + _dot(p_c.astype(BF16), vc_ref[0])) / l)
    o_ref[0] = jnp.where(lane < DB, outs[0], outs[1])


def _band_attention_prompt(qb, kbb, vbb, bias):
    b, s, _ = qb.shape
    blk = BAND_BLOCK
    assert s % blk == 0 and blk == BAND_PAST
    cur = pl.BlockSpec((1, blk, LANES), lambda pr, bb, i: (bb, i, pr))
    prev = pl.BlockSpec((1, blk, LANES), lambda pr, bb, i: (bb, jnp.maximum(i - 1, 0), pr))
    return pl.pallas_call(
        functools.partial(_battn_kernel, blk=blk),
        grid=(H_B // 2, b, s // blk),
        in_specs=[cur, prev, cur, prev, cur,
                  pl.BlockSpec((2, blk, 2 * blk), lambda pr, bb, i: (pr, 0, 0))],
        out_specs=cur,
        out_shape=jax.ShapeDtypeStruct((b, s, W_B), F32),
        compiler_params=pltpu.CompilerParams(
            dimension_semantics=("parallel", "parallel", "parallel"), vmem_limit_bytes=VMEM_LIMIT),
        name="band_attention",
    )(qb, kbb, kbb, vbb, vbb, bias)


def _battn_sample_kernel(q_ref, kc_ref, kn_ref, vc_ref, vn_ref, biasc_ref, biasn_ref, o_ref,
                         kpad_sc, vpad_sc, *, t):
    kpad_sc[...] = jnp.zeros(kpad_sc.shape, BF16)
    vpad_sc[...] = jnp.zeros(vpad_sc.shape, BF16)
    kpad_sc[0:t, :] = kn_ref[0]
    vpad_sc[0:t, :] = vn_ref[0]
    pieces = []
    for pr in range(H_B // 2):
        cols = slice(pr * LANES, (pr + 1) * LANES)
        lane, qz = _pair_masks(q_ref[0, :, cols])
        kc = kc_ref[0, :, cols].astype(BF16)
        vc = vc_ref[0, :, cols].astype(BF16)
        kn = kpad_sc[:, cols]
        vn = vpad_sc[:, cols]
        outs = []
        for e in range(2):
            hd = 2 * pr + e
            s_c = _dot_nt(qz[e], kc) + biasc_ref[hd]
            s_n = _dot_nt(qz[e], kn) + biasn_ref[hd]
            m = jnp.maximum(jnp.max(s_c, axis=1, keepdims=True), jnp.max(s_n, axis=1, keepdims=True))
            p_c = jnp.exp(s_c - m)
            p_n = jnp.exp(s_n - m)
            l = jnp.sum(p_c, axis=1, keepdims=True) + jnp.sum(p_n, axis=1, keepdims=True)
            outs.append((_dot(p_c.astype(BF16), vc) + _dot(p_n.astype(BF16), vn)) / l)
        pieces.append(jnp.where(lane < DB, outs[0], outs[1]))
    o_ref[0] = jnp.concatenate(pieces, axis=1)


def _band_attention_sample(qb, kbb, vbb, cache_k, cache_v, bias_c, bias_n):
    b, t, _ = qb.shape
    keep = cache_k.shape[1]
    new = pl.BlockSpec((1, t, W_B), lambda bb: (bb, 0, 0))
    old = pl.BlockSpec((1, keep, W_B), lambda bb: (bb, 0, 0))
    return pl.pallas_call(
        functools.partial(_battn_sample_kernel, t=t),
        grid=(b,),
        in_specs=[new, old, new, old, new, _const_spec(bias_c.shape), _const_spec(bias_n.shape)],
        out_specs=new,
        out_shape=jax.ShapeDtypeStruct((b, t, W_B), F32),
        scratch_shapes=[pltpu.VMEM((LANES, W_B), BF16), pltpu.VMEM((LANES, W_B), BF16)],
        compiler_params=pltpu.CompilerParams(dimension_semantics=("parallel",),
                                             vmem_limit_bytes=VMEM_LIMIT),
        name="band_attention_sample",
    )(qb, cache_k, kbb, cache_v, vbb, bias_c, bias_n)


def _band_bias(rel_bias, q0, nq, k0, nk, n_valid):
    qpos = q0 + np.arange(nq)[:, None]
    kidx = np.arange(nk)[None, :]
    kpos = k0 + kidx
    dchunk = qpos // CHUNK - kpos // CHUNK
    visible = (kpos >= 0) & (dchunk >= 0) & (dchunk <= BAND_CHUNKS) & (kidx < n_valid)
    rel = np.clip(qpos - kpos, -REL_CLIP, REL_CLIP) + REL_CLIP
    return jnp.where(jnp.asarray(visible)[None], rel_bias.astype(F32)[:, rel], NEG_INF)


def _merge_ffn_kernel(x_ref, oa_ref, ob_ref, g1_ref, gs_ref, wa_ref, wb_ref, wg_ref, wo_ref,
                      g2_ref, w1_ref, w2_ref, y_ref):
    x = x_ref[...]
    hb = _rms_rows(x, g1_ref[...]).astype(BF16)
    oa = oa_ref[...]
    heads = [_rms_rows(oa[:, c * DVA:(c + 1) * DVA], 1.0) for c in range(H_A)]
    oan = (jnp.concatenate(heads, axis=1) * gs_ref[...]).astype(BF16)
    ya = _dot(oan, wa_ref[...])
    yb = _dot(ob_ref[...].astype(BF16), wb_ref[...])
    gates = jax.nn.sigmoid(_dot(hb, wg_ref[...]))
    mix = gates[:, 0:D_MODEL] * ya + gates[:, D_MODEL:2 * D_MODEL] * yb
    x1 = x + _dot(mix.astype(BF16), wo_ref[...])
    h2 = _rms_rows(x1, g2_ref[...]).astype(BF16)
    u = jnp.maximum(_dot(h2, w1_ref[...]), 0.0)
    y_ref[...] = x1 + _dot((u * u).astype(BF16), w2_ref[...])


def _merge_ffn(x2d, oa2d, ob2d, g1, gs, wa, wb, wg, wo, g2, w1, w2):
    n = x2d.shape[0]
    tm = min(PROJ_ROWS, n)
    assert n % tm == 0
    row = lambda width: pl.BlockSpec((tm, width), lambda i: (i, 0))
    consts = (g1, gs, wa, wb, wg, wo, g2, w1, w2)
    return pl.pallas_call(
        _merge_ffn_kernel,
        grid=(n // tm,),
        in_specs=[row(D_MODEL), row(W_A), row(W_B)] + [_const_spec(c.shape) for c in consts],
        out_specs=row(D_MODEL),
        out_shape=jax.ShapeDtypeStruct((n, D_MODEL), F32),
        compiler_params=pltpu.CompilerParams(dimension_semantics=("parallel",),
                                             vmem_limit_bytes=VMEM_LIMIT),
        name="merge_ffn",
    )(x2d, oa2d, ob2d, *consts)


def kernel(x_prompt, x_sample, cache_a_k, cache_a_v, cache_b_k, cache_b_v, norm1_g, w_in, qn_a_g, kn_a_g, qn_b_g, kn_b_g, lam_q1, lam_k1, lam_q2, lam_k2, subln_a_g, rel_bias_b, w_br_a, w_br_b, w_gate, w_out, norm2_g, w_ff1, w_ff2):
    bp, s, d = x_prompt.shape
    bs, t, _ = x_sample.shape
    depth = w_in.shape[0]
    p_len = cache_a_k.shape[2]
    b_keep = cache_b_k.shape[2]
    keep_p = min(BAND_PAST, s)
    cdt = cache_a_k.dtype

    slopes = jnp.exp2(-8.0 * jnp.arange(1, H_A + 1, dtype=F32) / H_A)
    group = np.arange(NORM_CHUNK) // DA
    ones = jnp.asarray(group[:, None] == group[None, :], BF16)
    row = lambda v, reps, scale=1.0: (jnp.tile(v.astype(F32), reps) * scale)[None, :]

    xp = x_prompt.astype(F32).reshape(bp * s, d)
    xs = x_sample.astype(F32).reshape(bs * t, d)
    outs = [[] for _ in range(8)]
    for l in range(depth):
        lam_init = 0.8 - 0.6 * math.exp(-0.3 * l)
        lamv = jnp.stack([lam_q1[l], lam_k1[l], lam_q2[l], lam_k2[l]]).astype(F32)
        g1 = row(norm1_g[l], 1)
        proj_consts = (g1, w_in[l].astype(BF16), ones,
                       row(qn_a_g[l], 2 * H_A, DA ** -0.5), row(kn_a_g[l], 2 * H_A),
                       row(qn_b_g[l], H_B, DB ** -0.5), row(kn_b_g[l], H_B))
        tail = (g1, row(subln_a_g[l], H_A, 1.0 - lam_init), w_br_a[l].astype(BF16),
                w_br_b[l].astype(BF16), w_gate[l].astype(BF16), w_out[l].astype(BF16),
                row(norm2_g[l], 1), w_ff1[l].astype(BF16), w_ff2[l].astype(BF16))

        qa, kaf, kab, vaf, vab, qb, kbf, kbb, vbf, vbb = _projection(xp, *proj_consts)
        r3 = lambda a, n: a.reshape(n, -1, a.shape[-1])
        oa = _diff_attention_prompt(r3(qa, bp), r3(kab, bp), r3(vab, bp), slopes, lamv, lam_init)
        bias = _band_bias(rel_bias_b[l], BAND_PAST, BAND_BLOCK, 0, 2 * BAND_BLOCK, 2 * BAND_BLOCK)
        ob = _band_attention_prompt(r3(qb, bp), r3(kbb, bp), r3(vbb, bp), bias)
        xp = _merge_ffn(xp, oa.reshape(bp * s, W_A), ob.reshape(bp * s, W_B), *tail)
        outs[0].append(kaf.reshape(bp, s, H_A, 2 * DA))
        outs[1].append(vaf.reshape(bp, s, H_A, DVA))
        outs[2].append(kbf.reshape(bp, s, H_B, DB)[:, s - keep_p:])
        outs[3].append(vbf.reshape(bp, s, H_B, DB)[:, s - keep_p:])

        qa, kaf, kab, vaf, vab, qb, kbf, kbb, vbf, vbb = _projection(xs, *proj_consts)
        oa = _diff_attention_sample(
            r3(qa, bs), r3(kab, bs), r3(vab, bs),
            cache_a_k[l].reshape(bs, p_len, W_A), cache_a_v[l].reshape(bs, p_len, W_A),
            slopes, lamv, lam_init)
        bias_c = _band_bias(rel_bias_b[l], p_len, t, p_len - b_keep, b_keep, b_keep)
        bias_n = _band_bias(rel_bias_b[l], p_len, t, p_len, LANES, t)
        ob = _band_attention_sample(
            r3(qb, bs), r3(kbb, bs), r3(vbb, bs),
            cache_b_k[l].reshape(bs, b_keep, W_B), cache_b_v[l].reshape(bs, b_keep, W_B),
            bias_c, bias_n)
        xs = _merge_ffn(xs, oa.reshape(bs * t, W_A), ob.reshape(bs * t, W_B), *tail)
        outs[4].append(kaf.reshape(bs, t, H_A, 2 * DA))
        outs[5].append(vaf.reshape(bs, t, H_A, DVA))
        outs[6].append(kbf.reshape(bs, t, H_B, DB))
        outs[7].append(vbf.reshape(bs, t, H_B, DB))

    stacked = [jnp.stack(o).astype(cdt) for o in outs]
    return (xp.reshape(bp, s, d).astype(x_prompt.dtype), xs.reshape(bs, t, d).astype(x_sample.dtype),
            *stacked)
```

```python
import functools
import math

import numpy as np
import jax
import jax.numpy as jnp
from jax import lax
from jax.experimental import pallas as pl
from jax.experimental.pallas import tpu as pltpu

F32 = jnp.float32
BF16 = jnp.bfloat16

D_MODEL = 1024
CHUNK = 64
H_A = 8
DA = 64
DVA = 2 * DA
H_B = 8
DB = 64
BAND_CHUNKS = 8
BAND_PAST = BAND_CHUNKS * CHUNK
REL_CLIP = 128
D_FF = 4 * D_MODEL
EPS = 1e-6
NEG_INF = -1e30
W_A = H_A * 2 * DA
W_B = H_B * DB

LANES = 128
NORM_CHUNK = 256
VMEM_LIMIT = 56 * 1024 * 1024

PROJ_ROWS = 256
ATT_BLOCK = 512
LOG2E = 1.4426950408889634
N_BIAS_COLS = 3
BAND_BLOCK = 512


def _rms_rows(x, gain_row):
    return x * lax.rsqrt(jnp.mean(x * x, axis=-1, keepdims=True) + EPS) * gain_row


def _dot(a, b):
    return jnp.dot(a, b, preferred_element_type=F32)


def _dot_nt(a, b):
    return lax.dot_general(a, b, (((1,), (1,)), ((), ())), preferred_element_type=F32)


def _head_norm(z, group_ones, gain_row):
    outs = []
    for c in range(z.shape[1] // NORM_CHUNK):
        zc = z[:, c * NORM_CHUNK:(c + 1) * NORM_CHUNK]
        ss = _dot((zc * zc).astype(BF16), group_ones)
        outs.append(zc * lax.rsqrt(ss * (1.0 / DA) + EPS))
    return jnp.concatenate(outs, axis=1) * gain_row


def _proj_kernel(x_ref, g1_ref, w_ref, ones_ref, gqa_ref, gka_ref, gqb_ref, gkb_ref,
                 qa_ref, kaf_ref, kab_ref, vaf_ref, vab_ref,
                 qb_ref, kbf_ref, kbb_ref, vbf_ref, vbb_ref):
    hb = _rms_rows(x_ref[...], g1_ref[...]).astype(BF16)
    ones = ones_ref[...]

    qa = _head_norm(_dot(hb, w_ref[:, 0:W_A]), ones, gqa_ref[...])
    qa_ref[...] = qa.astype(BF16)
    ka = _head_norm(_dot(hb, w_ref[:, W_A:2 * W_A]), ones, gka_ref[...])
    kaf_ref[...] = ka
    kab_ref[...] = ka.astype(BF16)
    va = _dot(hb, w_ref[:, 2 * W_A:3 * W_A])
    vaf_ref[...] = va
    vab_ref[...] = va.astype(BF16)

    o = 3 * W_A
    qb = _head_norm(_dot(hb, w_ref[:, o:o + W_B]), ones, gqb_ref[...])
    qb_ref[...] = qb.astype(BF16)
    kb = _head_norm(_dot(hb, w_ref[:, o + W_B:o + 2 * W_B]), ones, gkb_ref[...])
    kbf_ref[...] = kb
    kbb_ref[...] = kb.astype(BF16)
    vb = _dot(hb, w_ref[:, o + 2 * W_B:o + 3 * W_B])
    vbf_ref[...] = vb
    vbb_ref[...] = vb.astype(BF16)


def _const_spec(shape):
    return pl.BlockSpec(shape, lambda *_: (0,) * len(shape), pipeline_mode=pl.Buffered(1))


def _projection(x2d, g1, w_bf, ones, gqa, gka, gqb, gkb):
    n = x2d.shape[0]
    tm = min(PROJ_ROWS, n)
    assert n % tm == 0
    row = lambda width: pl.BlockSpec((tm, width), lambda i: (i, 0))
    out_shapes = []
    out_specs = []
    for width, dtypes in ((W_A, (BF16,)), (W_A, (F32, BF16)), (W_A, (F32, BF16)),
                          (W_B, (BF16,)), (W_B, (F32, BF16)), (W_B, (F32, BF16))):
        for dt in dtypes:
            out_shapes.append(jax.ShapeDtypeStruct((n, width), dt))
            out_specs.append(row(width))
    return pl.pallas_call(
        _proj_kernel,
        grid=(n // tm,),
        in_specs=[row(D_MODEL), _const_spec((1, D_MODEL)), _const_spec(w_bf.shape),
                  _const_spec(ones.shape), _const_spec((1, W_A)), _const_spec((1, W_A)),
                  _const_spec((1, W_B)), _const_spec((1, W_B))],
        out_specs=out_specs,
        out_shape=out_shapes,
        compiler_params=pltpu.CompilerParams(dimension_semantics=("parallel",),
                                             vmem_limit_bytes=VMEM_LIMIT),
        name="projection",
    )(x2d, g1, w_bf, ones, gqa, gka, gqb, gkb)


def _split_maps(q):
    lane = lax.broadcasted_iota(jnp.int32, q.shape, 1)
    zero = jnp.zeros_like(q)
    return jnp.concatenate([jnp.where(lane < DA, q, zero), jnp.where(lane >= DA, q, zero)], axis=0)


def _lambda(lamv, lam_init):
    a = jnp.sum(lamv[0:1, :] * lamv[1:2, :], axis=1, keepdims=True)
    b = jnp.sum(lamv[2:3, :] * lamv[3:4, :], axis=1, keepdims=True)
    return jnp.exp(a) - jnp.exp(b) + lam_init


def _dattn_kernel(slopes_ref, lamv_ref, qt_ref, k_ref, vt_ref, ot_ref,
                  qz_sc, kb_sc, sa_sc, sb_sc, m_sc, l_sc, acc_sc, *, blk, lam_init):
    h = pl.program_id(1)
    i = pl.program_id(2)
    sl2 = slopes_ref[h] * LOG2E

    q = qt_ref[0, 0]
    row = lax.broadcasted_iota(jnp.int32, q.shape, 0)
    zero = jnp.zeros_like(q)
    qz_sc[0:LANES, 0:blk] = jnp.where(row < DA, q, zero)
    qz_sc[0:LANES, blk:2 * blk] = jnp.where(row >= DA, q, zero)
    qz_sc[LANES:2 * LANES, :] = (
        lax.broadcasted_iota(jnp.int32, (LANES, 2 * blk), 0) < N_BIAS_COLS).astype(BF16)
    r = lax.broadcasted_iota(jnp.int32, (blk, LANES), 0)
    c = lax.broadcasted_iota(jnp.int32, (blk, LANES), 1)
    rest = sl2 * r.astype(F32)
    kb = jnp.zeros((blk, LANES), F32)
    for col in range(N_BIAS_COLS):
        part = rest.astype(BF16).astype(F32)
        kb = jnp.where(c == col, part, kb)
        rest = rest - part
    kb_sc[...] = kb.astype(BF16)

    m_sc[...] = jnp.full(m_sc.shape, -jnp.inf, F32)
    l_sc[...] = jnp.zeros(l_sc.shape, F32)
    acc_sc[...] = jnp.zeros(acc_sc.shape, F32)

    def scores(j, dst):
        kblk = k_ref[0, pl.ds(pl.multiple_of(j * blk, blk), blk), :]
        dst[...] = _dot(jnp.concatenate([kblk, kb_sc[...]], axis=1), qz_sc[...])

    def softmax_pv(j, s):
        off = sl2 * ((j - i) * blk).astype(F32)
        m_prev = m_sc[...] - off
        m_new = jnp.maximum(m_prev, jnp.max(s, axis=0, keepdims=True))
        alpha = jnp.exp2(m_prev - m_new)
        pm = jnp.exp2(s - m_new)
        l_sc[...] = alpha * l_sc[...] + jnp.sum(pm, axis=0, keepdims=True)
        pb = pm.astype(BF16)
        vblk = vt_ref[0, 0, j]
        pv = jnp.concatenate([_dot(vblk, pb[:, 0:blk]), _dot(vblk, pb[:, blk:2 * blk])], axis=1)
        acc_sc[...] = alpha * acc_sc[...] + pv
        m_sc[...] = m_new + off

    def diagonal(s):
        ki = lax.broadcasted_iota(jnp.int32, (blk, blk), 0)
        qi = lax.broadcasted_iota(jnp.int32, (blk, blk), 1)
        visible = (ki // CHUNK) <= (qi // CHUNK)
        bias = sl2 * (qi - jnp.abs(qi - ki) - ki).astype(F32)
        bias2 = jnp.concatenate([bias, bias], axis=1)
        vis2 = jnp.concatenate([visible, visible], axis=1)
        softmax_pv(i, jnp.where(vis2, s + bias2, NEG_INF))

    scores(0, sa_sc)

    def pair(jj, carry):
        j0 = 2 * jj
        scores(j0 + 1, sb_sc)
        softmax_pv(j0, sa_sc[...])
        scores(j0 + 2, sa_sc)
        softmax_pv(j0 + 1, sb_sc[...])
        return carry

    lax.fori_loop(0, i // 2, pair, 0)

    @pl.when(i % 2 == 0)
    def _even():
        diagonal(sa_sc[...])

    @pl.when(i % 2 == 1)
    def _odd():
        scores(i, sb_sc)
        softmax_pv(i - 1, sa_sc[...])
        diagonal(sb_sc[...])

    out = acc_sc[...] / l_sc[...]
    lam = _lambda(lamv_ref[...], lam_init)
    ot_ref[0, 0] = out[:, 0:blk] - lam * out[:, blk:2 * blk]


def _diff_attention_prompt(qa, kab, vab, slopes, lamv, lam_init):
    b, s, _ = qa.shape
    blk = min(ATT_BLOCK, s)
    assert s % blk == 0 and blk % CHUNK == 0
    nb = s // blk
    qt = qa.reshape(b, s, H_A, DVA).transpose(0, 2, 3, 1)
    vt = vab.reshape(b, nb, blk, H_A, DVA).transpose(0, 3, 1, 4, 2)
    ot = pl.pallas_call(
        functools.partial(_dattn_kernel, blk=blk, lam_init=lam_init),
        grid=(b, H_A, nb),
        in_specs=[
            pl.BlockSpec(memory_space=pltpu.SMEM),
            pl.BlockSpec((4, DA), lambda bb, h, i: (0, 0)),
            pl.BlockSpec((1, 1, DVA, blk), lambda bb, h, i: (bb, h, 0, i)),
            pl.BlockSpec((1, s, DVA), lambda bb, h, i: (bb, 0, h)),
            pl.BlockSpec((1, 1, nb, DVA, blk), lambda bb, h, i: (bb, h, 0, 0, 0)),
        ],
        out_specs=pl.BlockSpec((1, 1, DVA, blk), lambda bb, h, i: (bb, h, 0, i)),
        out_shape=jax.ShapeDtypeStruct((b, H_A, DVA, s), F32),
        scratch_shapes=[pltpu.VMEM((2 * LANES, 2 * blk), BF16), pltpu.VMEM((blk, LANES), BF16),
                        pltpu.VMEM((blk, 2 * blk), F32), pltpu.VMEM((blk, 2 * blk), F32),
                        pltpu.VMEM((1, 2 * blk), F32), pltpu.VMEM((1, 2 * blk), F32),
                        pltpu.VMEM((DVA, 2 * blk), F32)],
        compiler_params=pltpu.CompilerParams(
            dimension_semantics=("parallel", "parallel", "arbitrary"), vmem_limit_bytes=VMEM_LIMIT),
        name="diff_attention",
    )(slopes, lamv, qt, kab, vt)
    return ot.transpose(0, 3, 1, 2).reshape(b, s, W_A)


def _dattn_sample_kernel(slopes_ref, lamv_ref, q_ref, kc_ref, kn_ref, vc_ref, vn_ref, o_ref,
                         kpad_sc, vpad_sc, *, t, p_len, lam_init):
    h = pl.program_id(1)
    slope = slopes_ref[h] * LOG2E
    qz = _split_maps(q_ref[0])
    kpad_sc[...] = jnp.zeros(kpad_sc.shape, BF16)
    vpad_sc[...] = jnp.zeros(vpad_sc.shape, BF16)
    kpad_sc[0:t, :] = kn_ref[0]
    vpad_sc[0:t, :] = vn_ref[0]

    def bias(k0, nk, n_valid):
        qpos = lax.broadcasted_iota(jnp.int32, (t, nk), 0) + p_len
        kidx = lax.broadcasted_iota(jnp.int32, (t, nk), 1)
        kpos = kidx + k0
        visible = ((kpos // CHUNK) <= (qpos // CHUNK)) & (kidx < n_valid)
        return jnp.where(visible, -slope * jnp.abs(qpos - kpos).astype(F32), NEG_INF)

    def scores(k, bias_t):
        s = _dot_nt(qz, k)
        return (s.reshape(2, t, -1) + bias_t[None]).reshape(2 * t, -1)

    s_c = scores(kc_ref[0].astype(BF16), bias(0, p_len, p_len))
    s_n = scores(kpad_sc[...], bias(p_len, LANES, t))
    m = jnp.maximum(jnp.max(s_c, axis=1, keepdims=True), jnp.max(s_n, axis=1, keepdims=True))
    p_c = jnp.exp2(s_c - m)
    p_n = jnp.exp2(s_n - m)
    l = jnp.sum(p_c, axis=1, keepdims=True) + jnp.sum(p_n, axis=1, keepdims=True)
    out = (_dot(p_c.astype(BF16), vc_ref[0].astype(BF16)) + _dot(p_n.astype(BF16), vpad_sc[...])) / l
    lam = _lambda(lamv_ref[...], lam_init)
    o_ref[0] = out[0:t] - lam * out[t:2 * t]


def _diff_attention_sample(qa, kab, vab, cache_k, cache_v, slopes, lamv, lam_init):
    b, t, _ = qa.shape
    p_len = cache_k.shape[1]
    assert t <= LANES
    new = pl.BlockSpec((1, t, DVA), lambda bb, h: (bb, 0, h))
    old = pl.BlockSpec((1, p_len, DVA), lambda bb, h: (bb, 0, h))
    return pl.pallas_call(
        functools.partial(_dattn_sample_kernel, t=t, p_len=p_len, lam_init=lam_init),
        grid=(b, H_A),
        in_specs=[pl.BlockSpec(memory_space=pltpu.SMEM), pl.BlockSpec((4, DA), lambda bb, h: (0, 0)),
                  new, old, new, old, new],
        out_specs=new,
        out_shape=jax.ShapeDtypeStruct((b, t, W_A), F32),
        scratch_shapes=[pltpu.VMEM((LANES, DVA), BF16), pltpu.VMEM((LANES, DVA), BF16)],
        compiler_params=pltpu.CompilerParams(dimension_semantics=("parallel", "parallel"),
                                             vmem_limit_bytes=VMEM_LIMIT),
        name="diff_attention_sample",
    )(slopes, lamv, qa, cache_k, kab, cache_v, vab)


def _pair_masks(q):
    lane = lax.broadcasted_iota(jnp.int32, q.shape, 1)
    zero = jnp.zeros_like(q)
    return lane, (jnp.where(lane < DB, q, zero), jnp.where(lane >= DB, q, zero))


def _battn_kernel(q_ref, kp_ref, kc_ref, vp_ref, vc_ref, bias_ref, o_ref, *, blk):
    i = pl.program_id(2)
    lane, qz = _pair_masks(q_ref[0])
    outs = []
    for e in range(2):
        bias = bias_ref[e]
        s_p = jnp.where(i > 0, _dot_nt(qz[e], kp_ref[0]) + bias[:, 0:blk], NEG_INF)
        s_c = _dot_nt(qz[e], kc_ref[0]) + bias[:, blk:2 * blk]
        m = jnp.maximum(jnp.max(s_p, axis=1, keepdims=True), jnp.max(s_c, axis=1, keepdims=True))
        p_p = jnp.exp(s_p - m)
        p_c = jnp.exp(s_c - m)
        l = jnp.sum(p_p, axis=1, keepdims=True) + jnp.sum(p_c, axis=1, keepdims=True)
        outs.append((_dot(p_p.astype(BF16), vp_ref[0]) + _dot(p_c.astype(BF16), vc_ref[0])) / l)
    o_ref[0] = jnp.where(lane < DB, outs[0], outs[1])


def _band_attention_prompt(qb, kbb, vbb, bias):
    b, s, _ = qb.shape
    blk = BAND_BLOCK
    assert s % blk == 0 and blk == BAND_PAST
    cur = pl.BlockSpec((1, blk, LANES), lambda pr, bb, i: (bb, i, pr))
    prev = pl.BlockSpec((1, blk, LANES), lambda pr, bb, i: (bb, jnp.maximum(i - 1, 0), pr))
    return pl.pallas_call(
        functools.partial(_battn_kernel, blk=blk),
        grid=(H_B // 2, b, s // blk),
        in_specs=[cur, prev, cur, prev, cur,
                  pl.BlockSpec((2, blk, 2 * blk), lambda pr, bb, i: (pr, 0, 0))],
        out_specs=cur,
        out_shape=jax.ShapeDtypeStruct((b, s, W_B), F32),
        compiler_params=pltpu.CompilerParams(
            dimension_semantics=("parallel", "parallel", "parallel"), vmem_limit_bytes=VMEM_LIMIT),
        name="band_attention",
    )(qb, kbb, kbb, vbb, vbb, bias)


def _battn_sample_kernel(q_ref, kc_ref, kn_ref, vc_ref, vn_ref, biasc_ref, biasn_ref, o_ref,
                         kpad_sc, vpad_sc, *, t):
    kpad_sc[...] = jnp.zeros(kpad_sc.shape, BF16)
    vpad_sc[...] = jnp.zeros(vpad_sc.shape, BF16)
    kpad_sc[0:t, :] = kn_ref[0]
    vpad_sc[0:t, :] = vn_ref[0]
    pieces = []
    for pr in range(H_B // 2):
        cols = slice(pr * LANES, (pr + 1) * LANES)
        lane, qz = _pair_masks(q_ref[0, :, cols])
        kc = kc_ref[0, :, cols].astype(BF16)
        vc = vc_ref[0, :, cols].astype(BF16)
        kn = kpad_sc[:, cols]
        vn = vpad_sc[:, cols]
        outs = []
        for e in range(2):
            hd = 2 * pr + e
            s_c = _dot_nt(qz[e], kc) + biasc_ref[hd]
            s_n = _dot_nt(qz[e], kn) + biasn_ref[hd]
            m = jnp.maximum(jnp.max(s_c, axis=1, keepdims=True), jnp.max(s_n, axis=1, keepdims=True))
            p_c = jnp.exp(s_c - m)
            p_n = jnp.exp(s_n - m)
            l = jnp.sum(p_c, axis=1, keepdims=True) + jnp.sum(p_n, axis=1, keepdims=True)
            outs.append((_dot(p_c.astype(BF16), vc) + _dot(p_n.astype(BF16), vn)) / l)
        pieces.append(jnp.where(lane < DB, outs[0], outs[1]))
    o_ref[0] = jnp.concatenate(pieces, axis=1)


def _band_attention_sample(qb, kbb, vbb, cache_k, cache_v, bias_c, bias_n):
    b, t, _ = qb.shape
    keep = cache_k.shape[1]
    new = pl.BlockSpec((1, t, W_B), lambda bb: (bb, 0, 0))
    old = pl.BlockSpec((1, keep, W_B), lambda bb: (bb, 0, 0))
    return pl.pallas_call(
        functools.partial(_battn_sample_kernel, t=t),
        grid=(b,),
        in_specs=[new, old, new, old, new, _const_spec(bias_c.shape), _const_spec(bias_n.shape)],
        out_specs=new,
        out_shape=jax.ShapeDtypeStruct((b, t, W_B), F32),
        scratch_shapes=[pltpu.VMEM((LANES, W_B), BF16), pltpu.VMEM((LANES, W_B), BF16)],
        compiler_params=pltpu.CompilerParams(dimension_semantics=("parallel",),
                                             vmem_limit_bytes=VMEM_LIMIT),
        name="band_attention_sample",
    )(qb, cache_k, kbb, cache_v, vbb, bias_c, bias_n)


def _band_bias(rel_bias, q0, nq, k0, nk, n_valid):
    qpos = q0 + np.arange(nq)[:, None]
    kidx = np.arange(nk)[None, :]
    kpos = k0 + kidx
    dchunk = qpos // CHUNK - kpos // CHUNK
    visible = (kpos >= 0) & (dchunk >= 0) & (dchunk <= BAND_CHUNKS) & (kidx < n_valid)
    dist = np.arange(q0 - k0 - (nk - 1), q0 - k0 + nq)
    vals = rel_bias.astype(F32)[:, np.clip(dist, -REL_CLIP, REL_CLIP) + REL_CLIP]
    w_pad = jnp.pad(vals[:, ::-1], ((0, 0), (0, 1)))
    width = nq + nk - 1
    skew = jnp.tile(w_pad, (1, nq))[:, :nq * width].reshape(-1, nq, width)
    return jnp.where(jnp.asarray(visible)[None], skew[:, :, nq - 1:nq - 1 + nk], NEG_INF)


def _merge_ffn_kernel(x_ref, oa_ref, ob_ref, g1_ref, gs_ref, wa_ref, wb_ref, wg_ref, wo_ref,
                      g2_ref, w1_ref, w2_ref, y_ref):
    x = x_ref[...]
    hb = _rms_rows(x, g1_ref[...]).astype(BF16)
    oa = oa_ref[...]
    heads = [_rms_rows(oa[:, c * DVA:(c + 1) * DVA], 1.0) for c in range(H_A)]
    oan = (jnp.concatenate(heads, axis=1) * gs_ref[...]).astype(BF16)
    ya = _dot(oan, wa_ref[...])
    yb = _dot(ob_ref[...].astype(BF16), wb_ref[...])
    gates = jax.nn.sigmoid(_dot(hb, wg_ref[...]))
    mix = gates[:, 0:D_MODEL] * ya + gates[:, D_MODEL:2 * D_MODEL] * yb
    x1 = x + _dot(mix.astype(BF16), wo_ref[...])
    h2 = _rms_rows(x1, g2_ref[...]).astype(BF16)
    u = jnp.maximum(_dot(h2, w1_ref[...]), 0.0)
    y_ref[...] = x1 + _dot((u * u).astype(BF16), w2_ref[...])


def _merge_ffn(x2d, oa2d, ob2d, g1, gs, wa, wb, wg, wo, g2, w1, w2):
    n = x2d.shape[0]
    tm = min(PROJ_ROWS, n)
    assert n % tm == 0
    row = lambda width: pl.BlockSpec((tm, width), lambda i: (i, 0))
    consts = (g1, gs, wa, wb, wg, wo, g2, w1, w2)
    return pl.pallas_call(
        _merge_ffn_kernel,
        grid=(n // tm,),
        in_specs=[row(D_MODEL), row(W_A), row(W_B)] + [_const_spec(c.shape) for c in consts],
        out_specs=row(D_MODEL),
        out_shape=jax.ShapeDtypeStruct((n, D_MODEL), F32),
        compiler_params=pltpu.CompilerParams(dimension_semantics=("parallel",),
                                             vmem_limit_bytes=VMEM_LIMIT),
        name="merge_ffn",
    )(x2d, oa2d, ob2d, *consts)


def kernel(x_prompt, x_sample, cache_a_k, cache_a_v, cache_b_k, cache_b_v, norm1_g, w_in, qn_a_g, kn_a_g, qn_b_g, kn_b_g, lam_q1, lam_k1, lam_q2, lam_k2, subln_a_g, rel_bias_b, w_br_a, w_br_b, w_gate, w_out, norm2_g, w_ff1, w_ff2):
    bp, s, d = x_prompt.shape
    bs, t, _ = x_sample.shape
    depth = w_in.shape[0]
    p_len = cache_a_k.shape[2]
    b_keep = cache_b_k.shape[2]
    keep_p = min(BAND_PAST, s)
    cdt = cache_a_k.dtype

    slopes = jnp.exp2(-8.0 * jnp.arange(1, H_A + 1, dtype=F32) / H_A)
    group = np.arange(NORM_CHUNK) // DA
    ones = jnp.asarray(group[:, None] == group[None, :], BF16)
    row = lambda v, reps, scale=1.0: (jnp.tile(v.astype(F32), reps) * scale)[None, :]

    xp = x_prompt.astype(F32).reshape(bp * s, d)
    xs = x_sample.astype(F32).reshape(bs * t, d)
    outs = [[] for _ in range(8)]
    for l in range(depth):
        lam_init = 0.8 - 0.6 * math.exp(-0.3 * l)
        lamv = jnp.stack([lam_q1[l], lam_k1[l], lam_q2[l], lam_k2[l]]).astype(F32)
        g1 = row(norm1_g[l], 1)
        proj_consts = (g1, w_in[l].astype(BF16), ones,
                       row(qn_a_g[l], 2 * H_A, DA ** -0.5 * LOG2E), row(kn_a_g[l], 2 * H_A),
                       row(qn_b_g[l], H_B, DB ** -0.5), row(kn_b_g[l], H_B))
        tail = (g1, row(subln_a_g[l], H_A, 1.0 - lam_init), w_br_a[l].astype(BF16),
                w_br_b[l].astype(BF16), w_gate[l].astype(BF16), w_out[l].astype(BF16),
                row(norm2_g[l], 1), w_ff1[l].astype(BF16), w_ff2[l].astype(BF16))

        qa, kaf, kab, vaf, vab, qb, kbf, kbb, vbf, vbb = _projection(xp, *proj_consts)
        r3 = lambda a, n: a.reshape(n, -1, a.shape[-1])
        oa = _diff_attention_prompt(r3(qa, bp), r3(kab, bp), r3(vab, bp), slopes, lamv, lam_init)
        bias = _band_bias(rel_bias_b[l], BAND_PAST, BAND_BLOCK, 0, 2 * BAND_BLOCK, 2 * BAND_BLOCK)
        ob = _band_attention_prompt(r3(qb, bp), r3(kbb, bp), r3(vbb, bp), bias)
        xp = _merge_ffn(xp, oa.reshape(bp * s, W_A), ob.reshape(bp * s, W_B), *tail)
        outs[0].append(kaf.reshape(bp, s, H_A, 2 * DA))
        outs[1].append(vaf.reshape(bp, s, H_A, DVA))
        outs[2].append(kbf.reshape(bp, s, H_B, DB)[:, s - keep_p:])
        outs[3].append(vbf.reshape(bp, s, H_B, DB)[:, s - keep_p:])

        qa, kaf, kab, vaf, vab, qb, kbf, kbb, vbf, vbb = _projection(xs, *proj_consts)
        oa = _diff_attention_sample(
            r3(qa, bs), r3(kab, bs), r3(vab, bs),
            cache_a_k[l].reshape(bs, p_len, W_A), cache_a_v[l].reshape(bs, p_len, W_A),
            slopes, lamv, lam_init)
        bias_c = _band_bias(rel_bias_b[l], p_len, t, p_len - b_keep, b_keep, b_keep)
        bias_n = _band_bias(rel_bias_b[l], p_len, t, p_len, LANES, t)
        ob = _band_attention_sample(
            r3(qb, bs), r3(kbb, bs), r3(vbb, bs),
            cache_b_k[l].reshape(bs, b_keep, W_B), cache_b_v[l].reshape(bs, b_keep, W_B),
            bias_c, bias_n)
        xs = _merge_ffn(xs, oa.reshape(bs * t, W_A), ob.reshape(bs * t, W_B), *tail)
        outs[4].append(kaf.reshape(bs, t, H_A, 2 * DA))
        outs[5].append(vaf.reshape(bs, t, H_A, DVA))
        outs[6].append(kbf.reshape(bs, t, H_B, DB))
        outs[7].append(vbf.reshape(bs, t, H_B, DB))

    stacked = [jnp.stack(o).astype(cdt) for o in outs]
    return (xp.reshape(bp, s, d).astype(x_prompt.dtype), xs.reshape(bs, t, d).astype(x_sample.dtype),
            *stacked)
```

```python
import functools
import math

import numpy as np
import jax
import jax.numpy as jnp
from jax import lax
from jax.experimental import pallas as pl
from jax.experimental.pallas import tpu as pltpu

F32 = jnp.float32
BF16 = jnp.bfloat16

D_MODEL = 1024
CHUNK = 64
H_A = 8
DA = 64
DVA = 2 * DA
H_B = 8
DB = 64
BAND_CHUNKS = 8
BAND_PAST = BAND_CHUNKS * CHUNK
REL_CLIP = 128
D_FF = 4 * D_MODEL
EPS = 1e-6
NEG_INF = -1e30
W_A = H_A * 2 * DA
W_B = H_B * DB

LANES = 128
NORM_CHUNK = 256
VMEM_LIMIT = 56 * 1024 * 1024

PROJ_ROWS = 256
ATT_BLOCK = 512
LOG2E = 1.4426950408889634
N_BIAS_COLS = 3
SUM_ROWS = 16
BAND_BLOCK = 512
BAND_SUB = 128


def _rms_rows(x, gain_row):
    return x * lax.rsqrt(jnp.mean(x * x, axis=-1, keepdims=True) + EPS) * gain_row


def _dot(a, b):
    return jnp.dot(a, b, preferred_element_type=F32)


def _dot_nt(a, b):
    return lax.dot_general(a, b, (((1,), (1,)), ((), ())), preferred_element_type=F32)


def _head_norm(z, group_ones, gain_row):
    outs = []
    for c in range(z.shape[1] // NORM_CHUNK):
        zc = z[:, c * NORM_CHUNK:(c + 1) * NORM_CHUNK]
        ss = _dot((zc * zc).astype(BF16), group_ones)
        outs.append(zc * lax.rsqrt(ss * (1.0 / DA) + EPS))
    return jnp.concatenate(outs, axis=1) * gain_row


def _proj_kernel(x_ref, g1_ref, w_ref, ones_ref, gqa_ref, gka_ref, gqb_ref, gkb_ref,
                 qa_ref, kaf_ref, kab_ref, vaf_ref, vab_ref,
                 qb_ref, kbf_ref, kbb_ref, vbf_ref, vbb_ref):
    hb = _rms_rows(x_ref[...], g1_ref[...]).astype(BF16)
    ones = ones_ref[...]

    qa = _head_norm(_dot(hb, w_ref[:, 0:W_A]), ones, gqa_ref[...])
    qa_ref[...] = qa.astype(BF16)
    ka = _head_norm(_dot(hb, w_ref[:, W_A:2 * W_A]), ones, gka_ref[...])
    kaf_ref[...] = ka
    kab_ref[...] = ka.astype(BF16)
    va = _dot(hb, w_ref[:, 2 * W_A:3 * W_A])
    vaf_ref[...] = va
    vab_ref[...] = va.astype(BF16)

    o = 3 * W_A
    qb = _head_norm(_dot(hb, w_ref[:, o:o + W_B]), ones, gqb_ref[...])
    qb_ref[...] = qb.astype(BF16)
    kb = _head_norm(_dot(hb, w_ref[:, o + W_B:o + 2 * W_B]), ones, gkb_ref[...])
    kbf_ref[...] = kb
    kbb_ref[...] = kb.astype(BF16)
    vb = _dot(hb, w_ref[:, o + 2 * W_B:o + 3 * W_B])
    vbf_ref[...] = vb
    vbb_ref[...] = vb.astype(BF16)


def _const_spec(shape):
    return pl.BlockSpec(shape, lambda *_: (0,) * len(shape), pipeline_mode=pl.Buffered(1))


def _projection(x2d, g1, w_bf, ones, gqa, gka, gqb, gkb):
    n = x2d.shape[0]
    tm = min(PROJ_ROWS, n)
    assert n % tm == 0
    row = lambda width: pl.BlockSpec((tm, width), lambda i: (i, 0))
    out_shapes = []
    out_specs = []
    for width, dtypes in ((W_A, (BF16,)), (W_A, (F32, BF16)), (W_A, (F32, BF16)),
                          (W_B, (BF16,)), (W_B, (F32, BF16)), (W_B, (F32, BF16))):
        for dt in dtypes:
            out_shapes.append(jax.ShapeDtypeStruct((n, width), dt))
            out_specs.append(row(width))
    return pl.pallas_call(
        _proj_kernel,
        grid=(n // tm,),
        in_specs=[row(D_MODEL), _const_spec((1, D_MODEL)), _const_spec(w_bf.shape),
                  _const_spec(ones.shape), _const_spec((1, W_A)), _const_spec((1, W_A)),
                  _const_spec((1, W_B)), _const_spec((1, W_B))],
        out_specs=out_specs,
        out_shape=out_shapes,
        compiler_params=pltpu.CompilerParams(dimension_semantics=("parallel",),
                                             vmem_limit_bytes=VMEM_LIMIT),
        name="projection",
    )(x2d, g1, w_bf, ones, gqa, gka, gqb, gkb)


def _split_maps(q):
    lane = lax.broadcasted_iota(jnp.int32, q.shape, 1)
    zero = jnp.zeros_like(q)
    return jnp.concatenate([jnp.where(lane < DA, q, zero), jnp.where(lane >= DA, q, zero)], axis=0)


def _lambda(lamv, lam_init):
    a = jnp.sum(lamv[0:1, :] * lamv[1:2, :], axis=1, keepdims=True)
    b = jnp.sum(lamv[2:3, :] * lamv[3:4, :], axis=1, keepdims=True)
    return jnp.exp(a) - jnp.exp(b) + lam_init


def _dattn_kernel(slopes_ref, lamv_ref, q_ref, k_ref, v_ref, o_ref,
                  qz_sc, kb_sc, db_sc, sa_sc, sb_sc, m_sc, acc_sc, *, blk, lam_init):
    h = pl.program_id(1)
    i = pl.program_id(2)
    sl2 = slopes_ref[h] * LOG2E

    @pl.when(i == 0)
    def _head_tables():
        qz_sc[LANES:2 * LANES, :] = (
            lax.broadcasted_iota(jnp.int32, (LANES, 2 * blk), 0) < N_BIAS_COLS).astype(BF16)
        r = lax.broadcasted_iota(jnp.int32, (blk, LANES), 0)
        c = lax.broadcasted_iota(jnp.int32, (blk, LANES), 1)
        rest = sl2 * r.astype(F32)
        kb = jnp.zeros((blk, LANES), F32)
        for col in range(N_BIAS_COLS):
            part = rest.astype(BF16).astype(F32)
            kb = jnp.where(c == col, part, kb)
            rest = rest - part
        kb_sc[...] = kb.astype(BF16)
        ki = lax.broadcasted_iota(jnp.int32, (blk, blk), 0)
        qi = lax.broadcasted_iota(jnp.int32, (blk, blk), 1)
        bias = jnp.where((ki // CHUNK) <= (qi // CHUNK),
                         sl2 * (qi - jnp.abs(qi - ki) - ki).astype(F32), NEG_INF)
        db_sc[:, 0:blk] = bias
        db_sc[:, blk:2 * blk] = bias

    qt = q_ref[0].T
    row = lax.broadcasted_iota(jnp.int32, qt.shape, 0)
    zero = jnp.zeros_like(qt)
    qz_sc[0:LANES, 0:blk] = jnp.where(row < DA, qt, zero)
    qz_sc[0:LANES, blk:2 * blk] = jnp.where(row >= DA, qt, zero)

    m_sc[...] = jnp.full(m_sc.shape, -jnp.inf, F32)
    acc_sc[...] = jnp.zeros(acc_sc.shape, F32)
    ones_rows = jnp.ones((SUM_ROWS, blk), BF16)

    def rows(j):
        return pl.ds(pl.multiple_of(j * blk, blk), blk)

    def scores(j, dst):
        dst[...] = _dot(jnp.concatenate([k_ref[0, rows(j), :], kb_sc[...]], axis=1), qz_sc[...])

    def softmax_pv(j, s):
        off = sl2 * ((j - i) * blk).astype(F32)
        m_prev = m_sc[...] - off
        m_new = jnp.maximum(m_prev, jnp.max(s, axis=0, keepdims=True))
        alpha = jnp.exp2(m_prev - m_new)
        pb = jnp.exp2(s - m_new).astype(BF16)
        vt = jnp.concatenate([v_ref[0, rows(j), :].T, ones_rows], axis=0)
        pv = jnp.concatenate([_dot(vt, pb[:, 0:blk]), _dot(vt, pb[:, blk:2 * blk])], axis=1)
        acc_sc[...] = alpha * acc_sc[...] + pv
        m_sc[...] = m_new + off

    scores(0, sa_sc)

    def pair(jj, carry):
        j0 = 2 * jj
        scores(j0 + 1, sb_sc)
        softmax_pv(j0, sa_sc[...])
        scores(j0 + 2, sa_sc)
        softmax_pv(j0 + 1, sb_sc[...])
        return carry

    lax.fori_loop(0, i // 2, pair, 0)

    @pl.when(i % 2 == 0)
    def _even():
        softmax_pv(i, sa_sc[...] + db_sc[...])

    @pl.when(i % 2 == 1)
    def _odd():
        scores(i, sb_sc)
        softmax_pv(i - 1, sa_sc[...])
        softmax_pv(i, sb_sc[...] + db_sc[...])

    out = acc_sc[0:DVA, :] / acc_sc[DVA:DVA + 1, :]
    lam = _lambda(lamv_ref[...], lam_init)
    o_ref[0] = (out[:, 0:blk] - lam * out[:, blk:2 * blk]).T


def _diff_attention_prompt(qa, kab, vab, slopes, lamv, lam_init):
    b, s, _ = qa.shape
    blk = min(ATT_BLOCK, s)
    assert s % blk == 0 and blk % CHUNK == 0
    head_rows = pl.BlockSpec((1, s, DVA), lambda bb, h, i: (bb, 0, h))
    block = pl.BlockSpec((1, blk, DVA), lambda bb, h, i: (bb, i, h))
    return pl.pallas_call(
        functools.partial(_dattn_kernel, blk=blk, lam_init=lam_init),
        grid=(b, H_A, s // blk),
        in_specs=[pl.BlockSpec(memory_space=pltpu.SMEM), pl.BlockSpec((4, DA), lambda bb, h, i: (0, 0)),
                  block, head_rows, head_rows],
        out_specs=block,
        out_shape=jax.ShapeDtypeStruct((b, s, W_A), F32),
        scratch_shapes=[pltpu.VMEM((2 * LANES, 2 * blk), BF16), pltpu.VMEM((blk, LANES), BF16),
                        pltpu.VMEM((blk, 2 * blk), F32),
                        pltpu.VMEM((blk, 2 * blk), F32), pltpu.VMEM((blk, 2 * blk), F32),
                        pltpu.VMEM((1, 2 * blk), F32), pltpu.VMEM((DVA + SUM_ROWS, 2 * blk), F32)],
        compiler_params=pltpu.CompilerParams(
            dimension_semantics=("parallel", "parallel", "arbitrary"), vmem_limit_bytes=VMEM_LIMIT),
        name="diff_attention",
    )(slopes, lamv, qa, kab, vab)


def _dattn_sample_kernel(slopes_ref, lamv_ref, q_ref, kc_ref, kn_ref, vc_ref, vn_ref, o_ref,
                         kpad_sc, vpad_sc, *, t, p_len, lam_init):
    h = pl.program_id(1)
    slope = slopes_ref[h] * LOG2E
    qz = _split_maps(q_ref[0])
    kpad_sc[...] = jnp.zeros(kpad_sc.shape, BF16)
    vpad_sc[...] = jnp.zeros(vpad_sc.shape, BF16)
    kpad_sc[0:t, :] = kn_ref[0]
    vpad_sc[0:t, :] = vn_ref[0]

    def bias(k0, nk, n_valid):
        qpos = lax.broadcasted_iota(jnp.int32, (t, nk), 0) + p_len
        kidx = lax.broadcasted_iota(jnp.int32, (t, nk), 1)
        kpos = kidx + k0
        visible = ((kpos // CHUNK) <= (qpos // CHUNK)) & (kidx < n_valid)
        return jnp.where(visible, -slope * jnp.abs(qpos - kpos).astype(F32), NEG_INF)

    def scores(k, bias_t):
        s = _dot_nt(qz, k)
        return (s.reshape(2, t, -1) + bias_t[None]).reshape(2 * t, -1)

    s_c = scores(kc_ref[0, 0].astype(BF16), bias(0, p_len, p_len))
    s_n = scores(kpad_sc[...], bias(p_len, LANES, t))
    m = jnp.maximum(jnp.max(s_c, axis=1, keepdims=True), jnp.max(s_n, axis=1, keepdims=True))
    p_c = jnp.exp2(s_c - m)
    p_n = jnp.exp2(s_n - m)
    l = jnp.sum(p_c, axis=1, keepdims=True) + jnp.sum(p_n, axis=1, keepdims=True)
    out = (_dot(p_c.astype(BF16), vc_ref[0, 0].astype(BF16)) + _dot(p_n.astype(BF16), vpad_sc[...])) / l
    lam = _lambda(lamv_ref[...], lam_init)
    o_ref[0] = out[0:t] - lam * out[t:2 * t]


def _diff_attention_sample(qa, kab, vab, cache_k, cache_v, layer, slopes, lamv, lam_init):
    b, t, _ = qa.shape
    p_len = cache_k.shape[2]
    assert t <= LANES
    new = pl.BlockSpec((1, t, DVA), lambda bb, h: (bb, 0, h))
    old = pl.BlockSpec((1, 1, p_len, DVA), lambda bb, h: (layer, bb, 0, h))
    return pl.pallas_call(
        functools.partial(_dattn_sample_kernel, t=t, p_len=p_len, lam_init=lam_init),
        grid=(b, H_A),
        in_specs=[pl.BlockSpec(memory_space=pltpu.SMEM), pl.BlockSpec((4, DA), lambda bb, h: (0, 0)),
                  new, old, new, old, new],
        out_specs=new,
        out_shape=jax.ShapeDtypeStruct((b, t, W_A), F32),
        scratch_shapes=[pltpu.VMEM((LANES, DVA), BF16), pltpu.VMEM((LANES, DVA), BF16)],
        compiler_params=pltpu.CompilerParams(dimension_semantics=("parallel", "parallel"),
                                             vmem_limit_bytes=VMEM_LIMIT),
        name="diff_attention_sample",
    )(slopes, lamv, qa, cache_k, kab, cache_v, vab)


def _pair_masks(q):
    lane = lax.broadcasted_iota(jnp.int32, q.shape, 1)
    zero = jnp.zeros_like(q)
    return lane, (jnp.where(lane < DB, q, zero), jnp.where(lane >= DB, q, zero))


def _battn_kernel(q_ref, kp_ref, kc_ref, vp_ref, vc_ref, strip_ref, o_ref, *, blk):
    i = pl.program_id(2)
    lane, qz = _pair_masks(q_ref[0])
    nsub = blk // BAND_SUB
    outs = []
    for e in range(2):
        def bias(first_col):
            return jnp.concatenate(
                [strip_ref[e, :, pl.ds((nsub - 1 - a) * BAND_SUB + first_col, blk)] for a in range(nsub)],
                axis=0)

        s_p = jnp.where(i > 0, _dot_nt(qz[e], kp_ref[0]) + bias(0), NEG_INF)
        s_c = _dot_nt(qz[e], kc_ref[0]) + bias(blk)
        m = jnp.maximum(jnp.max(s_p, axis=1, keepdims=True), jnp.max(s_c, axis=1, keepdims=True))
        p_p = jnp.exp(s_p - m)
        p_c = jnp.exp(s_c - m)
        l = jnp.sum(p_p, axis=1, keepdims=True) + jnp.sum(p_c, axis=1, keepdims=True)
        outs.append((_dot(p_p.astype(BF16), vp_ref[0]) + _dot(p_c.astype(BF16), vc_ref[0])) / l)
    o_ref[0] = jnp.where(lane < DB, outs[0], outs[1])


def _band_attention_prompt(qb, kbb, vbb, strip):
    b, s, _ = qb.shape
    blk = BAND_BLOCK
    assert s % blk == 0 and blk == BAND_PAST
    cur = pl.BlockSpec((1, blk, LANES), lambda pr, bb, i: (bb, i, pr))
    prev = pl.BlockSpec((1, blk, LANES), lambda pr, bb, i: (bb, jnp.maximum(i - 1, 0), pr))
    return pl.pallas_call(
        functools.partial(_battn_kernel, blk=blk),
        grid=(H_B // 2, b, s // blk),
        in_specs=[cur, prev, cur, prev, cur,
                  pl.BlockSpec((2,) + strip.shape[1:], lambda pr, bb, i: (pr, 0, 0))],
        out_specs=cur,
        out_shape=jax.ShapeDtypeStruct((b, s, W_B), F32),
        compiler_params=pltpu.CompilerParams(
            dimension_semantics=("parallel", "parallel", "parallel"), vmem_limit_bytes=VMEM_LIMIT),
        name="band_attention",
    )(qb, kbb, kbb, vbb, vbb, strip)


def _battn_sample_kernel(q_ref, kc_ref, kn_ref, vc_ref, vn_ref, biasc_ref, biasn_ref, o_ref,
                         kpad_sc, vpad_sc, *, t):
    kpad_sc[...] = jnp.zeros(kpad_sc.shape, BF16)
    vpad_sc[...] = jnp.zeros(vpad_sc.shape, BF16)
    kpad_sc[0:t, :] = kn_ref[0]
    vpad_sc[0:t, :] = vn_ref[0]
    pieces = []
    for pr in range(H_B // 2):
        cols = slice(pr * LANES, (pr + 1) * LANES)
        lane, qz = _pair_masks(q_ref[0, :, cols])
        kc = kc_ref[0, 0, :, cols].astype(BF16)
        vc = vc_ref[0, 0, :, cols].astype(BF16)
        kn = kpad_sc[:, cols]
        vn = vpad_sc[:, cols]
        outs = []
        for e in range(2):
            hd = 2 * pr + e
            s_c = _dot_nt(qz[e], kc) + biasc_ref[hd]
            s_n = _dot_nt(qz[e], kn) + biasn_ref[hd]
            m = jnp.maximum(jnp.max(s_c, axis=1, keepdims=True), jnp.max(s_n, axis=1, keepdims=True))
            p_c = jnp.exp(s_c - m)
            p_n = jnp.exp(s_n - m)
            l = jnp.sum(p_c, axis=1, keepdims=True) + jnp.sum(p_n, axis=1, keepdims=True)
            outs.append((_dot(p_c.astype(BF16), vc) + _dot(p_n.astype(BF16), vn)) / l)
        pieces.append(jnp.where(lane < DB, outs[0], outs[1]))
    o_ref[0] = jnp.concatenate(pieces, axis=1)


def _band_attention_sample(qb, kbb, vbb, cache_k, cache_v, layer, bias_c, bias_n):
    b, t, _ = qb.shape
    keep = cache_k.shape[2]
    new = pl.BlockSpec((1, t, W_B), lambda bb: (bb, 0, 0))
    old = pl.BlockSpec((1, 1, keep, W_B), lambda bb: (layer, bb, 0, 0))
    return pl.pallas_call(
        functools.partial(_battn_sample_kernel, t=t),
        grid=(b,),
        in_specs=[new, old, new, old, new, _const_spec(bias_c.shape), _const_spec(bias_n.shape)],
        out_specs=new,
        out_shape=jax.ShapeDtypeStruct((b, t, W_B), F32),
        scratch_shapes=[pltpu.VMEM((LANES, W_B), BF16), pltpu.VMEM((LANES, W_B), BF16)],
        compiler_params=pltpu.CompilerParams(dimension_semantics=("parallel",),
                                             vmem_limit_bytes=VMEM_LIMIT),
        name="band_attention_sample",
    )(qb, cache_k, kbb, cache_v, vbb, bias_c, bias_n)


def _band_bias(rel_bias, q0, nq, k0, nk, n_valid):
    qpos = q0 + np.arange(nq)[:, None]
    kidx = np.arange(nk)[None, :]
    kpos = k0 + kidx
    dchunk = qpos // CHUNK - kpos // CHUNK
    visible = (kpos >= 0) & (dchunk >= 0) & (dchunk <= BAND_CHUNKS) & (kidx < n_valid)
    dist = np.arange(q0 - k0 - (nk - 1), q0 - k0 + nq)
    vals = rel_bias.astype(F32)[:, np.clip(dist, -REL_CLIP, REL_CLIP) + REL_CLIP]
    w_pad = jnp.pad(vals[:, ::-1], ((0, 0), (0, 1)))
    width = nq + nk - 1
    skew = jnp.tile(w_pad, (1, nq))[:, :nq * width].reshape(-1, nq, width)
    return jnp.where(jnp.asarray(visible)[None], skew[:, :, nq - 1:nq - 1 + nk], NEG_INF)


def _merge_ffn_kernel(x_ref, oa_ref, ob_ref, g1_ref, gs_ref, wa_ref, wb_ref, wg_ref, wo_ref,
                      g2_ref, w1_ref, w2_ref, y_ref):
    x = x_ref[...]
    hb = _rms_rows(x, g1_ref[...]).astype(BF16)
    oa = oa_ref[...]
    heads = [_rms_rows(oa[:, c * DVA:(c + 1) * DVA], 1.0) for c in range(H_A)]
    oan = (jnp.concatenate(heads, axis=1) * gs_ref[...]).astype(BF16)
    ya = _dot(oan, wa_ref[...])
    yb = _dot(ob_ref[...].astype(BF16), wb_ref[...])
    gates = jax.nn.sigmoid(_dot(hb, wg_ref[...]))
    mix = gates[:, 0:D_MODEL] * ya + gates[:, D_MODEL:2 * D_MODEL] * yb
    x1 = x + _dot(mix.astype(BF16), wo_ref[...])
    h2 = _rms_rows(x1, g2_ref[...]).astype(BF16)
    u = jnp.maximum(_dot(h2, w1_ref[...]), 0.0)
    y_ref[...] = x1 + _dot((u * u).astype(BF16), w2_ref[...])


def _merge_ffn(x2d, oa2d, ob2d, g1, gs, wa, wb, wg, wo, g2, w1, w2):
    n = x2d.shape[0]
    tm = min(PROJ_ROWS, n)
    assert n % tm == 0
    row = lambda width: pl.BlockSpec((tm, width), lambda i: (i, 0))
    consts = (g1, gs, wa, wb, wg, wo, g2, w1, w2)
    return pl.pallas_call(
        _merge_ffn_kernel,
        grid=(n // tm,),
        in_specs=[row(D_MODEL), row(W_A), row(W_B)] + [_const_spec(c.shape) for c in consts],
        out_specs=row(D_MODEL),
        out_shape=jax.ShapeDtypeStruct((n, D_MODEL), F32),
        compiler_params=pltpu.CompilerParams(dimension_semantics=("parallel",),
                                             vmem_limit_bytes=VMEM_LIMIT),
        name="merge_ffn",
    )(x2d, oa2d, ob2d, *consts)


def kernel(x_prompt, x_sample, cache_a_k, cache_a_v, cache_b_k, cache_b_v, norm1_g, w_in, qn_a_g, kn_a_g, qn_b_g, kn_b_g, lam_q1, lam_k1, lam_q2, lam_k2, subln_a_g, rel_bias_b, w_br_a, w_br_b, w_gate, w_out, norm2_g, w_ff1, w_ff2):
    bp, s, d = x_prompt.shape
    bs, t, _ = x_sample.shape
    depth = w_in.shape[0]
    p_len = cache_a_k.shape[2]
    b_keep = cache_b_k.shape[2]
    keep_p = min(BAND_PAST, s)
    cdt = cache_a_k.dtype

    slopes = jnp.exp2(-8.0 * jnp.arange(1, H_A + 1, dtype=F32) / H_A)
    group = np.arange(NORM_CHUNK) // DA
    ones = jnp.asarray(group[:, None] == group[None, :], BF16)
    row = lambda v, reps, scale=1.0: (jnp.tile(v.astype(F32), reps) * scale)[None, :]
    cache_ak = cache_a_k.reshape(depth, bs, p_len, W_A)
    cache_av = cache_a_v.reshape(depth, bs, p_len, W_A)
    cache_bk = cache_b_k.reshape(depth, bs, b_keep, W_B)
    cache_bv = cache_b_v.reshape(depth, bs, b_keep, W_B)

    xp = x_prompt.astype(F32).reshape(bp * s, d)
    xs = x_sample.astype(F32).reshape(bs * t, d)
    outs = [[] for _ in range(8)]
    for l in range(depth):
        lam_init = 0.8 - 0.6 * math.exp(-0.3 * l)
        lamv = jnp.stack([lam_q1[l], lam_k1[l], lam_q2[l], lam_k2[l]]).astype(F32)
        g1 = row(norm1_g[l], 1)
        proj_consts = (g1, w_in[l].astype(BF16), ones,
                       row(qn_a_g[l], 2 * H_A, DA ** -0.5 * LOG2E), row(kn_a_g[l], 2 * H_A),
                       row(qn_b_g[l], H_B, DB ** -0.5), row(kn_b_g[l], H_B))
        tail = (g1, row(subln_a_g[l], H_A, 1.0 - lam_init), w_br_a[l].astype(BF16),
                w_br_b[l].astype(BF16), w_gate[l].astype(BF16), w_out[l].astype(BF16),
                row(norm2_g[l], 1), w_ff1[l].astype(BF16), w_ff2[l].astype(BF16))

        qa, kaf, kab, vaf, vab, qb, kbf, kbb, vbf, vbb = _projection(xp, *proj_consts)
        r3 = lambda a, n: a.reshape(n, -1, a.shape[-1])
        oa = _diff_attention_prompt(r3(qa, bp), r3(kab, bp), r3(vab, bp), slopes, lamv, lam_init)
        ext = BAND_BLOCK - BAND_SUB
        strip = _band_bias(rel_bias_b[l], BAND_PAST + ext, BAND_SUB, 0, 2 * BAND_BLOCK + ext,
                           2 * BAND_BLOCK + ext)
        ob = _band_attention_prompt(r3(qb, bp), r3(kbb, bp), r3(vbb, bp), strip)
        xp = _merge_ffn(xp, oa.reshape(bp * s, W_A), ob.reshape(bp * s, W_B), *tail)
        outs[0].append(kaf.reshape(bp, s, H_A, 2 * DA))
        outs[1].append(vaf.reshape(bp, s, H_A, DVA))
        outs[2].append(kbf.reshape(bp, s, H_B, DB)[:, s - keep_p:])
        outs[3].append(vbf.reshape(bp, s, H_B, DB)[:, s - keep_p:])

        qa, kaf, kab, vaf, vab, qb, kbf, kbb, vbf, vbb = _projection(xs, *proj_consts)
        oa = _diff_attention_sample(r3(qa, bs), r3(kab, bs), r3(vab, bs), cache_ak, cache_av, l,
                                    slopes, lamv, lam_init)
        bias_c = _band_bias(rel_bias_b[l], p_len, t, p_len - b_keep, b_keep, b_keep)
        bias_n = _band_bias(rel_bias_b[l], p_len, t, p_len, LANES, t)
        ob = _band_attention_sample(r3(qb, bs), r3(kbb, bs), r3(vbb, bs), cache_bk, cache_bv, l,
                                    bias_c, bias_n)
        xs = _merge_ffn(xs, oa.reshape(bs * t, W_A), ob.reshape(bs * t, W_B), *tail)
        outs[4].append(kaf.reshape(bs, t, H_A, 2 * DA))
        outs[5].append(vaf.reshape(bs, t, H_A, DVA))
        outs[6].append(kbf.reshape(bs, t, H_B, DB))
        outs[7].append(vbf.reshape(bs, t, H_B, DB))

    stacked = [jnp.stack(o).astype(cdt) for o in outs]
    return (xp.reshape(bp, s, d).astype(x_prompt.dtype), xs.reshape(bs, t, d).astype(x_sample.dtype),
            *stacked)
```

```python
import functools
import math

import numpy as np
import jax
import jax.numpy as jnp
from jax import lax
from jax.experimental import pallas as pl
from jax.experimental.pallas import tpu as pltpu

F32 = jnp.float32
BF16 = jnp.bfloat16

D_MODEL = 1024
CHUNK = 64
H_A = 8
DA = 64
DVA = 2 * DA
H_B = 8
DB = 64
BAND_CHUNKS = 8
BAND_PAST = BAND_CHUNKS * CHUNK
REL_CLIP = 128
D_FF = 4 * D_MODEL
EPS = 1e-6
NEG_INF = -1e30
W_A = H_A * 2 * DA
W_B = H_B * DB

LANES = 128
NORM_CHUNK = 256
VMEM_LIMIT = 56 * 1024 * 1024

PROJ_ROWS = 256
ATT_Q_BLOCK = 1024
LOG2E = 1.4426950408889634
N_BIAS_COLS = 3
SUM_ROWS = 16
BAND_BLOCK = 512
BAND_SUB = 128


def _rms_rows(x, gain_row):
    return x * lax.rsqrt(jnp.mean(x * x, axis=-1, keepdims=True) + EPS) * gain_row


def _dot(a, b):
    return jnp.dot(a, b, preferred_element_type=F32)


def _dot_nt(a, b):
    return lax.dot_general(a, b, (((1,), (1,)), ((), ())), preferred_element_type=F32)


def _head_norm(z, group_ones, gain_row):
    outs = []
    for c in range(z.shape[1] // NORM_CHUNK):
        zc = z[:, c * NORM_CHUNK:(c + 1) * NORM_CHUNK]
        ss = _dot((zc * zc).astype(BF16), group_ones)
        outs.append(zc * lax.rsqrt(ss * (1.0 / DA) + EPS))
    return jnp.concatenate(outs, axis=1) * gain_row


def _proj_kernel(x_ref, g1_ref, w_ref, ones_ref, gqa_ref, gka_ref, gqb_ref, gkb_ref, *rest):
    (qa_ref, kaf_ref, kab_ref, vaf_ref, vab_ref,
     qb_ref, kbf_ref, kbb_ref, vbf_ref, vbb_ref) = rest[-10:]
    hb = _rms_rows(x_ref[...], g1_ref[...]).astype(BF16)
    ones = ones_ref[...]

    def store_heads(dst_ref, z):
        for hd in range(H_A):
            dst_ref[0, :, hd, :] = z[:, hd * DVA:(hd + 1) * DVA]

    qa = _head_norm(_dot(hb, w_ref[:, 0:W_A]), ones, gqa_ref[...])
    qa_ref[...] = qa.astype(BF16)
    ka = _head_norm(_dot(hb, w_ref[:, W_A:2 * W_A]), ones, gka_ref[...])
    store_heads(kaf_ref, ka)
    kab_ref[...] = ka.astype(BF16)
    va = _dot(hb, w_ref[:, 2 * W_A:3 * W_A])
    store_heads(vaf_ref, va)
    vab_ref[...] = va.astype(BF16)

    o = 3 * W_A
    qb = _head_norm(_dot(hb, w_ref[:, o:o + W_B]), ones, gqb_ref[...])
    qb_ref[...] = qb.astype(BF16)
    kb = _head_norm(_dot(hb, w_ref[:, o + W_B:o + 2 * W_B]), ones, gkb_ref[...])
    kbf_ref[...] = kb
    kbb_ref[...] = kb.astype(BF16)
    vb = _dot(hb, w_ref[:, o + 2 * W_B:o + 3 * W_B])
    vbf_ref[...] = vb
    vbb_ref[...] = vb.astype(BF16)


def _const_spec(shape):
    return pl.BlockSpec(shape, lambda *_: (0,) * len(shape), pipeline_mode=pl.Buffered(1))


def _projection(x2d, g1, w_bf, ones, gqa, gka, gqb, gkb, layer, depth, caches):
    n = x2d.shape[0]
    tm = min(PROJ_ROWS, n)
    assert n % tm == 0
    row = lambda width: pl.BlockSpec((tm, width), lambda i: (i, 0))
    slab = pl.BlockSpec((1, tm, H_A, DVA), lambda i: (layer, i, 0, 0))
    out_shapes = []
    out_specs = []
    for width, dtypes in ((W_A, (BF16,)), (W_A, (F32, BF16)), (W_A, (F32, BF16)),
                          (W_B, (BF16,)), (W_B, (F32, BF16)), (W_B, (F32, BF16))):
        for dt in dtypes:
            if width == W_A and dt == F32:
                out_shapes.append(jax.ShapeDtypeStruct((depth, n, H_A, DVA), F32))
                out_specs.append(slab)
            else:
                out_shapes.append(jax.ShapeDtypeStruct((n, width), dt))
                out_specs.append(row(width))
    n_in = 8
    return pl.pallas_call(
        _proj_kernel,
        grid=(n // tm,),
        in_specs=[row(D_MODEL), _const_spec((1, D_MODEL)), _const_spec(w_bf.shape),
                  _const_spec(ones.shape), _const_spec((1, W_A)), _const_spec((1, W_A)),
                  _const_spec((1, W_B)), _const_spec((1, W_B))]
                 + [pl.BlockSpec(memory_space=pl.ANY)] * len(caches),
        out_specs=out_specs,
        out_shape=out_shapes,
        input_output_aliases={n_in: 1, n_in + 1: 3} if caches else {},
        compiler_params=pltpu.CompilerParams(dimension_semantics=("parallel",),
                                             vmem_limit_bytes=VMEM_LIMIT),
        name="projection",
    )(x2d, g1, w_bf, ones, gqa, gka, gqb, gkb, *caches)


def _split_maps(q):
    lane = lax.broadcasted_iota(jnp.int32, q.shape, 1)
    zero = jnp.zeros_like(q)
    return jnp.concatenate([jnp.where(lane < DA, q, zero), jnp.where(lane >= DA, q, zero)], axis=0)


def _lambda(lamv, lam_init):
    a = jnp.sum(lamv[0:1, :] * lamv[1:2, :], axis=1, keepdims=True)
    b = jnp.sum(lamv[2:3, :] * lamv[3:4, :], axis=1, keepdims=True)
    return jnp.exp(a) - jnp.exp(b) + lam_init


def _dattn_kernel(slopes_ref, lamv_ref, q_ref, k_ref, v_ref, o_ref,
                  qz_sc, kb_sc, db_sc, sa_sc, sb_sc, m_sc, acc_sc, *, qblk, kblk, lam_init):
    h = pl.program_id(1)
    i = pl.program_id(2)
    own = qblk // kblk
    sl2 = slopes_ref[h] * LOG2E

    @pl.when(i == 0)
    def _head_tables():
        qz_sc[LANES:2 * LANES, :] = (
            lax.broadcasted_iota(jnp.int32, (LANES, 2 * qblk), 0) < N_BIAS_COLS).astype(BF16)
        r = lax.broadcasted_iota(jnp.int32, (kblk, LANES), 0)
        c = lax.broadcasted_iota(jnp.int32, (kblk, LANES), 1)
        rest = sl2 * r.astype(F32)
        kb = jnp.zeros((kblk, LANES), F32)
        for col in range(N_BIAS_COLS):
            part = rest.astype(BF16).astype(F32)
            kb = jnp.where(c == col, part, kb)
            rest = rest - part
        kb_sc[...] = kb.astype(BF16)
        ki = lax.broadcasted_iota(jnp.int32, (qblk, qblk), 0)
        qi = lax.broadcasted_iota(jnp.int32, (qblk, qblk), 1)
        db_sc[...] = jnp.where((ki // CHUNK) <= (qi // CHUNK),
                               sl2 * (qi - jnp.abs(qi - ki) - ki).astype(F32), NEG_INF)

    qt = q_ref[0].T
    row = lax.broadcasted_iota(jnp.int32, qt.shape, 0)
    zero = jnp.zeros_like(qt)
    qz_sc[0:LANES, 0:qblk] = jnp.where(row < DA, qt, zero)
    qz_sc[0:LANES, qblk:2 * qblk] = jnp.where(row >= DA, qt, zero)

    m_sc[...] = jnp.full(m_sc.shape, -jnp.inf, F32)
    acc_sc[...] = jnp.zeros(acc_sc.shape, F32)
    ones_rows = jnp.ones((SUM_ROWS, kblk), BF16)

    def rows(j):
        return pl.ds(pl.multiple_of(j * kblk, kblk), kblk)

    def scores(j, dst):
        dst[...] = _dot(jnp.concatenate([k_ref[0, rows(j), :], kb_sc[...]], axis=1), qz_sc[...])

    def softmax_pv(j, s):
        off = sl2 * (j * kblk - i * qblk).astype(F32)
        m_prev = m_sc[...] - off
        m_new = jnp.maximum(m_prev, jnp.max(s, axis=0, keepdims=True))
        alpha = jnp.exp2(m_prev - m_new)
        pb = jnp.exp2(s - m_new).astype(BF16)
        vt = jnp.concatenate([v_ref[0, rows(j), :].T, ones_rows], axis=0)
        pv = jnp.concatenate([_dot(vt, pb[:, 0:qblk]), _dot(vt, pb[:, qblk:2 * qblk])], axis=1)
        acc_sc[...] = alpha * acc_sc[...] + pv
        m_sc[...] = m_new + off

    def own_bias(d, s):
        db = db_sc[d * kblk:(d + 1) * kblk, :]
        return jnp.concatenate([s[:, 0:qblk] + db, s[:, qblk:2 * qblk] + db], axis=1)

    first_own = own * i
    scores(0, sa_sc)

    def pair(jj, carry):
        j0 = 2 * jj
        scores(j0 + 1, sb_sc)
        softmax_pv(j0, sa_sc[...])
        scores(j0 + 2, sa_sc)
        softmax_pv(j0 + 1, sb_sc[...])
        return carry

    lax.fori_loop(0, first_own // 2, pair, 0)

    for d in range(0, own, 2):
        scores(first_own + d + 1, sb_sc)
        softmax_pv(first_own + d, own_bias(d, sa_sc[...]))
        if d + 2 < own:
            scores(first_own + d + 2, sa_sc)
        softmax_pv(first_own + d + 1, own_bias(d + 1, sb_sc[...]))

    out = acc_sc[0:DVA, :] / acc_sc[DVA:DVA + 1, :]
    lam = _lambda(lamv_ref[...], lam_init)
    o_ref[0] = (out[:, 0:qblk] - lam * out[:, qblk:2 * qblk]).T


def _diff_attention_prompt(qa, kab, vab, slopes, lamv, lam_init):
    b, s, _ = qa.shape
    qblk = min(ATT_Q_BLOCK, s)
    kblk = qblk // 2
    assert s % qblk == 0 and kblk % CHUNK == 0
    head_rows = pl.BlockSpec((1, s, DVA), lambda bb, h, i: (bb, 0, h))
    block = pl.BlockSpec((1, qblk, DVA), lambda bb, h, i: (bb, i, h))
    return pl.pallas_call(
        functools.partial(_dattn_kernel, qblk=qblk, kblk=kblk, lam_init=lam_init),
        grid=(b, H_A, s // qblk),
        in_specs=[pl.BlockSpec(memory_space=pltpu.SMEM), pl.BlockSpec((4, DA), lambda bb, h, i: (0, 0)),
                  block, head_rows, head_rows],
        out_specs=block,
        out_shape=jax.ShapeDtypeStruct((b, s, W_A), F32),
        scratch_shapes=[pltpu.VMEM((2 * LANES, 2 * qblk), BF16), pltpu.VMEM((kblk, LANES), BF16),
                        pltpu.VMEM((qblk, qblk), F32),
                        pltpu.VMEM((kblk, 2 * qblk), F32), pltpu.VMEM((kblk, 2 * qblk), F32),
                        pltpu.VMEM((1, 2 * qblk), F32), pltpu.VMEM((DVA + SUM_ROWS, 2 * qblk), F32)],
        compiler_params=pltpu.CompilerParams(
            dimension_semantics=("parallel", "parallel", "arbitrary"), vmem_limit_bytes=VMEM_LIMIT),
        name="diff_attention",
    )(slopes, lamv, qa, kab, vab)


def _dattn_sample_kernel(slopes_ref, lamv_ref, q_ref, kc_ref, kn_ref, vc_ref, vn_ref, o_ref,
                         kpad_sc, vpad_sc, *, t, p_len, lam_init):
    kpad_sc[...] = jnp.zeros(kpad_sc.shape, BF16)
    vpad_sc[...] = jnp.zeros(vpad_sc.shape, BF16)
    kpad_sc[0:t, :] = kn_ref[0]
    vpad_sc[0:t, :] = vn_ref[0]
    lam = _lambda(lamv_ref[...], lam_init)

    def distance(k0, nk, n_valid):
        qpos = lax.broadcasted_iota(jnp.int32, (t, nk), 0) + p_len
        kidx = lax.broadcasted_iota(jnp.int32, (t, nk), 1)
        kpos = kidx + k0
        visible = ((kpos // CHUNK) <= (qpos // CHUNK)) & (kidx < n_valid)
        return visible, jnp.abs(qpos - kpos).astype(F32)

    vis_c, dist_c = distance(0, p_len, p_len)
    vis_n, dist_n = distance(p_len, LANES, t)
    outs = []
    for hd in range(H_A):
        cols = slice(hd * DVA, (hd + 1) * DVA)
        slope = slopes_ref[hd] * LOG2E
        qz = _split_maps(q_ref[0, :, cols])

        def scores(k, vis, dist):
            s = _dot_nt(qz, k)
            bias_t = jnp.where(vis, -slope * dist, NEG_INF)
            return (s.reshape(2, t, -1) + bias_t[None]).reshape(2 * t, -1)

        s_c = scores(kc_ref[0, 0, :, hd, :].astype(BF16), vis_c, dist_c)
        s_n = scores(kpad_sc[:, cols], vis_n, dist_n)
        m = jnp.maximum(jnp.max(s_c, axis=1, keepdims=True), jnp.max(s_n, axis=1, keepdims=True))
        p_c = jnp.exp2(s_c - m)
        p_n = jnp.exp2(s_n - m)
        l = jnp.sum(p_c, axis=1, keepdims=True) + jnp.sum(p_n, axis=1, keepdims=True)
        out = (_dot(p_c.astype(BF16), vc_ref[0, 0, :, hd, :].astype(BF16))
               + _dot(p_n.astype(BF16), vpad_sc[:, cols])) / l
        outs.append(out[0:t] - lam * out[t:2 * t])
    o_ref[0] = jnp.concatenate(outs, axis=1)


def _diff_attention_sample(qa, kab, vab, cache_k, cache_v, layer, slopes, lamv, lam_init):
    b, t, _ = qa.shape
    p_len = cache_k.shape[2]
    assert t <= LANES
    new = pl.BlockSpec((1, t, W_A), lambda bb: (bb, 0, 0))
    old = pl.BlockSpec((1, 1, p_len, H_A, DVA), lambda bb: (layer, bb, 0, 0, 0))
    return pl.pallas_call(
        functools.partial(_dattn_sample_kernel, t=t, p_len=p_len, lam_init=lam_init),
        grid=(b,),
        in_specs=[pl.BlockSpec(memory_space=pltpu.SMEM), pl.BlockSpec((4, DA), lambda bb: (0, 0)),
                  new, old, new, old, new],
        out_specs=new,
        out_shape=jax.ShapeDtypeStruct((b, t, W_A), F32),
        scratch_shapes=[pltpu.VMEM((LANES, W_A), BF16), pltpu.VMEM((LANES, W_A), BF16)],
        compiler_params=pltpu.CompilerParams(dimension_semantics=("parallel",),
                                             vmem_limit_bytes=VMEM_LIMIT),
        name="diff_attention_sample",
    )(slopes, lamv, qa, cache_k, kab, cache_v, vab)


def _pair_masks(q):
    lane = lax.broadcasted_iota(jnp.int32, q.shape, 1)
    zero = jnp.zeros_like(q)
    return lane, (jnp.where(lane < DB, q, zero), jnp.where(lane >= DB, q, zero))


def _battn_kernel(q_ref, kp_ref, kc_ref, vp_ref, vc_ref, strip_ref, o_ref, *, blk):
    i = pl.program_id(2)
    lane, qz = _pair_masks(q_ref[0])
    nsub = blk // BAND_SUB
    outs = []
    for e in range(2):
        def bias(first_col):
            return jnp.concatenate(
                [strip_ref[e, :, pl.ds((nsub - 1 - a) * BAND_SUB + first_col, blk)] for a in range(nsub)],
                axis=0)

        s_p = jnp.where(i > 0, _dot_nt(qz[e], kp_ref[0]) + bias(0), NEG_INF)
        s_c = _dot_nt(qz[e], kc_ref[0]) + bias(blk)
        m = jnp.maximum(jnp.max(s_p, axis=1, keepdims=True), jnp.max(s_c, axis=1, keepdims=True))
        p_p = jnp.exp(s_p - m)
        p_c = jnp.exp(s_c - m)
        l = jnp.sum(p_p, axis=1, keepdims=True) + jnp.sum(p_c, axis=1, keepdims=True)
        outs.append((_dot(p_p.astype(BF16), vp_ref[0]) + _dot(p_c.astype(BF16), vc_ref[0])) / l)
    o_ref[0] = jnp.where(lane < DB, outs[0], outs[1])


def _band_attention_prompt(qb, kbb, vbb, strip):
    b, s, _ = qb.shape
    blk = BAND_BLOCK
    assert s % blk == 0 and blk == BAND_PAST
    cur = pl.BlockSpec((1, blk, LANES), lambda pr, bb, i: (bb, i, pr))
    prev = pl.BlockSpec((1, blk, LANES), lambda pr, bb, i: (bb, jnp.maximum(i - 1, 0), pr))
    return pl.pallas_call(
        functools.partial(_battn_kernel, blk=blk),
        grid=(H_B // 2, b, s // blk),
        in_specs=[cur, prev, cur, prev, cur,
                  pl.BlockSpec((2,) + strip.shape[1:], lambda pr, bb, i: (pr, 0, 0))],
        out_specs=cur,
        out_shape=jax.ShapeDtypeStruct((b, s, W_B), F32),
        compiler_params=pltpu.CompilerParams(
            dimension_semantics=("parallel", "parallel", "parallel"), vmem_limit_bytes=VMEM_LIMIT),
        name="band_attention",
    )(qb, kbb, kbb, vbb, vbb, strip)


def _battn_sample_kernel(q_ref, kc_ref, kn_ref, vc_ref, vn_ref, biasc_ref, biasn_ref, o_ref,
                         kpad_sc, vpad_sc, *, t):
    kpad_sc[...] = jnp.zeros(kpad_sc.shape, BF16)
    vpad_sc[...] = jnp.zeros(vpad_sc.shape, BF16)
    kpad_sc[0:t, :] = kn_ref[0]
    vpad_sc[0:t, :] = vn_ref[0]
    pieces = []
    for pr in range(H_B // 2):
        cols = slice(pr * LANES, (pr + 1) * LANES)
        lane, qz = _pair_masks(q_ref[0, :, cols])
        kc = kc_ref[0, 0, :, cols].astype(BF16)
        vc = vc_ref[0, 0, :, cols].astype(BF16)
        kn = kpad_sc[:, cols]
        vn = vpad_sc[:, cols]
        outs = []
        for e in range(2):
            hd = 2 * pr + e
            s_c = _dot_nt(qz[e], kc) + biasc_ref[hd]
            s_n = _dot_nt(qz[e], kn) + biasn_ref[hd]
            m = jnp.maximum(jnp.max(s_c, axis=1, keepdims=True), jnp.max(s_n, axis=1, keepdims=True))
            p_c = jnp.exp(s_c - m)
            p_n = jnp.exp(s_n - m)
            l = jnp.sum(p_c, axis=1, keepdims=True) + jnp.sum(p_n, axis=1, keepdims=True)
            outs.append((_dot(p_c.astype(BF16), vc) + _dot(p_n.astype(BF16), vn)) / l)
        pieces.append(jnp.where(lane < DB, outs[0], outs[1]))
    o_ref[0] = jnp.concatenate(pieces, axis=1)


def _band_attention_sample(qb, kbb, vbb, cache_k, cache_v, layer, bias_c, bias_n):
    b, t, _ = qb.shape
    keep = cache_k.shape[2]
    new = pl.BlockSpec((1, t, W_B), lambda bb: (bb, 0, 0))
    old = pl.BlockSpec((1, 1, keep, W_B), lambda bb: (layer, bb, 0, 0))
    return pl.pallas_call(
        functools.partial(_battn_sample_kernel, t=t),
        grid=(b,),
        in_specs=[new, old, new, old, new, _const_spec(bias_c.shape), _const_spec(bias_n.shape)],
        out_specs=new,
        out_shape=jax.ShapeDtypeStruct((b, t, W_B), F32),
        scratch_shapes=[pltpu.VMEM((LANES, W_B), BF16), pltpu.VMEM((LANES, W_B), BF16)],
        compiler_params=pltpu.CompilerParams(dimension_semantics=("parallel",),
                                             vmem_limit_bytes=VMEM_LIMIT),
        name="band_attention_sample",
    )(qb, cache_k, kbb, cache_v, vbb, bias_c, bias_n)


def _band_bias(rel_bias, q0, nq, k0, nk, n_valid):
    qpos = q0 + np.arange(nq)[:, None]
    kidx = np.arange(nk)[None, :]
    kpos = k0 + kidx
    dchunk = qpos // CHUNK - kpos // CHUNK
    visible = (kpos >= 0) & (dchunk >= 0) & (dchunk <= BAND_CHUNKS) & (kidx < n_valid)
    dist = np.arange(q0 - k0 - (nk - 1), q0 - k0 + nq)
    vals = rel_bias.astype(F32)[:, np.clip(dist, -REL_CLIP, REL_CLIP) + REL_CLIP]
    w_pad = jnp.pad(vals[:, ::-1], ((0, 0), (0, 1)))
    width = nq + nk - 1
    skew = jnp.tile(w_pad, (1, nq))[:, :nq * width].reshape(-1, nq, width)
    return jnp.where(jnp.asarray(visible)[None], skew[:, :, nq - 1:nq - 1 + nk], NEG_INF)


def _merge_ffn_kernel(x_ref, oa_ref, ob_ref, g1_ref, gs_ref, wa_ref, wb_ref, wg_ref, wo_ref,
                      g2_ref, w1_ref, w2_ref, y_ref):
    x = x_ref[...]
    hb = _rms_rows(x, g1_ref[...]).astype(BF16)
    oa = oa_ref[...]
    heads = [_rms_rows(oa[:, c * DVA:(c + 1) * DVA], 1.0) for c in range(H_A)]
    oan = (jnp.concatenate(heads, axis=1) * gs_ref[...]).astype(BF16)
    ya = _dot(oan, wa_ref[...])
    yb = _dot(ob_ref[...].astype(BF16), wb_ref[...])
    gates = jax.nn.sigmoid(_dot(hb, wg_ref[...]))
    mix = gates[:, 0:D_MODEL] * ya + gates[:, D_MODEL:2 * D_MODEL] * yb
    x1 = x + _dot(mix.astype(BF16), wo_ref[...])
    h2 = _rms_rows(x1, g2_ref[...]).astype(BF16)
    u = jnp.maximum(_dot(h2, w1_ref[...]), 0.0)
    y_ref[...] = x1 + _dot((u * u).astype(BF16), w2_ref[...])


def _merge_ffn(x2d, oa2d, ob2d, g1, gs, wa, wb, wg, wo, g2, w1, w2):
    n = x2d.shape[0]
    tm = min(PROJ_ROWS, n)
    assert n % tm == 0
    row = lambda width: pl.BlockSpec((tm, width), lambda i: (i, 0))
    consts = (g1, gs, wa, wb, wg, wo, g2, w1, w2)
    return pl.pallas_call(
        _merge_ffn_kernel,
        grid=(n // tm,),
        in_specs=[row(D_MODEL), row(W_A), row(W_B)] + [_const_spec(c.shape) for c in consts],
        out_specs=row(D_MODEL),
        out_shape=jax.ShapeDtypeStruct((n, D_MODEL), F32),
        compiler_params=pltpu.CompilerParams(dimension_semantics=("parallel",),
                                             vmem_limit_bytes=VMEM_LIMIT),
        name="merge_ffn",
    )(x2d, oa2d, ob2d, *consts)


def kernel(x_prompt, x_sample, cache_a_k, cache_a_v, cache_b_k, cache_b_v, norm1_g, w_in, qn_a_g, kn_a_g, qn_b_g, kn_b_g, lam_q1, lam_k1, lam_q2, lam_k2, subln_a_g, rel_bias_b, w_br_a, w_br_b, w_gate, w_out, norm2_g, w_ff1, w_ff2):
    bp, s, d = x_prompt.shape
    bs, t, _ = x_sample.shape
    depth = w_in.shape[0]
    p_len = cache_a_k.shape[2]
    b_keep = cache_b_k.shape[2]
    keep_p = min(BAND_PAST, s)
    cdt = cache_a_k.dtype

    slopes = jnp.exp2(-8.0 * jnp.arange(1, H_A + 1, dtype=F32) / H_A)
    group = np.arange(NORM_CHUNK) // DA
    ones = jnp.asarray(group[:, None] == group[None, :], BF16)
    row = lambda v, reps, scale=1.0: (jnp.tile(v.astype(F32), reps) * scale)[None, :]
    cache_bk = cache_b_k.reshape(depth, bs, b_keep, W_B)
    cache_bv = cache_b_v.reshape(depth, bs, b_keep, W_B)

    xp = x_prompt.astype(F32).reshape(bp * s, d)
    xs = x_sample.astype(F32).reshape(bs * t, d)
    a_caches_p = ()
    a_caches_s = ()
    b_outs = [[] for _ in range(4)]
    for l in range(depth):
        lam_init = 0.8 - 0.6 * math.exp(-0.3 * l)
        lamv = jnp.stack([lam_q1[l], lam_k1[l], lam_q2[l], lam_k2[l]]).astype(F32)
        g1 = row(norm1_g[l], 1)
        proj_consts = (g1, w_in[l].astype(BF16), ones,
                       row(qn_a_g[l], 2 * H_A, DA ** -0.5 * LOG2E), row(kn_a_g[l], 2 * H_A),
                       row(qn_b_g[l], H_B, DB ** -0.5), row(kn_b_g[l], H_B))
        tail = (g1, row(subln_a_g[l], H_A, 1.0 - lam_init), w_br_a[l].astype(BF16),
                w_br_b[l].astype(BF16), w_gate[l].astype(BF16), w_out[l].astype(BF16),
                row(norm2_g[l], 1), w_ff1[l].astype(BF16), w_ff2[l].astype(BF16))

        qa, kaf, kab, vaf, vab, qb, kbf, kbb, vbf, vbb = _projection(xp, *proj_consts, l, depth, a_caches_p)
        a_caches_p = (kaf, vaf)
        r3 = lambda a, n: a.reshape(n, -1, a.shape[-1])
        oa = _diff_attention_prompt(r3(qa, bp), r3(kab, bp), r3(vab, bp), slopes, lamv, lam_init)
        ext = BAND_BLOCK - BAND_SUB
        strip = _band_bias(rel_bias_b[l], BAND_PAST + ext, BAND_SUB, 0, 2 * BAND_BLOCK + ext,
                           2 * BAND_BLOCK + ext)
        ob = _band_attention_prompt(r3(qb, bp), r3(kbb, bp), r3(vbb, bp), strip)
        xp = _merge_ffn(xp, oa.reshape(bp * s, W_A), ob.reshape(bp * s, W_B), *tail)
        b_outs[0].append(kbf.reshape(bp, s, H_B, DB)[:, s - keep_p:])
        b_outs[1].append(vbf.reshape(bp, s, H_B, DB)[:, s - keep_p:])

        qa, kaf, kab, vaf, vab, qb, kbf, kbb, vbf, vbb = _projection(xs, *proj_consts, l, depth, a_caches_s)
        a_caches_s = (kaf, vaf)
        oa = _diff_attention_sample(r3(qa, bs), r3(kab, bs), r3(vab, bs), cache_a_k, cache_a_v, l,
                                    slopes, lamv, lam_init)
        bias_c = _band_bias(rel_bias_b[l], p_len, t, p_len - b_keep, b_keep, b_keep)
        bias_n = _band_bias(rel_bias_b[l], p_len, t, p_len, LANES, t)
        ob = _band_attention_sample(r3(qb, bs), r3(kbb, bs), r3(vbb, bs), cache_bk, cache_bv, l,
                                    bias_c, bias_n)
        xs = _merge_ffn(xs, oa.reshape(bs * t, W_A), ob.reshape(bs * t, W_B), *tail)
        b_outs[2].append(kbf.reshape(bs, t, H_B, DB))
        b_outs[3].append(vbf.reshape(bs, t, H_B, DB))

    bk_p, bv_p, bk_s, bv_s = [jnp.stack(o).astype(cdt) for o in b_outs]
    ak_p, av_p = [a.reshape(depth, bp, s, H_A, DVA).astype(cdt) for a in a_caches_p]
    ak_s, av_s = [a.reshape(depth, bs, t, H_A, DVA).astype(cdt) for a in a_caches_s]
    return (xp.reshape(bp, s, d).astype(x_prompt.dtype), xs.reshape(bs, t, d).astype(x_sample.dtype),
            ak_p, av_p, bk_p, bv_p, ak_s, av_s, bk_s, bv_s)
```

```python
import functools
import math

import numpy as np
import jax
import jax.numpy as jnp
from jax import lax
from jax.experimental import pallas as pl
from jax.experimental.pallas import tpu as pltpu

F32 = jnp.float32
BF16 = jnp.bfloat16

D_MODEL = 1024
CHUNK = 64
H_A = 8
DA = 64
DVA = 2 * DA
H_B = 8
DB = 64
BAND_CHUNKS = 8
BAND_PAST = BAND_CHUNKS * CHUNK
REL_CLIP = 128
D_FF = 4 * D_MODEL
EPS = 1e-6
NEG_INF = -1e30
W_A = H_A * 2 * DA
W_B = H_B * DB

LANES = 128
NORM_CHUNK = 256
VMEM_LIMIT = 56 * 1024 * 1024

PROJ_ROWS = 256
ATT_Q_BLOCK = 1024
LOG2E = 1.4426950408889634
N_BIAS_COLS = 3
SUM_ROWS = 16
BAND_BLOCK = 512
BAND_TILE = 256


def _rms_rows(x, gain_row):
    return x * lax.rsqrt(jnp.mean(x * x, axis=-1, keepdims=True) + EPS) * gain_row


def _dot(a, b):
    return jnp.dot(a, b, preferred_element_type=F32)


def _dot_nt(a, b):
    return lax.dot_general(a, b, (((1,), (1,)), ((), ())), preferred_element_type=F32)


def _head_norm(z, group_ones, gain_row):
    outs = []
    for c in range(z.shape[1] // NORM_CHUNK):
        zc = z[:, c * NORM_CHUNK:(c + 1) * NORM_CHUNK]
        ss = _dot((zc * zc).astype(BF16), group_ones)
        outs.append(zc * lax.rsqrt(ss * (1.0 / DA) + EPS))
    return jnp.concatenate(outs, axis=1) * gain_row


def _proj_kernel(x_ref, g1_ref, w_ref, ones_ref, gqa_ref, gka_ref, gqb_ref, gkb_ref, *rest):
    (qa_ref, kaf_ref, kab_ref, vaf_ref, vab_ref,
     qb_ref, kbf_ref, kbb_ref, vbf_ref, vbb_ref) = rest[-10:]
    hb = _rms_rows(x_ref[...], g1_ref[...]).astype(BF16)
    ones = ones_ref[...]

    def store_heads(dst_ref, z):
        dst_ref[0] = z.reshape(z.shape[0], H_A, DVA)

    qa = _head_norm(_dot(hb, w_ref[:, 0:W_A]), ones, gqa_ref[...])
    qa_ref[...] = qa.astype(BF16)
    ka = _head_norm(_dot(hb, w_ref[:, W_A:2 * W_A]), ones, gka_ref[...])
    store_heads(kaf_ref, ka)
    kab_ref[...] = ka.astype(BF16)
    va = _dot(hb, w_ref[:, 2 * W_A:3 * W_A])
    store_heads(vaf_ref, va)
    vab_ref[...] = va.astype(BF16)

    o = 3 * W_A
    qb = _head_norm(_dot(hb, w_ref[:, o:o + W_B]), ones, gqb_ref[...])
    qb_ref[...] = qb.astype(BF16)
    kb = _head_norm(_dot(hb, w_ref[:, o + W_B:o + 2 * W_B]), ones, gkb_ref[...])
    kbf_ref[...] = kb
    kbb_ref[...] = kb.astype(BF16)
    vb = _dot(hb, w_ref[:, o + 2 * W_B:o + 3 * W_B])
    vbf_ref[...] = vb
    vbb_ref[...] = vb.astype(BF16)


def _const_spec(shape):
    return pl.BlockSpec(shape, lambda *_: (0,) * len(shape), pipeline_mode=pl.Buffered(1))


def _projection(x2d, g1, w_bf, ones, gqa, gka, gqb, gkb, layer, depth, caches):
    n = x2d.shape[0]
    tm = min(PROJ_ROWS, n)
    assert n % tm == 0
    row = lambda width: pl.BlockSpec((tm, width), lambda i: (i, 0))
    slab = pl.BlockSpec((1, tm, H_A, DVA), lambda i: (layer, i, 0, 0))
    out_shapes = []
    out_specs = []
    for width, dtypes in ((W_A, (BF16,)), (W_A, (F32, BF16)), (W_A, (F32, BF16)),
                          (W_B, (BF16,)), (W_B, (F32, BF16)), (W_B, (F32, BF16))):
        for dt in dtypes:
            if width == W_A and dt == F32:
                out_shapes.append(jax.ShapeDtypeStruct((depth, n, H_A, DVA), F32))
                out_specs.append(slab)
            else:
                out_shapes.append(jax.ShapeDtypeStruct((n, width), dt))
                out_specs.append(row(width))
    n_in = 8
    return pl.pallas_call(
        _proj_kernel,
        grid=(n // tm,),
        in_specs=[row(D_MODEL), _const_spec((1, D_MODEL)), _const_spec(w_bf.shape),
                  _const_spec(ones.shape), _const_spec((1, W_A)), _const_spec((1, W_A)),
                  _const_spec((1, W_B)), _const_spec((1, W_B))]
                 + [pl.BlockSpec(memory_space=pl.ANY)] * len(caches),
        out_specs=out_specs,
        out_shape=out_shapes,
        input_output_aliases={n_in: 1, n_in + 1: 3} if caches else {},
        compiler_params=pltpu.CompilerParams(dimension_semantics=("parallel",),
                                             vmem_limit_bytes=VMEM_LIMIT),
        name="projection",
    )(x2d, g1, w_bf, ones, gqa, gka, gqb, gkb, *caches)


def _split_maps(q):
    lane = lax.broadcasted_iota(jnp.int32, q.shape, 1)
    zero = jnp.zeros_like(q)
    return jnp.concatenate([jnp.where(lane < DA, q, zero), jnp.where(lane >= DA, q, zero)], axis=0)


def _lambda(lamv, lam_init):
    a = jnp.sum(lamv[0:1, :] * lamv[1:2, :], axis=1, keepdims=True)
    b = jnp.sum(lamv[2:3, :] * lamv[3:4, :], axis=1, keepdims=True)
    return jnp.exp(a) - jnp.exp(b) + lam_init


def _dattn_kernel(slopes_ref, lamv_ref, q_ref, k_ref, v_ref, o_ref,
                  qz_sc, kb_sc, db_sc, sa_sc, sb_sc, ma_sc, mb_sc, m_sc, acc_sc, *, qblk, kblk, lam_init):
    h = pl.program_id(1)
    i = pl.program_id(2)
    own = qblk // kblk
    sl2 = slopes_ref[h] * LOG2E

    @pl.when(i == 0)
    def _head_tables():
        qz_sc[LANES:2 * LANES, :] = (
            lax.broadcasted_iota(jnp.int32, (LANES, 2 * qblk), 0) < N_BIAS_COLS).astype(BF16)
        r = lax.broadcasted_iota(jnp.int32, (kblk, LANES), 0)
        c = lax.broadcasted_iota(jnp.int32, (kblk, LANES), 1)
        rest = sl2 * r.astype(F32)
        kb = jnp.zeros((kblk, LANES), F32)
        for col in range(N_BIAS_COLS):
            part = rest.astype(BF16).astype(F32)
            kb = jnp.where(c == col, part, kb)
            rest = rest - part
        kb_sc[...] = kb.astype(BF16)
        ki = lax.broadcasted_iota(jnp.int32, (qblk, qblk), 0)
        qi = lax.broadcasted_iota(jnp.int32, (qblk, qblk), 1)
        db_sc[...] = jnp.where((ki // CHUNK) <= (qi // CHUNK),
                               sl2 * (qi - jnp.abs(qi - ki) - ki).astype(F32), NEG_INF)

    qt = q_ref[0].T
    row = lax.broadcasted_iota(jnp.int32, qt.shape, 0)
    zero = jnp.zeros_like(qt)
    qz_sc[0:LANES, 0:qblk] = jnp.where(row < DA, qt, zero)
    qz_sc[0:LANES, qblk:2 * qblk] = jnp.where(row >= DA, qt, zero)

    m_sc[...] = jnp.full(m_sc.shape, -jnp.inf, F32)
    acc_sc[...] = jnp.zeros(acc_sc.shape, F32)
    ones_rows = jnp.ones((SUM_ROWS, kblk), BF16)

    def rows(j):
        return pl.ds(pl.multiple_of(j * kblk, kblk), kblk)

    def scores(j, dst, max_dst):
        s = _dot(jnp.concatenate([k_ref[0, rows(j), :], kb_sc[...]], axis=1), qz_sc[...])
        dst[...] = s
        max_dst[...] = jnp.max(s, axis=0, keepdims=True)

    def softmax_pv(j, s, s_max):
        off = sl2 * (j * kblk - i * qblk).astype(F32)
        m_prev = m_sc[...] - off
        m_new = jnp.maximum(m_prev, s_max)
        alpha = jnp.exp2(m_prev - m_new)
        pb = jnp.exp2(s - m_new).astype(BF16)
        vt = jnp.concatenate([v_ref[0, rows(j), :].T, ones_rows], axis=0)
        pv = jnp.concatenate([_dot(vt, pb[:, 0:qblk]), _dot(vt, pb[:, qblk:2 * qblk])], axis=1)
        acc_sc[...] = alpha * acc_sc[...] + pv
        m_sc[...] = m_new + off

    def own_bias(d, s):
        db = db_sc[d * kblk:(d + 1) * kblk, :]
        return jnp.concatenate([s[:, 0:qblk] + db, s[:, qblk:2 * qblk] + db], axis=1)

    first_own = own * i
    scores(0, sa_sc, ma_sc)

    def pair(jj, carry):
        j0 = 2 * jj
        scores(j0 + 1, sb_sc, mb_sc)
        softmax_pv(j0, sa_sc[...], ma_sc[...])
        scores(j0 + 2, sa_sc, ma_sc)
        softmax_pv(j0 + 1, sb_sc[...], mb_sc[...])
        return carry

    lax.fori_loop(0, first_own // 2, pair, 0)

    def own_softmax_pv(d, buf):
        s = own_bias(d, buf[...])
        softmax_pv(first_own + d, s, jnp.max(s, axis=0, keepdims=True))

    for d in range(0, own, 2):
        scores(first_own + d + 1, sb_sc, mb_sc)
        own_softmax_pv(d, sa_sc)
        if d + 2 < own:
            scores(first_own + d + 2, sa_sc, ma_sc)
        own_softmax_pv(d + 1, sb_sc)

    out = acc_sc[0:DVA, :] / acc_sc[DVA:DVA + 1, :]
    lam = _lambda(lamv_ref[...], lam_init)
    o_ref[0] = (out[:, 0:qblk] - lam * out[:, qblk:2 * qblk]).T


def _diff_attention_prompt(qa, kab, vab, slopes, lamv, lam_init):
    b, s, _ = qa.shape
    qblk = min(ATT_Q_BLOCK, s)
    kblk = qblk // 2
    assert s % qblk == 0 and kblk % CHUNK == 0
    head_rows = pl.BlockSpec((1, s, DVA), lambda bb, h, i: (bb, 0, h))
    block = pl.BlockSpec((1, qblk, DVA), lambda bb, h, i: (bb, i, h))
    return pl.pallas_call(
        functools.partial(_dattn_kernel, qblk=qblk, kblk=kblk, lam_init=lam_init),
        grid=(b, H_A, s // qblk),
        in_specs=[pl.BlockSpec(memory_space=pltpu.SMEM), pl.BlockSpec((4, DA), lambda bb, h, i: (0, 0)),
                  block, head_rows, head_rows],
        out_specs=block,
        out_shape=jax.ShapeDtypeStruct((b, s, W_A), F32),
        scratch_shapes=[pltpu.VMEM((2 * LANES, 2 * qblk), BF16), pltpu.VMEM((kblk, LANES), BF16),
                        pltpu.VMEM((qblk, qblk), F32),
                        pltpu.VMEM((kblk, 2 * qblk), F32), pltpu.VMEM((kblk, 2 * qblk), F32),
                        pltpu.VMEM((1, 2 * qblk), F32), pltpu.VMEM((1, 2 * qblk), F32),
                        pltpu.VMEM((1, 2 * qblk), F32), pltpu.VMEM((DVA + SUM_ROWS, 2 * qblk), F32)],
        compiler_params=pltpu.CompilerParams(
            dimension_semantics=("parallel", "parallel", "arbitrary"), vmem_limit_bytes=VMEM_LIMIT),
        name="diff_attention",
    )(slopes, lamv, qa, kab, vab)


def _dattn_sample_kernel(slopes_ref, lamv_ref, q_ref, kc_ref, kn_ref, vc_ref, vn_ref, o_ref,
                         kpad_sc, vpad_sc, *, t, p_len, lam_init):
    kpad_sc[...] = jnp.zeros(kpad_sc.shape, BF16)
    vpad_sc[...] = jnp.zeros(vpad_sc.shape, BF16)
    kpad_sc[0:t, :] = kn_ref[0]
    vpad_sc[0:t, :] = vn_ref[0]
    lam = _lambda(lamv_ref[...], lam_init)

    def distance(k0, nk, n_valid):
        qpos = lax.broadcasted_iota(jnp.int32, (t, nk), 0) + p_len
        kidx = lax.broadcasted_iota(jnp.int32, (t, nk), 1)
        kpos = kidx + k0
        visible = ((kpos // CHUNK) <= (qpos // CHUNK)) & (kidx < n_valid)
        return visible, jnp.abs(qpos - kpos).astype(F32)

    vis_c, dist_c = distance(0, p_len, p_len)
    vis_n, dist_n = distance(p_len, LANES, t)
    kc_all = kc_ref[0, 0].reshape(p_len, W_A).astype(BF16)
    vc_all = vc_ref[0, 0].reshape(p_len, W_A).astype(BF16)
    outs = []
    for hd in range(H_A):
        cols = slice(hd * DVA, (hd + 1) * DVA)
        slope = slopes_ref[hd] * LOG2E
        qz = _split_maps(q_ref[0, :, cols])

        def scores(k, vis, dist):
            s = _dot_nt(qz, k)
            bias_t = jnp.where(vis, -slope * dist, NEG_INF)
            return (s.reshape(2, t, -1) + bias_t[None]).reshape(2 * t, -1)

        s_c = scores(kc_all[:, cols], vis_c, dist_c)
        s_n = scores(kpad_sc[:, cols], vis_n, dist_n)
        m = jnp.maximum(jnp.max(s_c, axis=1, keepdims=True), jnp.max(s_n, axis=1, keepdims=True))
        p_c = jnp.exp2(s_c - m)
        p_n = jnp.exp2(s_n - m)
        l = jnp.sum(p_c, axis=1, keepdims=True) + jnp.sum(p_n, axis=1, keepdims=True)
        out = (_dot(p_c.astype(BF16), vc_all[:, cols]) + _dot(p_n.astype(BF16), vpad_sc[:, cols])) / l
        outs.append(out[0:t] - lam * out[t:2 * t])
    o_ref[0] = jnp.concatenate(outs, axis=1)


def _diff_attention_sample(qa, kab, vab, cache_k, cache_v, layer, slopes, lamv, lam_init):
    b, t, _ = qa.shape
    p_len = cache_k.shape[2]
    assert t <= LANES
    new = pl.BlockSpec((1, t, W_A), lambda bb: (bb, 0, 0))
    old = pl.BlockSpec((1, 1, p_len, H_A, DVA), lambda bb: (layer, bb, 0, 0, 0))
    return pl.pallas_call(
        functools.partial(_dattn_sample_kernel, t=t, p_len=p_len, lam_init=lam_init),
        grid=(b,),
        in_specs=[pl.BlockSpec(memory_space=pltpu.SMEM), pl.BlockSpec((4, DA), lambda bb: (0, 0)),
                  new, old, new, old, new],
        out_specs=new,
        out_shape=jax.ShapeDtypeStruct((b, t, W_A), F32),
        scratch_shapes=[pltpu.VMEM((LANES, W_A), BF16), pltpu.VMEM((LANES, W_A), BF16)],
        compiler_params=pltpu.CompilerParams(dimension_semantics=("parallel",),
                                             vmem_limit_bytes=VMEM_LIMIT),
        name="diff_attention_sample",
    )(slopes, lamv, qa, cache_k, kab, cache_v, vab)


def _pair_masks(q):
    lane = lax.broadcasted_iota(jnp.int32, q.shape, 1)
    zero = jnp.zeros_like(q)
    return lane, (jnp.where(lane < DB, q, zero), jnp.where(lane >= DB, q, zero))


def _battn_kernel(q_ref, kp_ref, kc_ref, vp_ref, vc_ref, bias_ref, o_ref, sa_sc, sb_sc, *, blk):
    i = pl.program_id(2)
    win = BAND_PAST + BAND_TILE
    qt = q_ref[0].T
    row = lax.broadcasted_iota(jnp.int32, qt.shape, 0)
    zero = jnp.zeros_like(qt)
    qz = (jnp.where(row < DB, qt, zero), jnp.where(row >= DB, qt, zero))
    ones_rows = jnp.ones((SUM_ROWS, win), BF16)
    key_row = lax.broadcasted_iota(jnp.int32, (win, BAND_TILE), 0)
    out_row = lax.broadcasted_iota(jnp.int32, (LANES, BAND_TILE), 0)

    def window(prev_ref, cur_ref, lo):
        return jnp.concatenate([prev_ref[0, lo:blk, :], cur_ref[0, 0:lo + BAND_TILE, :]], axis=0)

    def scores(item, dst):
        lo, e = item
        dst[...] = _dot(window(kp_ref, kc_ref, lo), qz[e][:, lo:lo + BAND_TILE])

    def softmax_pv(item, src):
        lo, e = item
        n_prev = blk - lo
        s = jnp.where((i > 0) | (key_row >= n_prev), src[...] + bias_ref[e], NEG_INF)
        pb = jnp.exp2(s - jnp.max(s, axis=0, keepdims=True)).astype(BF16)
        vt = jnp.concatenate([window(vp_ref, vc_ref, lo).T, ones_rows], axis=0)
        pv = _dot(vt, pb)
        return pv[0:LANES] / pv[LANES:LANES + 1]

    items = [(nt * BAND_TILE, e) for nt in range(blk // BAND_TILE) for e in range(2)]
    bufs = (sa_sc, sb_sc)
    outs = []
    scores(items[0], bufs[0])
    for n, item in enumerate(items):
        if n + 1 < len(items):
            scores(items[n + 1], bufs[(n + 1) % 2])
        outs.append(softmax_pv(item, bufs[n % 2]))
    tiles = [jnp.where(out_row < DB, outs[2 * nt], outs[2 * nt + 1]) for nt in range(blk // BAND_TILE)]
    o_ref[0] = jnp.concatenate(tiles, axis=1).T


def _band_attention_prompt(qb, kbb, vbb, bias_t):
    b, s, _ = qb.shape
    blk = BAND_BLOCK
    assert s % blk == 0 and blk == BAND_PAST and blk % BAND_TILE == 0
    cur = pl.BlockSpec((1, blk, LANES), lambda pr, bb, i: (bb, i, pr))
    prev = pl.BlockSpec((1, blk, LANES), lambda pr, bb, i: (bb, jnp.maximum(i - 1, 0), pr))
    return pl.pallas_call(
        functools.partial(_battn_kernel, blk=blk),
        grid=(H_B // 2, b, s // blk),
        in_specs=[cur, prev, cur, prev, cur,
                  pl.BlockSpec((2,) + bias_t.shape[1:], lambda pr, bb, i: (pr, 0, 0))],
        out_specs=cur,
        out_shape=jax.ShapeDtypeStruct((b, s, W_B), F32),
        scratch_shapes=[pltpu.VMEM(bias_t.shape[1:], F32), pltpu.VMEM(bias_t.shape[1:], F32)],
        compiler_params=pltpu.CompilerParams(
            dimension_semantics=("parallel", "parallel", "parallel"), vmem_limit_bytes=VMEM_LIMIT),
        name="band_attention",
    )(qb, kbb, kbb, vbb, vbb, bias_t)


def _battn_sample_kernel(q_ref, kc_ref, kn_ref, vc_ref, vn_ref, biasc_ref, biasn_ref, o_ref,
                         kpad_sc, vpad_sc, *, t):
    kpad_sc[...] = jnp.zeros(kpad_sc.shape, BF16)
    vpad_sc[...] = jnp.zeros(vpad_sc.shape, BF16)
    kpad_sc[0:t, :] = kn_ref[0]
    vpad_sc[0:t, :] = vn_ref[0]
    pieces = []
    for pr in range(H_B // 2):
        cols = slice(pr * LANES, (pr + 1) * LANES)
        lane, qz = _pair_masks(q_ref[0, :, cols])
        kc = kc_ref[0, 0, :, cols].astype(BF16)
        vc = vc_ref[0, 0, :, cols].astype(BF16)
        kn = kpad_sc[:, cols]
        vn = vpad_sc[:, cols]
        outs = []
        for e in range(2):
            hd = 2 * pr + e
            s_c = _dot_nt(qz[e], kc) + biasc_ref[hd]
            s_n = _dot_nt(qz[e], kn) + biasn_ref[hd]
            m = jnp.maximum(jnp.max(s_c, axis=1, keepdims=True), jnp.max(s_n, axis=1, keepdims=True))
            p_c = jnp.exp2(s_c - m)
            p_n = jnp.exp2(s_n - m)
            l = jnp.sum(p_c, axis=1, keepdims=True) + jnp.sum(p_n, axis=1, keepdims=True)
            outs.append((_dot(p_c.astype(BF16), vc) + _dot(p_n.astype(BF16), vn)) / l)
        pieces.append(jnp.where(lane < DB, outs[0], outs[1]))
    o_ref[0] = jnp.concatenate(pieces, axis=1)


def _band_attention_sample(qb, kbb, vbb, cache_k, cache_v, layer, bias_c, bias_n):
    b, t, _ = qb.shape
    keep = cache_k.shape[2]
    new = pl.BlockSpec((1, t, W_B), lambda bb: (bb, 0, 0))
    old = pl.BlockSpec((1, 1, keep, W_B), lambda bb: (layer, bb, 0, 0))
    return pl.pallas_call(
        functools.partial(_battn_sample_kernel, t=t),
        grid=(b,),
        in_specs=[new, old, new, old, new, _const_spec(bias_c.shape), _const_spec(bias_n.shape)],
        out_specs=new,
        out_shape=jax.ShapeDtypeStruct((b, t, W_B), F32),
        scratch_shapes=[pltpu.VMEM((LANES, W_B), BF16), pltpu.VMEM((LANES, W_B), BF16)],
        compiler_params=pltpu.CompilerParams(dimension_semantics=("parallel",),
                                             vmem_limit_bytes=VMEM_LIMIT),
        name="band_attention_sample",
    )(qb, cache_k, kbb, cache_v, vbb, bias_c, bias_n)


def _band_bias(rel_bias, q0, nq, k0, nk, n_valid):
    qpos = q0 + np.arange(nq)[:, None]
    kidx = np.arange(nk)[None, :]
    kpos = k0 + kidx
    dchunk = qpos // CHUNK - kpos // CHUNK
    visible = (kpos >= 0) & (dchunk >= 0) & (dchunk <= BAND_CHUNKS) & (kidx < n_valid)
    dist = np.arange(q0 - k0 - (nk - 1), q0 - k0 + nq)
    vals = rel_bias.astype(F32)[:, np.clip(dist, -REL_CLIP, REL_CLIP) + REL_CLIP]
    w_pad = jnp.pad(vals[:, ::-1], ((0, 0), (0, 1)))
    width = nq + nk - 1
    skew = jnp.tile(w_pad, (1, nq))[:, :nq * width].reshape(-1, nq, width)
    return jnp.where(jnp.asarray(visible)[None], skew[:, :, nq - 1:nq - 1 + nk], NEG_INF)


def _merge_ffn_kernel(x_ref, oa_ref, ob_ref, g1_ref, gs_ref, wa_ref, wb_ref, wg_ref, wo_ref,
                      g2_ref, w1_ref, w2_ref, y_ref):
    x = x_ref[...]
    hb = _rms_rows(x, g1_ref[...]).astype(BF16)
    oa = oa_ref[...]
    heads = [_rms_rows(oa[:, c * DVA:(c + 1) * DVA], 1.0) for c in range(H_A)]
    oan = (jnp.concatenate(heads, axis=1) * gs_ref[...]).astype(BF16)
    ya = _dot(oan, wa_ref[...])
    yb = _dot(ob_ref[...].astype(BF16), wb_ref[...])
    gates = jax.nn.sigmoid(_dot(hb, wg_ref[...]))
    mix = gates[:, 0:D_MODEL] * ya + gates[:, D_MODEL:2 * D_MODEL] * yb
    x1 = x + _dot(mix.astype(BF16), wo_ref[...])
    h2 = _rms_rows(x1, g2_ref[...]).astype(BF16)
    u = jnp.maximum(_dot(h2, w1_ref[...]), 0.0)
    y_ref[...] = x1 + _dot((u * u).astype(BF16), w2_ref[...])


def _merge_ffn(x2d, oa2d, ob2d, g1, gs, wa, wb, wg, wo, g2, w1, w2):
    n = x2d.shape[0]
    tm = min(PROJ_ROWS, n)
    assert n % tm == 0
    row = lambda width: pl.BlockSpec((tm, width), lambda i: (i, 0))
    consts = (g1, gs, wa, wb, wg, wo, g2, w1, w2)
    return pl.pallas_call(
        _merge_ffn_kernel,
        grid=(n // tm,),
        in_specs=[row(D_MODEL), row(W_A), row(W_B)] + [_const_spec(c.shape) for c in consts],
        out_specs=row(D_MODEL),
        out_shape=jax.ShapeDtypeStruct((n, D_MODEL), F32),
        compiler_params=pltpu.CompilerParams(dimension_semantics=("parallel",),
                                             vmem_limit_bytes=VMEM_LIMIT),
        name="merge_ffn",
    )(x2d, oa2d, ob2d, *consts)


def kernel(x_prompt, x_sample, cache_a_k, cache_a_v, cache_b_k, cache_b_v, norm1_g, w_in, qn_a_g, kn_a_g, qn_b_g, kn_b_g, lam_q1, lam_k1, lam_q2, lam_k2, subln_a_g, rel_bias_b, w_br_a, w_br_b, w_gate, w_out, norm2_g, w_ff1, w_ff2):
    bp, s, d = x_prompt.shape
    bs, t, _ = x_sample.shape
    depth = w_in.shape[0]
    p_len = cache_a_k.shape[2]
    b_keep = cache_b_k.shape[2]
    keep_p = min(BAND_PAST, s)
    cdt = cache_a_k.dtype

    slopes = jnp.exp2(-8.0 * jnp.arange(1, H_A + 1, dtype=F32) / H_A)
    group = np.arange(NORM_CHUNK) // DA
    ones = jnp.asarray(group[:, None] == group[None, :], BF16)
    row = lambda v, reps, scale=1.0: (jnp.tile(v.astype(F32), reps) * scale)[None, :]
    cache_bk = cache_b_k.reshape(depth, bs, b_keep, W_B)
    cache_bv = cache_b_v.reshape(depth, bs, b_keep, W_B)

    xp = x_prompt.astype(F32).reshape(bp * s, d)
    xs = x_sample.astype(F32).reshape(bs * t, d)
    a_caches_p = ()
    a_caches_s = ()
    b_outs = [[] for _ in range(4)]
    for l in range(depth):
        lam_init = 0.8 - 0.6 * math.exp(-0.3 * l)
        lamv = jnp.stack([lam_q1[l], lam_k1[l], lam_q2[l], lam_k2[l]]).astype(F32)
        g1 = row(norm1_g[l], 1)
        proj_consts = (g1, w_in[l].astype(BF16), ones,
                       row(qn_a_g[l], 2 * H_A, DA ** -0.5 * LOG2E), row(kn_a_g[l], 2 * H_A),
                       row(qn_b_g[l], H_B, DB ** -0.5 * LOG2E), row(kn_b_g[l], H_B))
        rel_l2 = rel_bias_b[l].astype(F32) * LOG2E
        tail = (g1, row(subln_a_g[l], H_A, 1.0 - lam_init), w_br_a[l].astype(BF16),
                w_br_b[l].astype(BF16), w_gate[l].astype(BF16), w_out[l].astype(BF16),
                row(norm2_g[l], 1), w_ff1[l].astype(BF16), w_ff2[l].astype(BF16))

        qa, kaf, kab, vaf, vab, qb, kbf, kbb, vbf, vbb = _projection(xp, *proj_consts, l, depth, a_caches_p)
        a_caches_p = (kaf, vaf)
        r3 = lambda a, n: a.reshape(n, -1, a.shape[-1])
        oa = _diff_attention_prompt(r3(qa, bp), r3(kab, bp), r3(vab, bp), slopes, lamv, lam_init)
        win = BAND_PAST + BAND_TILE
        bias_t = _band_bias(rel_l2, BAND_PAST, BAND_TILE, 0, win, win).transpose(0, 2, 1)
        ob = _band_attention_prompt(r3(qb, bp), r3(kbb, bp), r3(vbb, bp), bias_t)
        xp = _merge_ffn(xp, oa.reshape(bp * s, W_A), ob.reshape(bp * s, W_B), *tail)
        b_outs[0].append(r3(kbf, bp)[:, s - keep_p:].reshape(bp, keep_p, H_B, DB))
        b_outs[1].append(r3(vbf, bp)[:, s - keep_p:].reshape(bp, keep_p, H_B, DB))

        qa, kaf, kab, vaf, vab, qb, kbf, kbb, vbf, vbb = _projection(xs, *proj_consts, l, depth, a_caches_s)
        a_caches_s = (kaf, vaf)
        oa = _diff_attention_sample(r3(qa, bs), r3(kab, bs), r3(vab, bs), cache_a_k, cache_a_v, l,
                                    slopes, lamv, lam_init)
        bias_c = _band_bias(rel_l2, p_len, t, p_len - b_keep, b_keep, b_keep)
        bias_n = _band_bias(rel_l2, p_len, t, p_len, LANES, t)
        ob = _band_attention_sample(r3(qb, bs), r3(kbb, bs), r3(vbb, bs), cache_bk, cache_bv, l,
                                    bias_c, bias_n)
        xs = _merge_ffn(xs, oa.reshape(bs * t, W_A), ob.reshape(bs * t, W_B), *tail)
        b_outs[2].append(kbf.reshape(bs, t, H_B, DB))
        b_outs[3].append(vbf.reshape(bs, t, H_B, DB))

    bk_p, bv_p, bk_s, bv_s = [jnp.stack(o).astype(cdt) for o in b_outs]
    ak_p, av_p = [a.reshape(depth, bp, s, H_A, DVA).astype(cdt) for a in a_caches_p]
    ak_s, av_s = [a.reshape(depth, bs, t, H_A, DVA).astype(cdt) for a in a_caches_s]
    return (xp.reshape(bp, s, d).astype(x_prompt.dtype), xs.reshape(bs, t, d).astype(x_sample.dtype),
            ak_p, av_p, bk_p, bv_p, ak_s, av_s, bk_s, bv_s)
```

```python
import functools
import math

import numpy as np
import jax
import jax.numpy as jnp
from jax import lax
from jax.experimental import pallas as pl
from jax.experimental.pallas import tpu as pltpu

F32 = jnp.float32
BF16 = jnp.bfloat16

D_MODEL = 1024
CHUNK = 64
H_A = 8
DA = 64
DVA = 2 * DA
H_B = 8
DB = 64
BAND_CHUNKS = 8
BAND_PAST = BAND_CHUNKS * CHUNK
REL_CLIP = 128
D_FF = 4 * D_MODEL
EPS = 1e-6
NEG_INF = -1e30
W_A = H_A * 2 * DA
W_B = H_B * DB

LANES = 128
NORM_CHUNK = 256
VMEM_LIMIT = 56 * 1024 * 1024

PROJ_ROWS = 256
ATT_Q_BLOCK = 1024
LOG2E = 1.4426950408889634
N_BIAS_COLS = 3
SUM_ROWS = 16
BAND_BLOCK = 512
BAND_TILE = 256


def _rms_rows(x, gain_row):
    return x * lax.rsqrt(jnp.mean(x * x, axis=-1, keepdims=True) + EPS) * gain_row


def _dot(a, b):
    return jnp.dot(a, b, preferred_element_type=F32)


def _dot_nt(a, b):
    return lax.dot_general(a, b, (((1,), (1,)), ((), ())), preferred_element_type=F32)


def _head_norm(z, group_ones, gain_row):
    outs = []
    for c in range(z.shape[1] // NORM_CHUNK):
        zc = z[:, c * NORM_CHUNK:(c + 1) * NORM_CHUNK]
        ss = _dot((zc * zc).astype(BF16), group_ones)
        outs.append(zc * lax.rsqrt(ss * (1.0 / DA) + EPS))
    return jnp.concatenate(outs, axis=1) * gain_row


def _proj_kernel(x_ref, g1_ref, w_ref, ones_ref, gqa_ref, gka_ref, gqb_ref, gkb_ref, *rest, slab):
    (qa_ref, kaf_ref, kab_ref, vaf_ref, vab_ref,
     qb_ref, kbf_ref, kbb_ref, vbf_ref, vbb_ref) = rest[-10:]
    hb = _rms_rows(x_ref[...], g1_ref[...]).astype(BF16)
    ones = ones_ref[...]

    def store_heads(dst_ref, z):
        for d in range(dst_ref.shape[0]):
            if d == slab:
                dst_ref[d] = z.reshape(z.shape[0], H_A, DVA)
            else:
                dst_ref[d] = jnp.zeros(dst_ref.shape[1:], F32)

    qa = _head_norm(_dot(hb, w_ref[:, 0:W_A]), ones, gqa_ref[...])
    qa_ref[...] = qa.astype(BF16)
    ka = _head_norm(_dot(hb, w_ref[:, W_A:2 * W_A]), ones, gka_ref[...])
    store_heads(kaf_ref, ka)
    kab_ref[...] = ka.astype(BF16)
    va = _dot(hb, w_ref[:, 2 * W_A:3 * W_A])
    store_heads(vaf_ref, va)
    vab_ref[...] = va.astype(BF16)

    o = 3 * W_A
    qb = _head_norm(_dot(hb, w_ref[:, o:o + W_B]), ones, gqb_ref[...])
    qb_ref[...] = qb.astype(BF16)
    kb = _head_norm(_dot(hb, w_ref[:, o + W_B:o + 2 * W_B]), ones, gkb_ref[...])
    kbf_ref[...] = kb
    kbb_ref[...] = kb.astype(BF16)
    vb = _dot(hb, w_ref[:, o + 2 * W_B:o + 3 * W_B])
    vbf_ref[...] = vb
    vbb_ref[...] = vb.astype(BF16)


def _const_spec(shape):
    return pl.BlockSpec(shape, lambda *_: (0,) * len(shape), pipeline_mode=pl.Buffered(1))


def _projection(x2d, g1, w_bf, ones, gqa, gka, gqb, gkb, layer, depth, caches):
    n = x2d.shape[0]
    tm = min(PROJ_ROWS, n)
    assert n % tm == 0
    row = lambda width: pl.BlockSpec((tm, width), lambda i: (i, 0))
    if caches:
        slab, slab_in_block = pl.BlockSpec((1, tm, H_A, DVA), lambda i: (layer, i, 0, 0)), 0
    else:
        slab, slab_in_block = pl.BlockSpec((depth, tm, H_A, DVA), lambda i: (0, i, 0, 0)), layer
    out_shapes = []
    out_specs = []
    for width, dtypes in ((W_A, (BF16,)), (W_A, (F32, BF16)), (W_A, (F32, BF16)),
                          (W_B, (BF16,)), (W_B, (F32, BF16)), (W_B, (F32, BF16))):
        for dt in dtypes:
            if width == W_A and dt == F32:
                out_shapes.append(jax.ShapeDtypeStruct((depth, n, H_A, DVA), F32))
                out_specs.append(slab)
            else:
                out_shapes.append(jax.ShapeDtypeStruct((n, width), dt))
                out_specs.append(row(width))
    n_in = 8
    return pl.pallas_call(
        functools.partial(_proj_kernel, slab=slab_in_block),
        grid=(n // tm,),
        in_specs=[row(D_MODEL), _const_spec((1, D_MODEL)), _const_spec(w_bf.shape),
                  _const_spec(ones.shape), _const_spec((1, W_A)), _const_spec((1, W_A)),
                  _const_spec((1, W_B)), _const_spec((1, W_B))]
                 + [pl.BlockSpec(memory_space=pl.ANY)] * len(caches),
        out_specs=out_specs,
        out_shape=out_shapes,
        input_output_aliases={n_in: 1, n_in + 1: 3} if caches else {},
        compiler_params=pltpu.CompilerParams(dimension_semantics=("parallel",),
                                             vmem_limit_bytes=VMEM_LIMIT),
        name="projection",
    )(x2d, g1, w_bf, ones, gqa, gka, gqb, gkb, *caches)


def _split_maps(q):
    lane = lax.broadcasted_iota(jnp.int32, q.shape, 1)
    zero = jnp.zeros_like(q)
    return jnp.concatenate([jnp.where(lane < DA, q, zero), jnp.where(lane >= DA, q, zero)], axis=0)


def _lambda(lamv, lam_init):
    a = jnp.sum(lamv[0:1, :] * lamv[1:2, :], axis=1, keepdims=True)
    b = jnp.sum(lamv[2:3, :] * lamv[3:4, :], axis=1, keepdims=True)
    return jnp.exp(a) - jnp.exp(b) + lam_init


def _dattn_kernel(slopes_ref, lamv_ref, q_ref, k_ref, v_ref, o_ref,
                  qz_sc, kb_sc, db_sc, sa_sc, sb_sc, ma_sc, mb_sc, m_sc, acc_sc, *, qblk, kblk, lam_init):
    h = pl.program_id(1)
    i = pl.program_id(2)
    own = qblk // kblk
    sl2 = slopes_ref[h] * LOG2E

    @pl.when(i == 0)
    def _head_tables():
        qz_sc[LANES:2 * LANES, :] = (
            lax.broadcasted_iota(jnp.int32, (LANES, 2 * qblk), 0) < N_BIAS_COLS).astype(BF16)
        r = lax.broadcasted_iota(jnp.int32, (kblk, LANES), 0)
        c = lax.broadcasted_iota(jnp.int32, (kblk, LANES), 1)
        rest = sl2 * r.astype(F32)
        kb = jnp.zeros((kblk, LANES), F32)
        for col in range(N_BIAS_COLS):
            part = rest.astype(BF16).astype(F32)
            kb = jnp.where(c == col, part, kb)
            rest = rest - part
        kb_sc[...] = kb.astype(BF16)
        ki = lax.broadcasted_iota(jnp.int32, (qblk, qblk), 0)
        qi = lax.broadcasted_iota(jnp.int32, (qblk, qblk), 1)
        db_sc[...] = jnp.where((ki // CHUNK) <= (qi // CHUNK),
                               sl2 * (qi - jnp.abs(qi - ki) - ki).astype(F32), NEG_INF)

    qt = q_ref[0].T
    row = lax.broadcasted_iota(jnp.int32, qt.shape, 0)
    zero = jnp.zeros_like(qt)
    qz_sc[0:LANES, 0:qblk] = jnp.where(row < DA, qt, zero)
    qz_sc[0:LANES, qblk:2 * qblk] = jnp.where(row >= DA, qt, zero)

    m_sc[...] = jnp.full(m_sc.shape, -jnp.inf, F32)
    acc_sc[...] = jnp.zeros(acc_sc.shape, F32)
    ones_rows = jnp.ones((SUM_ROWS, kblk), BF16)

    first_own = own * i

    def rows(j):
        return pl.ds(pl.multiple_of(j * kblk, kblk), kblk)

    def scores(j, dst, max_dst):
        s = _dot(jnp.concatenate([k_ref[0, rows(j), :], kb_sc[...]], axis=1), qz_sc[...])
        dst[...] = s
        max_dst[...] = jnp.max(s, axis=0, keepdims=True)

    def softmax_pv(j, s, s_max):
        off = sl2 * (j * kblk - i * qblk).astype(F32)
        m_prev = m_sc[...] - off
        m_new = jnp.maximum(m_prev, s_max)
        alpha = jnp.exp2(m_prev - m_new)
        pb = jnp.exp2(s - m_new).astype(BF16)
        vt = jnp.concatenate([v_ref[0, rows(j), :].T, ones_rows], axis=0)
        pv = jnp.concatenate([_dot(vt, pb[:, 0:qblk]), _dot(vt, pb[:, qblk:2 * qblk])], axis=1)
        acc_sc[...] = alpha * acc_sc[...] + pv
        m_sc[...] = m_new + off

    def both_maps(ref, rws, lo):
        return jnp.concatenate([ref[rws, lo:qblk], ref[rws, qblk + lo:2 * qblk]], axis=1)

    def own_scores(d, dst):
        lhs = jnp.concatenate([k_ref[0, rows(first_own + d), :], kb_sc[...]], axis=1)
        rhs = both_maps(qz_sc, slice(None), d * kblk)
        dst[:, 0:rhs.shape[1]] = _dot(lhs, rhs)

    def own_softmax_pv(d, src):
        lo = d * kblk
        w = qblk - lo
        db = db_sc[lo:lo + kblk, lo:qblk]
        s = jnp.concatenate([src[:, 0:w] + db, src[:, w:2 * w] + db], axis=1)
        off = sl2 * jnp.float32(lo)
        m_prev = both_maps(m_sc, slice(None), lo) - off
        m_new = jnp.maximum(m_prev, jnp.max(s, axis=0, keepdims=True))
        alpha = jnp.exp2(m_prev - m_new)
        pb = jnp.exp2(s - m_new).astype(BF16)
        vt = jnp.concatenate([v_ref[0, rows(first_own + d), :].T, ones_rows], axis=0)
        for half, first_lane in enumerate((lo, qblk + lo)):
            lanes = slice(first_lane, first_lane + w)
            part = slice(half * w, (half + 1) * w)
            acc_sc[:, lanes] = alpha[:, part] * acc_sc[:, lanes] + _dot(vt, pb[:, part])
            m_sc[:, lanes] = m_new[:, part] + off

    scores(0, sa_sc, ma_sc)

    def pair(j0):
        scores(j0 + 1, sb_sc, mb_sc)
        softmax_pv(j0, sa_sc[...], ma_sc[...])
        scores(j0 + 2, sa_sc, ma_sc)
        softmax_pv(j0 + 1, sb_sc[...], mb_sc[...])

    def two_pairs(qq, carry):
        pair(4 * qq)
        pair(4 * qq + 2)
        return carry

    n_pairs = first_own // 2
    lax.fori_loop(0, n_pairs // 2, two_pairs, 0)

    @pl.when(n_pairs % 2 == 1)
    def _last_pair():
        pair(first_own - 2)

    bufs = (sa_sc, sb_sc)
    for d in range(own):
        if d + 1 < own:
            own_scores(d + 1, bufs[(d + 1) % 2])
        own_softmax_pv(d, bufs[d % 2])

    out = acc_sc[0:DVA, :] / acc_sc[DVA:DVA + 1, :]
    lam = _lambda(lamv_ref[...], lam_init)
    o_ref[0] = (out[:, 0:qblk] - lam * out[:, qblk:2 * qblk]).T


def _diff_attention_prompt(qa, kab, vab, slopes, lamv, lam_init):
    b, s, _ = qa.shape
    qblk = min(ATT_Q_BLOCK, s)
    kblk = qblk // 2
    assert s % qblk == 0 and kblk % CHUNK == 0
    head_rows = pl.BlockSpec((1, s, DVA), lambda bb, h, i: (bb, 0, h))
    block = pl.BlockSpec((1, qblk, DVA), lambda bb, h, i: (bb, i, h))
    return pl.pallas_call(
        functools.partial(_dattn_kernel, qblk=qblk, kblk=kblk, lam_init=lam_init),
        grid=(b, H_A, s // qblk),
        in_specs=[pl.BlockSpec(memory_space=pltpu.SMEM), pl.BlockSpec((4, DA), lambda bb, h, i: (0, 0)),
                  block, head_rows, head_rows],
        out_specs=block,
        out_shape=jax.ShapeDtypeStruct((b, s, W_A), F32),
        scratch_shapes=[pltpu.VMEM((2 * LANES, 2 * qblk), BF16), pltpu.VMEM((kblk, LANES), BF16),
                        pltpu.VMEM((qblk, qblk), F32),
                        pltpu.VMEM((kblk, 2 * qblk), F32), pltpu.VMEM((kblk, 2 * qblk), F32),
                        pltpu.VMEM((1, 2 * qblk), F32), pltpu.VMEM((1, 2 * qblk), F32),
                        pltpu.VMEM((1, 2 * qblk), F32), pltpu.VMEM((DVA + SUM_ROWS, 2 * qblk), F32)],
        compiler_params=pltpu.CompilerParams(
            dimension_semantics=("parallel", "parallel", "arbitrary"), vmem_limit_bytes=VMEM_LIMIT),
        name="diff_attention",
    )(slopes, lamv, qa, kab, vab)


def _dattn_sample_kernel(slopes_ref, lamv_ref, q_ref, kc_ref, kn_ref, vc_ref, vn_ref, o_ref,
                         kpad_sc, vpad_sc, *, t, p_len, lam_init):
    kpad_sc[...] = jnp.zeros(kpad_sc.shape, BF16)
    vpad_sc[...] = jnp.zeros(vpad_sc.shape, BF16)
    kpad_sc[0:t, :] = kn_ref[0]
    vpad_sc[0:t, :] = vn_ref[0]
    lam = _lambda(lamv_ref[...], lam_init)

    def distance(k0, nk, n_valid):
        qpos = lax.broadcasted_iota(jnp.int32, (t, nk), 0) + p_len
        kidx = lax.broadcasted_iota(jnp.int32, (t, nk), 1)
        kpos = kidx + k0
        visible = ((kpos // CHUNK) <= (qpos // CHUNK)) & (kidx < n_valid)
        return visible, jnp.abs(qpos - kpos).astype(F32)

    vis_c, dist_c = distance(0, p_len, p_len)
    vis_n, dist_n = distance(p_len, LANES, t)
    kc_all = kc_ref[0, 0].reshape(p_len, W_A).astype(BF16)
    vc_all = vc_ref[0, 0].reshape(p_len, W_A).astype(BF16)
    outs = []
    for hd in range(H_A):
        cols = slice(hd * DVA, (hd + 1) * DVA)
        slope = slopes_ref[hd] * LOG2E
        qz = _split_maps(q_ref[0, :, cols])

        def scores(k, vis, dist):
            s = _dot_nt(qz, k)
            bias_t = jnp.where(vis, -slope * dist, NEG_INF)
            return (s.reshape(2, t, -1) + bias_t[None]).reshape(2 * t, -1)

        s_c = scores(kc_all[:, cols], vis_c, dist_c)
        s_n = scores(kpad_sc[:, cols], vis_n, dist_n)
        m = jnp.maximum(jnp.max(s_c, axis=1, keepdims=True), jnp.max(s_n, axis=1, keepdims=True))
        p_c = jnp.exp2(s_c - m)
        p_n = jnp.exp2(s_n - m)
        l = jnp.sum(p_c, axis=1, keepdims=True) + jnp.sum(p_n, axis=1, keepdims=True)
        out = (_dot(p_c.astype(BF16), vc_all[:, cols]) + _dot(p_n.astype(BF16), vpad_sc[:, cols])) / l
        outs.append(out[0:t] - lam * out[t:2 * t])
    o_ref[0] = jnp.concatenate(outs, axis=1)


def _diff_attention_sample(qa, kab, vab, cache_k, cache_v, layer, slopes, lamv, lam_init):
    b, t, _ = qa.shape
    p_len = cache_k.shape[2]
    assert t <= LANES
    new = pl.BlockSpec((1, t, W_A), lambda bb: (bb, 0, 0))
    old = pl.BlockSpec((1, 1, p_len, H_A, DVA), lambda bb: (layer, bb, 0, 0, 0))
    return pl.pallas_call(
        functools.partial(_dattn_sample_kernel, t=t, p_len=p_len, lam_init=lam_init),
        grid=(b,),
        in_specs=[pl.BlockSpec(memory_space=pltpu.SMEM), pl.BlockSpec((4, DA), lambda bb: (0, 0)),
                  new, old, new, old, new],
        out_specs=new,
        out_shape=jax.ShapeDtypeStruct((b, t, W_A), F32),
        scratch_shapes=[pltpu.VMEM((LANES, W_A), BF16), pltpu.VMEM((LANES, W_A), BF16)],
        compiler_params=pltpu.CompilerParams(dimension_semantics=("parallel",),
                                             vmem_limit_bytes=VMEM_LIMIT),
        name="diff_attention_sample",
    )(slopes, lamv, qa, cache_k, kab, cache_v, vab)


def _pair_masks(q):
    lane = lax.broadcasted_iota(jnp.int32, q.shape, 1)
    zero = jnp.zeros_like(q)
    return lane, (jnp.where(lane < DB, q, zero), jnp.where(lane >= DB, q, zero))


def _battn_kernel(q_ref, kp_ref, kc_ref, vp_ref, vc_ref, bias_ref, o_ref, sa_sc, sb_sc, *, blk):
    i = pl.program_id(2)
    win = BAND_PAST + BAND_TILE
    qt = q_ref[0].T
    row = lax.broadcasted_iota(jnp.int32, qt.shape, 0)
    zero = jnp.zeros_like(qt)
    qz = (jnp.where(row < DB, qt, zero), jnp.where(row >= DB, qt, zero))
    ones_rows = jnp.ones((SUM_ROWS, win), BF16)
    key_row = lax.broadcasted_iota(jnp.int32, (win, BAND_TILE), 0)
    out_row = lax.broadcasted_iota(jnp.int32, (LANES, BAND_TILE), 0)

    def window(prev_ref, cur_ref, lo):
        return jnp.concatenate([prev_ref[0, lo:blk, :], cur_ref[0, 0:lo + BAND_TILE, :]], axis=0)

    def scores(item, dst):
        lo, e = item
        dst[...] = _dot(window(kp_ref, kc_ref, lo), qz[e][:, lo:lo + BAND_TILE])

    def softmax_pv(item, src):
        lo, e = item
        n_prev = blk - lo
        s = jnp.where((i > 0) | (key_row >= n_prev), src[...] + bias_ref[e], NEG_INF)
        pb = jnp.exp2(s - jnp.max(s, axis=0, keepdims=True)).astype(BF16)
        vt = jnp.concatenate([window(vp_ref, vc_ref, lo).T, ones_rows], axis=0)
        pv = _dot(vt, pb)
        return pv[0:LANES] / pv[LANES:LANES + 1]

    items = [(nt * BAND_TILE, e) for nt in range(blk // BAND_TILE) for e in range(2)]
    bufs = (sa_sc, sb_sc)
    outs = []
    scores(items[0], bufs[0])
    for n, item in enumerate(items):
        if n + 1 < len(items):
            scores(items[n + 1], bufs[(n + 1) % 2])
        outs.append(softmax_pv(item, bufs[n % 2]))
    tiles = [jnp.where(out_row < DB, outs[2 * nt], outs[2 * nt + 1]) for nt in range(blk // BAND_TILE)]
    o_ref[0] = jnp.concatenate(tiles, axis=1).T


def _band_attention_prompt(qb, kbb, vbb, bias_t):
    b, s, _ = qb.shape
    blk = BAND_BLOCK
    assert s % blk == 0 and blk == BAND_PAST and blk % BAND_TILE == 0
    cur = pl.BlockSpec((1, blk, LANES), lambda pr, bb, i: (bb, i, pr))
    prev = pl.BlockSpec((1, blk, LANES), lambda pr, bb, i: (bb, jnp.maximum(i - 1, 0), pr))
    return pl.pallas_call(
        functools.partial(_battn_kernel, blk=blk),
        grid=(H_B // 2, b, s // blk),
        in_specs=[cur, prev, cur, prev, cur,
                  pl.BlockSpec((2,) + bias_t.shape[1:], lambda pr, bb, i: (pr, 0, 0))],
        out_specs=cur,
        out_shape=jax.ShapeDtypeStruct((b, s, W_B), F32),
        scratch_shapes=[pltpu.VMEM(bias_t.shape[1:], F32), pltpu.VMEM(bias_t.shape[1:], F32)],
        compiler_params=pltpu.CompilerParams(
            dimension_semantics=("parallel", "parallel", "parallel"), vmem_limit_bytes=VMEM_LIMIT),
        name="band_attention",
    )(qb, kbb, kbb, vbb, vbb, bias_t)


def _battn_sample_kernel(q_ref, kc_ref, kn_ref, vc_ref, vn_ref, biasc_ref, biasn_ref, o_ref,
                         kpad_sc, vpad_sc, *, t):
    kpad_sc[...] = jnp.zeros(kpad_sc.shape, BF16)
    vpad_sc[...] = jnp.zeros(vpad_sc.shape, BF16)
    kpad_sc[0:t, :] = kn_ref[0]
    vpad_sc[0:t, :] = vn_ref[0]
    pieces = []
    for pr in range(H_B // 2):
        cols = slice(pr * LANES, (pr + 1) * LANES)
        lane, qz = _pair_masks(q_ref[0, :, cols])
        kc = kc_ref[0, 0, :, cols].astype(BF16)
        vc = vc_ref[0, 0, :, cols].astype(BF16)
        kn = kpad_sc[:, cols]
        vn = vpad_sc[:, cols]
        outs = []
        for e in range(2):
            hd = 2 * pr + e
            s_c = _dot_nt(qz[e], kc) + biasc_ref[hd]
            s_n = _dot_nt(qz[e], kn) + biasn_ref[hd]
            m = jnp.maximum(jnp.max(s_c, axis=1, keepdims=True), jnp.max(s_n, axis=1, keepdims=True))
            p_c = jnp.exp2(s_c - m)
            p_n = jnp.exp2(s_n - m)
            l = jnp.sum(p_c, axis=1, keepdims=True) + jnp.sum(p_n, axis=1, keepdims=True)
            outs.append((_dot(p_c.astype(BF16), vc) + _dot(p_n.astype(BF16), vn)) / l)
        pieces.append(jnp.where(lane < DB, outs[0], outs[1]))
    o_ref[0] = jnp.concatenate(pieces, axis=1)


def _band_attention_sample(qb, kbb, vbb, cache_k, cache_v, layer, bias_c, bias_n):
    b, t, _ = qb.shape
    keep = cache_k.shape[2]
    new = pl.BlockSpec((1, t, W_B), lambda bb: (bb, 0, 0))
    old = pl.BlockSpec((1, 1, keep, W_B), lambda bb: (layer, bb, 0, 0))
    return pl.pallas_call(
        functools.partial(_battn_sample_kernel, t=t),
        grid=(b,),
        in_specs=[new, old, new, old, new, _const_spec(bias_c.shape), _const_spec(bias_n.shape)],
        out_specs=new,
        out_shape=jax.ShapeDtypeStruct((b, t, W_B), F32),
        scratch_shapes=[pltpu.VMEM((LANES, W_B), BF16), pltpu.VMEM((LANES, W_B), BF16)],
        compiler_params=pltpu.CompilerParams(dimension_semantics=("parallel",),
                                             vmem_limit_bytes=VMEM_LIMIT),
        name="band_attention_sample",
    )(qb, cache_k, kbb, cache_v, vbb, bias_c, bias_n)


def _band_bias(rel_bias, q0, nq, k0, nk, n_valid):
    qpos = q0 + np.arange(nq)[:, None]
    kidx = np.arange(nk)[None, :]
    kpos = k0 + kidx
    dchunk = qpos // CHUNK - kpos // CHUNK
    visible = (kpos >= 0) & (dchunk >= 0) & (dchunk <= BAND_CHUNKS) & (kidx < n_valid)
    dist = np.arange(q0 - k0 - (nk - 1), q0 - k0 + nq)
    vals = rel_bias.astype(F32)[:, np.clip(dist, -REL_CLIP, REL_CLIP) + REL_CLIP]
    w_pad = jnp.pad(vals[:, ::-1], ((0, 0), (0, 1)))
    width = nq + nk - 1
    skew = jnp.tile(w_pad, (1, nq))[:, :nq * width].reshape(-1, nq, width)
    return jnp.where(jnp.asarray(visible)[None], skew[:, :, nq - 1:nq - 1 + nk], NEG_INF)


def _merge_ffn_kernel(x_ref, oa_ref, ob_ref, g1_ref, gs_ref, wa_ref, wb_ref, wg_ref, wo_ref,
                      g2_ref, w1_ref, w2_ref, y_ref):
    x = x_ref[...]
    hb = _rms_rows(x, g1_ref[...]).astype(BF16)
    oa = oa_ref[...]
    heads = [_rms_rows(oa[:, c * DVA:(c + 1) * DVA], 1.0) for c in range(H_A)]
    oan = (jnp.concatenate(heads, axis=1) * gs_ref[...]).astype(BF16)
    ya = _dot(oan, wa_ref[...])
    yb = _dot(ob_ref[...].astype(BF16), wb_ref[...])
    gates = jax.nn.sigmoid(_dot(hb, wg_ref[...]))
    mix = gates[:, 0:D_MODEL] * ya + gates[:, D_MODEL:2 * D_MODEL] * yb
    x1 = x + _dot(mix.astype(BF16), wo_ref[...])
    h2 = _rms_rows(x1, g2_ref[...]).astype(BF16)
    u = jnp.maximum(_dot(h2, w1_ref[...]), 0.0)
    y_ref[...] = x1 + _dot((u * u).astype(BF16), w2_ref[...])


def _merge_ffn(x2d, oa2d, ob2d, g1, gs, wa, wb, wg, wo, g2, w1, w2):
    n = x2d.shape[0]
    tm = min(PROJ_ROWS, n)
    assert n % tm == 0
    row = lambda width: pl.BlockSpec((tm, width), lambda i: (i, 0))
    consts = (g1, gs, wa, wb, wg, wo, g2, w1, w2)
    return pl.pallas_call(
        _merge_ffn_kernel,
        grid=(n // tm,),
        in_specs=[row(D_MODEL), row(W_A), row(W_B)] + [_const_spec(c.shape) for c in consts],
        out_specs=row(D_MODEL),
        out_shape=jax.ShapeDtypeStruct((n, D_MODEL), F32),
        compiler_params=pltpu.CompilerParams(dimension_semantics=("parallel",),
                                             vmem_limit_bytes=VMEM_LIMIT),
        name="merge_ffn",
    )(x2d, oa2d, ob2d, *consts)


def kernel(x_prompt, x_sample, cache_a_k, cache_a_v, cache_b_k, cache_b_v, norm1_g, w_in, qn_a_g, kn_a_g, qn_b_g, kn_b_g, lam_q1, lam_k1, lam_q2, lam_k2, subln_a_g, rel_bias_b, w_br_a, w_br_b, w_gate, w_out, norm2_g, w_ff1, w_ff2):
    bp, s, d = x_prompt.shape
    bs, t, _ = x_sample.shape
    depth = w_in.shape[0]
    p_len = cache_a_k.shape[2]
    b_keep = cache_b_k.shape[2]
    keep_p = min(BAND_PAST, s)
    cdt = cache_a_k.dtype

    slopes = jnp.exp2(-8.0 * jnp.arange(1, H_A + 1, dtype=F32) / H_A)
    group = np.arange(NORM_CHUNK) // DA
    ones = jnp.asarray(group[:, None] == group[None, :], BF16)
    row = lambda v, reps, scale=1.0: (jnp.tile(v.astype(F32), reps) * scale)[None, :]
    cache_bk = cache_b_k.reshape(depth, bs, b_keep, W_B)
    cache_bv = cache_b_v.reshape(depth, bs, b_keep, W_B)

    xp = x_prompt.astype(F32).reshape(bp * s, d)
    xs = x_sample.astype(F32).reshape(bs * t, d)
    a_caches_p = ()
    a_caches_s = ()
    b_outs = [[] for _ in range(4)]
    for l in range(depth):
        lam_init = 0.8 - 0.6 * math.exp(-0.3 * l)
        lamv = jnp.stack([lam_q1[l], lam_k1[l], lam_q2[l], lam_k2[l]]).astype(F32)
        g1 = row(norm1_g[l], 1)
        proj_consts = (g1, w_in[l].astype(BF16), ones,
                       row(qn_a_g[l], 2 * H_A, DA ** -0.5 * LOG2E), row(kn_a_g[l], 2 * H_A),
                       row(qn_b_g[l], H_B, DB ** -0.5 * LOG2E), row(kn_b_g[l], H_B))
        rel_l2 = rel_bias_b[l].astype(F32) * LOG2E
        tail = (g1, row(subln_a_g[l], H_A, 1.0 - lam_init), w_br_a[l].astype(BF16),
                w_br_b[l].astype(BF16), w_gate[l].astype(BF16), w_out[l].astype(BF16),
                row(norm2_g[l], 1), w_ff1[l].astype(BF16), w_ff2[l].astype(BF16))

        qa, kaf, kab, vaf, vab, qb, kbf, kbb, vbf, vbb = _projection(xp, *proj_consts, l, depth, a_caches_p)
        a_caches_p = (kaf, vaf)
        r3 = lambda a, n: a.reshape(n, -1, a.shape[-1])
        oa = _diff_attention_prompt(r3(qa, bp), r3(kab, bp), r3(vab, bp), slopes, lamv, lam_init)
        win = BAND_PAST + BAND_TILE
        bias_t = _band_bias(rel_l2, BAND_PAST, BAND_TILE, 0, win, win).transpose(0, 2, 1)
        ob = _band_attention_prompt(r3(qb, bp), r3(kbb, bp), r3(vbb, bp), bias_t)
        xp = _merge_ffn(xp, oa.reshape(bp * s, W_A), ob.reshape(bp * s, W_B), *tail)
        b_outs[0].append(r3(kbf, bp)[:, s - keep_p:].reshape(bp, keep_p, H_B, DB))
        b_outs[1].append(r3(vbf, bp)[:, s - keep_p:].reshape(bp, keep_p, H_B, DB))

        qa, kaf, kab, vaf, vab, qb, kbf, kbb, vbf, vbb = _projection(xs, *proj_consts, l, depth, a_caches_s)
        a_caches_s = (kaf, vaf)
        oa = _diff_attention_sample(r3(qa, bs), r3(kab, bs), r3(vab, bs), cache_a_k, cache_a_v, l,
                                    slopes, lamv, lam_init)
        bias_c = _band_bias(rel_l2, p_len, t, p_len - b_keep, b_keep, b_keep)
        bias_n = _band_bias(rel_l2, p_len, t, p_len, LANES, t)
        ob = _band_attention_sample(r3(qb, bs), r3(kbb, bs), r3(vbb, bs), cache_bk, cache_bv, l,
                                    bias_c, bias_n)
        xs = _merge_ffn(xs, oa.reshape(bs * t, W_A), ob.reshape(bs * t, W_B), *tail)
        b_outs[2].append(kbf.reshape(bs, t, H_B, DB))
        b_outs[3].append(vbf.reshape(bs, t, H_B, DB))

    bk_p, bv_p, bk_s, bv_s = [jnp.stack(o).astype(cdt) for o in b_outs]
    ak_p, av_p = [a.reshape(depth, bp, s, H_A, DVA).astype(cdt) for a in a_caches_p]
    ak_s, av_s = [a.reshape(depth, bs, t, H_A, DVA).astype(cdt) for a in a_caches_s]
    return (xp.reshape(bp, s, d).astype(x_prompt.dtype), xs.reshape(bs, t, d).astype(x_sample.dtype),
            ak_p, av_p, bk_p, bv_p, ak_s, av_s, bk_s, bv_s)
```

```python
import functools
import math

import numpy as np
import jax
import jax.numpy as jnp
from jax import lax
from jax.experimental import pallas as pl
from jax.experimental.pallas import tpu as pltpu

F32 = jnp.float32
BF16 = jnp.bfloat16

D_MODEL = 1024
CHUNK = 64
H_A = 8
DA = 64
DVA = 2 * DA
H_B = 8
DB = 64
BAND_CHUNKS = 8
BAND_PAST = BAND_CHUNKS * CHUNK
REL_CLIP = 128
D_FF = 4 * D_MODEL
EPS = 1e-6
NEG_INF = -1e30
W_A = H_A * 2 * DA
W_B = H_B * DB

LANES = 128
NORM_CHUNK = 256
VMEM_LIMIT = 56 * 1024 * 1024

PROJ_ROWS = 256
ATT_Q_BLOCK = 1024
LOG2E = 1.4426950408889634
N_BIAS_COLS = 3
SUM_ROWS = 16
BAND_BLOCK = 1024
BAND_TILE = 256


def _rms_rows(x, gain_row):
    return x * lax.rsqrt(jnp.mean(x * x, axis=-1, keepdims=True) + EPS) * gain_row


def _dot(a, b):
    return jnp.dot(a, b, preferred_element_type=F32)


def _dot_nt(a, b):
    return lax.dot_general(a, b, (((1,), (1,)), ((), ())), preferred_element_type=F32)


def _head_norm(z, group_ones, gain_row):
    outs = []
    for c in range(z.shape[1] // NORM_CHUNK):
        zc = z[:, c * NORM_CHUNK:(c + 1) * NORM_CHUNK]
        ss = _dot((zc * zc).astype(BF16), group_ones)
        outs.append(zc * lax.rsqrt(ss * (1.0 / DA) + EPS))
    return jnp.concatenate(outs, axis=1) * gain_row


def _proj_kernel(x_ref, g1_ref, w_ref, ones_ref, gqa_ref, gka_ref, gqb_ref, gkb_ref, *rest, slab):
    (qa_ref, kaf_ref, kab_ref, vaf_ref, vab_ref,
     qb_ref, kbf_ref, kbb_ref, vbf_ref, vbb_ref) = rest[-10:]
    hb = _rms_rows(x_ref[...], g1_ref[...]).astype(BF16)
    ones = ones_ref[...]

    def store_heads(dst_ref, z):
        for d in range(dst_ref.shape[0]):
            if d == slab:
                dst_ref[d] = z.reshape(z.shape[0], H_A, DVA)
            else:
                dst_ref[d] = jnp.zeros(dst_ref.shape[1:], F32)

    qa = _head_norm(_dot(hb, w_ref[:, 0:W_A]), ones, gqa_ref[...])
    qa_ref[...] = qa.astype(BF16)
    ka = _head_norm(_dot(hb, w_ref[:, W_A:2 * W_A]), ones, gka_ref[...])
    def store_head_major(dst_ref, z):
        for hd in range(H_A):
            dst_ref[hd] = z[:, hd * DVA:(hd + 1) * DVA].astype(BF16)

    store_heads(kaf_ref, ka)
    store_head_major(kab_ref, ka)
    va = _dot(hb, w_ref[:, 2 * W_A:3 * W_A])
    store_heads(vaf_ref, va)
    store_head_major(vab_ref, va)

    o = 3 * W_A
    qb = _head_norm(_dot(hb, w_ref[:, o:o + W_B]), ones, gqb_ref[...])
    qb_ref[...] = qb.astype(BF16)
    kb = _head_norm(_dot(hb, w_ref[:, o + W_B:o + 2 * W_B]), ones, gkb_ref[...])
    kbf_ref[...] = kb
    kbb_ref[...] = kb.astype(BF16)
    vb = _dot(hb, w_ref[:, o + 2 * W_B:o + 3 * W_B])
    vbf_ref[...] = vb
    vbb_ref[...] = vb.astype(BF16)


def _const_spec(shape):
    return pl.BlockSpec(shape, lambda *_: (0,) * len(shape), pipeline_mode=pl.Buffered(1))


def _projection(x2d, g1, w_bf, ones, gqa, gka, gqb, gkb, layer, depth, caches):
    n = x2d.shape[0]
    tm = min(PROJ_ROWS, n)
    assert n % tm == 0
    row = lambda width: pl.BlockSpec((tm, width), lambda i: (i, 0))
    if caches:
        slab, slab_in_block = pl.BlockSpec((1, tm, H_A, DVA), lambda i: (layer, i, 0, 0)), 0
    else:
        slab, slab_in_block = pl.BlockSpec((depth, tm, H_A, DVA), lambda i: (0, i, 0, 0)), layer
    out_shapes = []
    out_specs = []
    for width, dtypes in ((W_A, (BF16,)), (W_A, (F32, BF16)), (W_A, (F32, BF16)),
                          (W_B, (BF16,)), (W_B, (F32, BF16)), (W_B, (F32, BF16))):
        for dt in dtypes:
            if width == W_A and dt == F32:
                out_shapes.append(jax.ShapeDtypeStruct((depth, n, H_A, DVA), F32))
                out_specs.append(slab)
            elif width == W_A and len(dtypes) == 2:
                out_shapes.append(jax.ShapeDtypeStruct((H_A, n, DVA), BF16))
                out_specs.append(pl.BlockSpec((H_A, tm, DVA), lambda i: (0, i, 0)))
            else:
                out_shapes.append(jax.ShapeDtypeStruct((n, width), dt))
                out_specs.append(row(width))
    n_in = 8
    return pl.pallas_call(
        functools.partial(_proj_kernel, slab=slab_in_block),
        grid=(n // tm,),
        in_specs=[row(D_MODEL), _const_spec((1, D_MODEL)), _const_spec(w_bf.shape),
                  _const_spec(ones.shape), _const_spec((1, W_A)), _const_spec((1, W_A)),
                  _const_spec((1, W_B)), _const_spec((1, W_B))]
                 + [pl.BlockSpec(memory_space=pl.ANY)] * len(caches),
        out_specs=out_specs,
        out_shape=out_shapes,
        input_output_aliases={n_in: 1, n_in + 1: 3} if caches else {},
        compiler_params=pltpu.CompilerParams(dimension_semantics=("parallel",),
                                             vmem_limit_bytes=VMEM_LIMIT),
        name="projection",
    )(x2d, g1, w_bf, ones, gqa, gka, gqb, gkb, *caches)


def _split_maps(q):
    lane = lax.broadcasted_iota(jnp.int32, q.shape, 1)
    zero = jnp.zeros_like(q)
    return jnp.concatenate([jnp.where(lane < DA, q, zero), jnp.where(lane >= DA, q, zero)], axis=0)


def _lambda(lamv, lam_init):
    a = jnp.sum(lamv[0:1, :] * lamv[1:2, :], axis=1, keepdims=True)
    b = jnp.sum(lamv[2:3, :] * lamv[3:4, :], axis=1, keepdims=True)
    return jnp.exp(a) - jnp.exp(b) + lam_init


def _dattn_kernel(slopes_ref, lamv_ref, q_ref, k_ref, v_ref, o_ref,
                  qz_sc, kb_sc, db_sc, sa_sc, sb_sc, ma_sc, mb_sc, m_sc, acc_sc, *, qblk, kblk, lam_init):
    h = pl.program_id(1)
    i = pl.program_id(2)
    own = qblk // kblk
    sl2 = slopes_ref[h] * LOG2E

    @pl.when(i == 0)
    def _head_tables():
        qz_sc[LANES:2 * LANES, :] = (
            lax.broadcasted_iota(jnp.int32, (LANES, 2 * qblk), 0) < N_BIAS_COLS).astype(BF16)
        r = lax.broadcasted_iota(jnp.int32, (kblk, LANES), 0)
        c = lax.broadcasted_iota(jnp.int32, (kblk, LANES), 1)
        rest = sl2 * r.astype(F32)
        kb = jnp.zeros((kblk, LANES), F32)
        for col in range(N_BIAS_COLS):
            part = rest.astype(BF16).astype(F32)
            kb = jnp.where(c == col, part, kb)
            rest = rest - part
        kb_sc[...] = kb.astype(BF16)
        ki = lax.broadcasted_iota(jnp.int32, (qblk, qblk), 0)
        qi = lax.broadcasted_iota(jnp.int32, (qblk, qblk), 1)
        db_sc[...] = jnp.where((ki // CHUNK) <= (qi // CHUNK),
                               sl2 * (qi - jnp.abs(qi - ki) - ki).astype(F32), NEG_INF)

    qt = q_ref[0].T
    row = lax.broadcasted_iota(jnp.int32, qt.shape, 0)
    zero = jnp.zeros_like(qt)
    qz_sc[0:LANES, 0:qblk] = jnp.where(row < DA, qt, zero)
    qz_sc[0:LANES, qblk:2 * qblk] = jnp.where(row >= DA, qt, zero)

    m_sc[...] = jnp.full(m_sc.shape, -jnp.inf, F32)
    acc_sc[...] = jnp.zeros(acc_sc.shape, F32)
    ones_rows = jnp.ones((SUM_ROWS, kblk), BF16)

    first_own = own * i

    def rows(j):
        return pl.ds(pl.multiple_of(j * kblk, kblk), kblk)

    def scores(j, dst, max_dst):
        s = _dot(jnp.concatenate([k_ref[0, 0, rows(j), :], kb_sc[...]], axis=1), qz_sc[...])
        dst[...] = s
        max_dst[...] = jnp.max(s, axis=0, keepdims=True)

    def softmax_pv(j, s, s_max):
        off = sl2 * (j * kblk - i * qblk).astype(F32)
        m_prev = m_sc[...] - off
        m_new = jnp.maximum(m_prev, s_max)
        alpha = jnp.exp2(m_prev - m_new)
        pb = jnp.exp2(s - m_new).astype(BF16)
        vt = jnp.concatenate([v_ref[0, 0, rows(j), :].T, ones_rows], axis=0)
        pv = jnp.concatenate([_dot(vt, pb[:, 0:qblk]), _dot(vt, pb[:, qblk:2 * qblk])], axis=1)
        acc_sc[...] = alpha * acc_sc[...] + pv
        m_sc[...] = m_new + off

    def both_maps(ref, rws, lo):
        return jnp.concatenate([ref[rws, lo:qblk], ref[rws, qblk + lo:2 * qblk]], axis=1)

    def own_scores(d, dst):
        lhs = jnp.concatenate([k_ref[0, 0, rows(first_own + d), :], kb_sc[...]], axis=1)
        rhs = both_maps(qz_sc, slice(None), d * kblk)
        dst[:, 0:rhs.shape[1]] = _dot(lhs, rhs)

    def own_softmax_pv(d, src):
        lo = d * kblk
        w = qblk - lo
        db = db_sc[lo:lo + kblk, lo:qblk]
        s = jnp.concatenate([src[:, 0:w] + db, src[:, w:2 * w] + db], axis=1)
        off = sl2 * jnp.float32(lo)
        m_prev = both_maps(m_sc, slice(None), lo) - off
        m_new = jnp.maximum(m_prev, jnp.max(s, axis=0, keepdims=True))
        alpha = jnp.exp2(m_prev - m_new)
        pb = jnp.exp2(s - m_new).astype(BF16)
        vt = jnp.concatenate([v_ref[0, 0, rows(first_own + d), :].T, ones_rows], axis=0)
        for half, first_lane in enumerate((lo, qblk + lo)):
            lanes = slice(first_lane, first_lane + w)
            part = slice(half * w, (half + 1) * w)
            acc_sc[:, lanes] = alpha[:, part] * acc_sc[:, lanes] + _dot(vt, pb[:, part])
            m_sc[:, lanes] = m_new[:, part] + off

    scores(0, sa_sc, ma_sc)

    def pair(j0):
        scores(j0 + 1, sb_sc, mb_sc)
        softmax_pv(j0, sa_sc[...], ma_sc[...])
        scores(j0 + 2, sa_sc, ma_sc)
        softmax_pv(j0 + 1, sb_sc[...], mb_sc[...])

    def two_pairs(qq, carry):
        pair(4 * qq)
        pair(4 * qq + 2)
        return carry

    n_pairs = first_own // 2
    lax.fori_loop(0, n_pairs // 2, two_pairs, 0)

    @pl.when(n_pairs % 2 == 1)
    def _last_pair():
        pair(first_own - 2)

    bufs = (sa_sc, sb_sc)
    for d in range(own):
        if d + 1 < own:
            own_scores(d + 1, bufs[(d + 1) % 2])
        own_softmax_pv(d, bufs[d % 2])

    inv = 1.0 / acc_sc[DVA:DVA + 1, :]
    lam = _lambda(lamv_ref[...], lam_init)
    o_ref[0] = (acc_sc[0:DVA, 0:qblk] * inv[:, 0:qblk]
                - acc_sc[0:DVA, qblk:2 * qblk] * (lam * inv[:, qblk:2 * qblk])).T


def _diff_attention_prompt(qa, kab, vab, slopes, lamv, lam_init):
    b, s, _ = qa.shape
    qblk = min(ATT_Q_BLOCK, s)
    kblk = qblk // 2
    assert s % qblk == 0 and kblk % CHUNK == 0
    head_rows = pl.BlockSpec((1, 1, s, DVA), lambda bb, h, i: (h, bb, 0, 0))
    block = pl.BlockSpec((1, qblk, DVA), lambda bb, h, i: (bb, i, h))
    return pl.pallas_call(
        functools.partial(_dattn_kernel, qblk=qblk, kblk=kblk, lam_init=lam_init),
        grid=(b, H_A, s // qblk),
        in_specs=[pl.BlockSpec(memory_space=pltpu.SMEM), pl.BlockSpec((4, DA), lambda bb, h, i: (0, 0)),
                  block, head_rows, head_rows],
        out_specs=block,
        out_shape=jax.ShapeDtypeStruct((b, s, W_A), F32),
        scratch_shapes=[pltpu.VMEM((2 * LANES, 2 * qblk), BF16), pltpu.VMEM((kblk, LANES), BF16),
                        pltpu.VMEM((qblk, qblk), F32),
                        pltpu.VMEM((kblk, 2 * qblk), F32), pltpu.VMEM((kblk, 2 * qblk), F32),
                        pltpu.VMEM((1, 2 * qblk), F32), pltpu.VMEM((1, 2 * qblk), F32),
                        pltpu.VMEM((1, 2 * qblk), F32), pltpu.VMEM((DVA + SUM_ROWS, 2 * qblk), F32)],
        compiler_params=pltpu.CompilerParams(
            dimension_semantics=("parallel", "parallel", "arbitrary"), vmem_limit_bytes=VMEM_LIMIT),
        name="diff_attention",
    )(slopes, lamv, qa, kab, vab)


def _dattn_sample_kernel(slopes_ref, lamv_ref, q_ref, kc_ref, kn_ref, vc_ref, vn_ref, o_ref,
                         kpad_sc, vpad_sc, *, t, p_len, lam_init):
    kpad_sc[...] = jnp.zeros(kpad_sc.shape, BF16)
    vpad_sc[...] = jnp.zeros(vpad_sc.shape, BF16)
    for hd in range(H_A):
        kpad_sc[0:t, hd * DVA:(hd + 1) * DVA] = kn_ref[hd, 0]
        vpad_sc[0:t, hd * DVA:(hd + 1) * DVA] = vn_ref[hd, 0]
    lam = _lambda(lamv_ref[...], lam_init)

    def distance(k0, nk, n_valid):
        qpos = lax.broadcasted_iota(jnp.int32, (t, nk), 0) + p_len
        kidx = lax.broadcasted_iota(jnp.int32, (t, nk), 1)
        kpos = kidx + k0
        visible = ((kpos // CHUNK) <= (qpos // CHUNK)) & (kidx < n_valid)
        return visible, jnp.abs(qpos - kpos).astype(F32)

    vis_c, dist_c = distance(0, p_len, p_len)
    vis_n, dist_n = distance(p_len, LANES, t)
    kc_all = kc_ref[0, 0].reshape(p_len, W_A).astype(BF16)
    vc_all = vc_ref[0, 0].reshape(p_len, W_A).astype(BF16)
    outs = []
    for hd in range(H_A):
        cols = slice(hd * DVA, (hd + 1) * DVA)
        slope = slopes_ref[hd] * LOG2E
        qz = _split_maps(q_ref[0, :, cols])

        def scores(k, vis, dist):
            s = _dot_nt(qz, k)
            bias_t = jnp.where(vis, -slope * dist, NEG_INF)
            return (s.reshape(2, t, -1) + bias_t[None]).reshape(2 * t, -1)

        s_c = scores(kc_all[:, cols], vis_c, dist_c)
        s_n = scores(kpad_sc[:, cols], vis_n, dist_n)
        m = jnp.maximum(jnp.max(s_c, axis=1, keepdims=True), jnp.max(s_n, axis=1, keepdims=True))
        p_c = jnp.exp2(s_c - m)
        p_n = jnp.exp2(s_n - m)
        l = jnp.sum(p_c, axis=1, keepdims=True) + jnp.sum(p_n, axis=1, keepdims=True)
        out = (_dot(p_c.astype(BF16), vc_all[:, cols]) + _dot(p_n.astype(BF16), vpad_sc[:, cols])) / l
        outs.append(out[0:t] - lam * out[t:2 * t])
    o_ref[0] = jnp.concatenate(outs, axis=1)


def _diff_attention_sample(qa, kab, vab, cache_k, cache_v, layer, slopes, lamv, lam_init):
    b, t, _ = qa.shape
    p_len = cache_k.shape[2]
    assert t <= LANES
    new = pl.BlockSpec((1, t, W_A), lambda bb: (bb, 0, 0))
    new_kv = pl.BlockSpec((H_A, 1, t, DVA), lambda bb: (0, bb, 0, 0))
    old = pl.BlockSpec((1, 1, p_len, H_A, DVA), lambda bb: (layer, bb, 0, 0, 0))
    return pl.pallas_call(
        functools.partial(_dattn_sample_kernel, t=t, p_len=p_len, lam_init=lam_init),
        grid=(b,),
        in_specs=[pl.BlockSpec(memory_space=pltpu.SMEM), pl.BlockSpec((4, DA), lambda bb: (0, 0)),
                  new, old, new_kv, old, new_kv],
        out_specs=new,
        out_shape=jax.ShapeDtypeStruct((b, t, W_A), F32),
        scratch_shapes=[pltpu.VMEM((LANES, W_A), BF16), pltpu.VMEM((LANES, W_A), BF16)],
        compiler_params=pltpu.CompilerParams(dimension_semantics=("parallel",),
                                             vmem_limit_bytes=VMEM_LIMIT),
        name="diff_attention_sample",
    )(slopes, lamv, qa, cache_k, kab, cache_v, vab)


def _pair_masks(q):
    lane = lax.broadcasted_iota(jnp.int32, q.shape, 1)
    zero = jnp.zeros_like(q)
    return lane, (jnp.where(lane < DB, q, zero), jnp.where(lane >= DB, q, zero))


def _battn_kernel(q_ref, kp_ref, kc_ref, vp_ref, vc_ref, bias_ref, o_ref, sa_sc, sb_sc, *, blk):
    i = pl.program_id(2)
    win = BAND_PAST + BAND_TILE
    qt = q_ref[0].T
    row = lax.broadcasted_iota(jnp.int32, qt.shape, 0)
    zero = jnp.zeros_like(qt)
    qz = (jnp.where(row < DB, qt, zero), jnp.where(row >= DB, qt, zero))
    ones_rows = jnp.ones((SUM_ROWS, win), BF16)
    key_row = lax.broadcasted_iota(jnp.int32, (win, BAND_TILE), 0)
    out_row = lax.broadcasted_iota(jnp.int32, (LANES, BAND_TILE), 0)

    def window(prev_ref, cur_ref, lo):
        cur = cur_ref[0, max(lo - BAND_PAST, 0):lo + BAND_TILE, :]
        return jnp.concatenate([prev_ref[0, lo:BAND_PAST, :], cur], axis=0) if lo < BAND_PAST else cur

    def scores(item, dst):
        lo, e = item
        dst[...] = _dot(window(kp_ref, kc_ref, lo), qz[e][:, lo:lo + BAND_TILE])

    def softmax_pv(item, src):
        lo, e = item
        s = src[...] + bias_ref[e]
        n_prev = BAND_PAST - lo
        if n_prev > 0:
            s = jnp.where((i > 0) | (key_row >= n_prev), s, NEG_INF)
        pb = jnp.exp2(s - jnp.max(s, axis=0, keepdims=True)).astype(BF16)
        vt = jnp.concatenate([window(vp_ref, vc_ref, lo).T, ones_rows], axis=0)
        pv = _dot(vt, pb)
        return pv[0:LANES] / pv[LANES:LANES + 1]

    items = [(nt * BAND_TILE, e) for nt in range(blk // BAND_TILE) for e in range(2)]
    bufs = (sa_sc, sb_sc)
    outs = []
    scores(items[0], bufs[0])
    for n, item in enumerate(items):
        if n + 1 < len(items):
            scores(items[n + 1], bufs[(n + 1) % 2])
        outs.append(softmax_pv(item, bufs[n % 2]))
    tiles = [jnp.where(out_row < DB, outs[2 * nt], outs[2 * nt + 1]) for nt in range(blk // BAND_TILE)]
    o_ref[0] = jnp.concatenate(tiles, axis=1).T


def _band_attention_prompt(qb, kbb, vbb, bias_t):
    b, s, _ = qb.shape
    blk = min(BAND_BLOCK, s)
    ratio = blk // BAND_PAST
    assert s % blk == 0 and blk % BAND_PAST == 0 and BAND_PAST % BAND_TILE == 0
    cur = pl.BlockSpec((1, blk, LANES), lambda pr, bb, i: (bb, i, pr))
    prev = pl.BlockSpec((1, BAND_PAST, LANES),
                        lambda pr, bb, i: (bb, jnp.maximum(i * ratio - 1, 0), pr))
    return pl.pallas_call(
        functools.partial(_battn_kernel, blk=blk),
        grid=(H_B // 2, b, s // blk),
        in_specs=[cur, prev, cur, prev, cur,
                  pl.BlockSpec((2,) + bias_t.shape[1:], lambda pr, bb, i: (pr, 0, 0))],
        out_specs=cur,
        out_shape=jax.ShapeDtypeStruct((b, s, W_B), F32),
        scratch_shapes=[pltpu.VMEM(bias_t.shape[1:], F32), pltpu.VMEM(bias_t.shape[1:], F32)],
        compiler_params=pltpu.CompilerParams(
            dimension_semantics=("parallel", "parallel", "parallel"), vmem_limit_bytes=VMEM_LIMIT),
        name="band_attention",
    )(qb, kbb, kbb, vbb, vbb, bias_t)


def _battn_sample_kernel(q_ref, kc_ref, kn_ref, vc_ref, vn_ref, biasc_ref, biasn_ref, o_ref,
                         kpad_sc, vpad_sc, *, t):
    kpad_sc[...] = jnp.zeros(kpad_sc.shape, BF16)
    vpad_sc[...] = jnp.zeros(vpad_sc.shape, BF16)
    kpad_sc[0:t, :] = kn_ref[0]
    vpad_sc[0:t, :] = vn_ref[0]
    pieces = []
    for pr in range(H_B // 2):
        cols = slice(pr * LANES, (pr + 1) * LANES)
        lane, qz = _pair_masks(q_ref[0, :, cols])
        kc = kc_ref[0, 0, :, cols].astype(BF16)
        vc = vc_ref[0, 0, :, cols].astype(BF16)
        kn = kpad_sc[:, cols]
        vn = vpad_sc[:, cols]
        outs = []
        for e in range(2):
            hd = 2 * pr + e
            s_c = _dot_nt(qz[e], kc) + biasc_ref[hd]
            s_n = _dot_nt(qz[e], kn) + biasn_ref[hd]
            m = jnp.maximum(jnp.max(s_c, axis=1, keepdims=True), jnp.max(s_n, axis=1, keepdims=True))
            p_c = jnp.exp2(s_c - m)
            p_n = jnp.exp2(s_n - m)
            l = jnp.sum(p_c, axis=1, keepdims=True) + jnp.sum(p_n, axis=1, keepdims=True)
            outs.append((_dot(p_c.astype(BF16), vc) + _dot(p_n.astype(BF16), vn)) / l)
        pieces.append(jnp.where(lane < DB, outs[0], outs[1]))
    o_ref[0] = jnp.concatenate(pieces, axis=1)


def _band_attention_sample(qb, kbb, vbb, cache_k, cache_v, layer, bias_c, bias_n):
    b, t, _ = qb.shape
    keep = cache_k.shape[2]
    new = pl.BlockSpec((1, t, W_B), lambda bb: (bb, 0, 0))
    old = pl.BlockSpec((1, 1, keep, W_B), lambda bb: (layer, bb, 0, 0))
    return pl.pallas_call(
        functools.partial(_battn_sample_kernel, t=t),
        grid=(b,),
        in_specs=[new, old, new, old, new, _const_spec(bias_c.shape), _const_spec(bias_n.shape)],
        out_specs=new,
        out_shape=jax.ShapeDtypeStruct((b, t, W_B), F32),
        scratch_shapes=[pltpu.VMEM((LANES, W_B), BF16), pltpu.VMEM((LANES, W_B), BF16)],
        compiler_params=pltpu.CompilerParams(dimension_semantics=("parallel",),
                                             vmem_limit_bytes=VMEM_LIMIT),
        name="band_attention_sample",
    )(qb, cache_k, kbb, cache_v, vbb, bias_c, bias_n)


def _band_bias(rel_bias, q0, nq, k0, nk, n_valid):
    qpos = q0 + np.arange(nq)[:, None]
    kidx = np.arange(nk)[None, :]
    kpos = k0 + kidx
    dchunk = qpos // CHUNK - kpos // CHUNK
    visible = (kpos >= 0) & (dchunk >= 0) & (dchunk <= BAND_CHUNKS) & (kidx < n_valid)
    dist = np.arange(q0 - k0 - (nk - 1), q0 - k0 + nq)
    vals = rel_bias.astype(F32)[:, np.clip(dist, -REL_CLIP, REL_CLIP) + REL_CLIP]
    w_pad = jnp.pad(vals[:, ::-1], ((0, 0), (0, 1)))
    width = nq + nk - 1
    skew = jnp.tile(w_pad, (1, nq))[:, :nq * width].reshape(-1, nq, width)
    return jnp.where(jnp.asarray(visible)[None], skew[:, :, nq - 1:nq - 1 + nk], NEG_INF)


def _merge_ffn_kernel(x_ref, oa_ref, ob_ref, g1_ref, gs_ref, wa_ref, wb_ref, wg_ref, wo_ref,
                      g2_ref, w1_ref, w2_ref, y_ref):
    x = x_ref[...]
    hb = _rms_rows(x, g1_ref[...]).astype(BF16)
    oa = oa_ref[...]
    heads = [_rms_rows(oa[:, c * DVA:(c + 1) * DVA], 1.0) for c in range(H_A)]
    oan = (jnp.concatenate(heads, axis=1) * gs_ref[...]).astype(BF16)
    ya = _dot(oan, wa_ref[...])
    yb = _dot(ob_ref[...].astype(BF16), wb_ref[...])
    gates = jax.nn.sigmoid(_dot(hb, wg_ref[...]))
    mix = gates[:, 0:D_MODEL] * ya + gates[:, D_MODEL:2 * D_MODEL] * yb
    x1 = x + _dot(mix.astype(BF16), wo_ref[...])
    h2 = _rms_rows(x1, g2_ref[...]).astype(BF16)
    u = jnp.maximum(_dot(h2, w1_ref[...]), 0.0)
    y_ref[...] = x1 + _dot((u * u).astype(BF16), w2_ref[...])


def _merge_ffn(x2d, oa2d, ob2d, g1, gs, wa, wb, wg, wo, g2, w1, w2):
    n = x2d.shape[0]
    tm = min(PROJ_ROWS, n)
    assert n % tm == 0
    row = lambda width: pl.BlockSpec((tm, width), lambda i: (i, 0))
    consts = (g1, gs, wa, wb, wg, wo, g2, w1, w2)
    return pl.pallas_call(
        _merge_ffn_kernel,
        grid=(n // tm,),
        in_specs=[row(D_MODEL), row(W_A), row(W_B)] + [_const_spec(c.shape) for c in consts],
        out_specs=row(D_MODEL),
        out_shape=jax.ShapeDtypeStruct((n, D_MODEL), F32),
        compiler_params=pltpu.CompilerParams(dimension_semantics=("parallel",),
                                             vmem_limit_bytes=VMEM_LIMIT),
        name="merge_ffn",
    )(x2d, oa2d, ob2d, *consts)


def kernel(x_prompt, x_sample, cache_a_k, cache_a_v, cache_b_k, cache_b_v, norm1_g, w_in, qn_a_g, kn_a_g, qn_b_g, kn_b_g, lam_q1, lam_k1, lam_q2, lam_k2, subln_a_g, rel_bias_b, w_br_a, w_br_b, w_gate, w_out, norm2_g, w_ff1, w_ff2):
    bp, s, d = x_prompt.shape
    bs, t, _ = x_sample.shape
    depth = w_in.shape[0]
    p_len = cache_a_k.shape[2]
    b_keep = cache_b_k.shape[2]
    keep_p = min(BAND_PAST, s)
    cdt = cache_a_k.dtype

    slopes = jnp.exp2(-8.0 * jnp.arange(1, H_A + 1, dtype=F32) / H_A)
    group = np.arange(NORM_CHUNK) // DA
    ones = jnp.asarray(group[:, None] == group[None, :], BF16)
    row = lambda v, reps, scale=1.0: (jnp.tile(v.astype(F32), reps) * scale)[None, :]
    cache_bk = cache_b_k.reshape(depth, bs, b_keep, W_B)
    cache_bv = cache_b_v.reshape(depth, bs, b_keep, W_B)

    xp = x_prompt.astype(F32).reshape(bp * s, d)
    xs = x_sample.astype(F32).reshape(bs * t, d)
    a_caches_p = ()
    a_caches_s = ()
    b_outs = [[] for _ in range(4)]
    for l in range(depth):
        lam_init = 0.8 - 0.6 * math.exp(-0.3 * l)
        lamv = jnp.stack([lam_q1[l], lam_k1[l], lam_q2[l], lam_k2[l]]).astype(F32)
        g1 = row(norm1_g[l], 1)
        proj_consts = (g1, w_in[l].astype(BF16), ones,
                       row(qn_a_g[l], 2 * H_A, DA ** -0.5 * LOG2E), row(kn_a_g[l], 2 * H_A),
                       row(qn_b_g[l], H_B, DB ** -0.5 * LOG2E), row(kn_b_g[l], H_B))
        rel_l2 = rel_bias_b[l].astype(F32) * LOG2E
        tail = (g1, row(subln_a_g[l], H_A, 1.0 - lam_init), w_br_a[l].astype(BF16),
                w_br_b[l].astype(BF16), w_gate[l].astype(BF16), w_out[l].astype(BF16),
                row(norm2_g[l], 1), w_ff1[l].astype(BF16), w_ff2[l].astype(BF16))

        qa, kaf, kab, vaf, vab, qb, kbf, kbb, vbf, vbb = _projection(xp, *proj_consts, l, depth, a_caches_p)
        a_caches_p = (kaf, vaf)
        r3 = lambda a, n: a.reshape(n, -1, a.shape[-1])
        by_head = lambda a, n: a.reshape(H_A, n, -1, DVA)
        oa = _diff_attention_prompt(r3(qa, bp), by_head(kab, bp), by_head(vab, bp), slopes, lamv, lam_init)
        win = BAND_PAST + BAND_TILE
        bias_t = _band_bias(rel_l2, BAND_PAST, BAND_TILE, 0, win, win).transpose(0, 2, 1)
        ob = _band_attention_prompt(r3(qb, bp), r3(kbb, bp), r3(vbb, bp), bias_t)
        xp = _merge_ffn(xp, oa.reshape(bp * s, W_A), ob.reshape(bp * s, W_B), *tail)
        b_outs[0].append(r3(kbf, bp)[:, s - keep_p:].reshape(bp, keep_p, H_B, DB))
        b_outs[1].append(r3(vbf, bp)[:, s - keep_p:].reshape(bp, keep_p, H_B, DB))

        qa, kaf, kab, vaf, vab, qb, kbf, kbb, vbf, vbb = _projection(xs, *proj_consts, l, depth, a_caches_s)
        a_caches_s = (kaf, vaf)
        oa = _diff_attention_sample(r3(qa, bs), by_head(kab, bs), by_head(vab, bs), cache_a_k, cache_a_v, l,
                                    slopes, lamv, lam_init)
        bias_c = _band_bias(rel_l2, p_len, t, p_len - b_keep, b_keep, b_keep)
        bias_n = _band_bias(rel_l2, p_len, t, p_len, LANES, t)
        ob = _band_attention_sample(r3(qb, bs), r3(kbb, bs), r3(vbb, bs), cache_bk, cache_bv, l,
                                    bias_c, bias_n)
        xs = _merge_ffn(xs, oa.reshape(bs * t, W_A), ob.reshape(bs * t, W_B), *tail)
        b_outs[2].append(kbf.reshape(bs, t, H_B, DB))
        b_outs[3].append(vbf.reshape(bs, t, H_B, DB))

    bk_p, bv_p, bk_s, bv_s = [jnp.stack(o).astype(cdt) for o in b_outs]
    ak_p, av_p = [a.reshape(depth, bp, s, H_A, DVA).astype(cdt) for a in a_caches_p]
    ak_s, av_s = [a.reshape(depth, bs, t, H_A, DVA).astype(cdt) for a in a_caches_s]
    return (xp.reshape(bp, s, d).astype(x_prompt.dtype), xs.reshape(bs, t, d).astype(x_sample.dtype),
            ak_p, av_p, bk_p, bv_p, ak_s, av_s, bk_s, bv_s)
```

```python
import functools
import math

import numpy as np
import jax
import jax.numpy as jnp
from jax import lax
from jax.experimental import pallas as pl
from jax.experimental.pallas import tpu as pltpu

F32 = jnp.float32
BF16 = jnp.bfloat16

D_MODEL = 1024
CHUNK = 64
H_A = 8
DA = 64
DVA = 2 * DA
H_B = 8
DB = 64
BAND_CHUNKS = 8
BAND_PAST = BAND_CHUNKS * CHUNK
REL_CLIP = 128
D_FF = 4 * D_MODEL
EPS = 1e-6
NEG_INF = -1e30
W_A = H_A * 2 * DA
W_B = H_B * DB

LANES = 128
NORM_CHUNK = 256
VMEM_LIMIT = 56 * 1024 * 1024

PROJ_ROWS = 256
ATT_Q_BLOCK = 1024
LOG2E = 1.4426950408889634
N_BIAS_COLS = 3
SUM_ROWS = 16
BAND_BLOCK = 1024
BAND_TILE = 256


def _rms_rows(x, gain_row):
    return x * lax.rsqrt(jnp.mean(x * x, axis=-1, keepdims=True) + EPS) * gain_row


def _dot(a, b):
    return jnp.dot(a, b, preferred_element_type=F32)


def _dot_nt(a, b):
    return lax.dot_general(a, b, (((1,), (1,)), ((), ())), preferred_element_type=F32)


def _head_norm(z, group_ones, gain_row):
    outs = []
    for c in range(z.shape[1] // NORM_CHUNK):
        zc = z[:, c * NORM_CHUNK:(c + 1) * NORM_CHUNK]
        ss = _dot((zc * zc).astype(BF16), group_ones)
        outs.append(zc * lax.rsqrt(ss * (1.0 / DA) + EPS))
    return jnp.concatenate(outs, axis=1) * gain_row


def _proj_kernel(x_ref, g1_ref, w_ref, ones_ref, gqa_ref, gka_ref, gqb_ref, gkb_ref, *rest, slab):
    (qa_ref, kaf_ref, kab_ref, vaf_ref, vab_ref,
     qb_ref, kbf_ref, kbb_ref, vbf_ref, vbb_ref) = rest[-10:]
    hb = _rms_rows(x_ref[...], g1_ref[...]).astype(BF16)
    ones = ones_ref[...]

    def store_heads(dst_ref, z):
        for d in range(dst_ref.shape[0]):
            if d == slab:
                dst_ref[d] = z.reshape(z.shape[0], H_A, DVA)
            else:
                dst_ref[d] = jnp.zeros(dst_ref.shape[1:], F32)

    qa = _head_norm(_dot(hb, w_ref[:, 0:W_A]), ones, gqa_ref[...])
    qa_ref[...] = qa.astype(BF16)
    ka = _head_norm(_dot(hb, w_ref[:, W_A:2 * W_A]), ones, gka_ref[...])
    def store_head_major(dst_ref, z):
        for hd in range(H_A):
            dst_ref[hd] = z[:, hd * DVA:(hd + 1) * DVA].astype(BF16)

    store_heads(kaf_ref, ka)
    store_head_major(kab_ref, ka)
    va = _dot(hb, w_ref[:, 2 * W_A:3 * W_A])
    store_heads(vaf_ref, va)
    store_head_major(vab_ref, va)

    o = 3 * W_A
    qb = _head_norm(_dot(hb, w_ref[:, o:o + W_B]), ones, gqb_ref[...])
    qb_ref[...] = qb.astype(BF16)
    kb = _head_norm(_dot(hb, w_ref[:, o + W_B:o + 2 * W_B]), ones, gkb_ref[...])
    kbf_ref[...] = kb
    kbb_ref[...] = kb.astype(BF16)
    vb = _dot(hb, w_ref[:, o + 2 * W_B:o + 3 * W_B])
    vbf_ref[...] = vb
    vbb_ref[...] = vb.astype(BF16)


def _const_spec(shape):
    return pl.BlockSpec(shape, lambda *_: (0,) * len(shape), pipeline_mode=pl.Buffered(1))


def _projection(x2d, g1, w_bf, ones, gqa, gka, gqb, gkb, layer, depth, caches):
    n = x2d.shape[0]
    tm = min(PROJ_ROWS, n)
    assert n % tm == 0
    row = lambda width: pl.BlockSpec((tm, width), lambda i: (i, 0))
    if caches:
        slab, slab_in_block = pl.BlockSpec((1, tm, H_A, DVA), lambda i: (layer, i, 0, 0)), 0
    else:
        slab, slab_in_block = pl.BlockSpec((depth, tm, H_A, DVA), lambda i: (0, i, 0, 0)), layer
    out_shapes = []
    out_specs = []
    for width, dtypes in ((W_A, (BF16,)), (W_A, (F32, BF16)), (W_A, (F32, BF16)),
                          (W_B, (BF16,)), (W_B, (F32, BF16)), (W_B, (F32, BF16))):
        for dt in dtypes:
            if width == W_A and dt == F32:
                out_shapes.append(jax.ShapeDtypeStruct((depth, n, H_A, DVA), F32))
                out_specs.append(slab)
            elif width == W_A and len(dtypes) == 2:
                out_shapes.append(jax.ShapeDtypeStruct((H_A, n, DVA), BF16))
                out_specs.append(pl.BlockSpec((H_A, tm, DVA), lambda i: (0, i, 0)))
            else:
                out_shapes.append(jax.ShapeDtypeStruct((n, width), dt))
                out_specs.append(row(width))
    n_in = 8
    return pl.pallas_call(
        functools.partial(_proj_kernel, slab=slab_in_block),
        grid=(n // tm,),
        in_specs=[row(D_MODEL), _const_spec((1, D_MODEL)), _const_spec(w_bf.shape),
                  _const_spec(ones.shape), _const_spec((1, W_A)), _const_spec((1, W_A)),
                  _const_spec((1, W_B)), _const_spec((1, W_B))]
                 + [pl.BlockSpec(memory_space=pl.ANY)] * len(caches),
        out_specs=out_specs,
        out_shape=out_shapes,
        input_output_aliases={n_in: 1, n_in + 1: 3} if caches else {},
        compiler_params=pltpu.CompilerParams(dimension_semantics=("parallel",),
                                             vmem_limit_bytes=VMEM_LIMIT),
        name="projection",
    )(x2d, g1, w_bf, ones, gqa, gka, gqb, gkb, *caches)


def _split_maps(q):
    lane = lax.broadcasted_iota(jnp.int32, q.shape, 1)
    zero = jnp.zeros_like(q)
    return jnp.concatenate([jnp.where(lane < DA, q, zero), jnp.where(lane >= DA, q, zero)], axis=0)


def _lambda(lamv, lam_init):
    a = jnp.sum(lamv[0:1, :] * lamv[1:2, :], axis=1, keepdims=True)
    b = jnp.sum(lamv[2:3, :] * lamv[3:4, :], axis=1, keepdims=True)
    return jnp.exp(a) - jnp.exp(b) + lam_init


def _dattn_kernel(slopes_ref, lamv_ref, q_ref, k_ref, v_ref, o_ref,
                  qz_sc, kb_sc, db_sc, sa_sc, sb_sc, ma_sc, mb_sc, m_sc, acc_sc, *, qblk, kblk, lam_init):
    h = pl.program_id(1)
    i = pl.program_id(2)
    own = qblk // kblk
    sl2 = slopes_ref[h] * LOG2E

    @pl.when(i == 0)
    def _head_tables():
        qz_sc[LANES:2 * LANES, :] = (
            lax.broadcasted_iota(jnp.int32, (LANES, 2 * qblk), 0) < N_BIAS_COLS).astype(BF16)
        r = lax.broadcasted_iota(jnp.int32, (kblk, LANES), 0)
        c = lax.broadcasted_iota(jnp.int32, (kblk, LANES), 1)
        rest = sl2 * r.astype(F32)
        kb = jnp.zeros((kblk, LANES), F32)
        for col in range(N_BIAS_COLS):
            part = rest.astype(BF16).astype(F32)
            kb = jnp.where(c == col, part, kb)
            rest = rest - part
        kb_sc[...] = kb.astype(BF16)
        ki = lax.broadcasted_iota(jnp.int32, (qblk, qblk), 0)
        qi = lax.broadcasted_iota(jnp.int32, (qblk, qblk), 1)
        db_sc[...] = jnp.where((ki // CHUNK) <= (qi // CHUNK),
                               sl2 * (qi - jnp.abs(qi - ki) - ki).astype(F32), NEG_INF)

    qt = q_ref[0].T
    row = lax.broadcasted_iota(jnp.int32, qt.shape, 0)
    zero = jnp.zeros_like(qt)
    qz_sc[0:LANES, 0:qblk] = jnp.where(row < DA, qt, zero)
    qz_sc[0:LANES, qblk:2 * qblk] = jnp.where(row >= DA, qt, zero)

    m_sc[...] = jnp.full(m_sc.shape, -jnp.inf, F32)
    acc_sc[...] = jnp.zeros(acc_sc.shape, F32)
    ones_rows = jnp.ones((SUM_ROWS, kblk), BF16)

    first_own = own * i

    def rows(j):
        return pl.ds(pl.multiple_of(j * kblk, kblk), kblk)

    def scores(j, dst, max_dst):
        s = _dot(jnp.concatenate([k_ref[0, 0, rows(j), :], kb_sc[...]], axis=1), qz_sc[...])
        dst[:, 0:2 * qblk] = s
        max_dst[...] = jnp.max(s, axis=0, keepdims=True)

    def softmax_pv(j, s, s_max):
        off = sl2 * (j * kblk - i * qblk).astype(F32)
        m_prev = m_sc[...] - off
        m_new = jnp.maximum(m_prev, s_max)
        alpha = jnp.exp2(m_prev - m_new)
        pb = jnp.exp2(s - m_new).astype(BF16)
        vt = jnp.concatenate([v_ref[0, 0, rows(j), :].T, ones_rows], axis=0)
        pv = jnp.concatenate([_dot(vt, pb[:, 0:qblk]), _dot(vt, pb[:, qblk:2 * qblk])], axis=1)
        acc_sc[...] = alpha * acc_sc[...] + pv
        m_sc[...] = m_new + off

    def both_maps(ref, rws, lo):
        return jnp.concatenate([ref[rws, lo:qblk], ref[rws, qblk + lo:2 * qblk]], axis=1)

    def own_scores(d, dst):
        lhs = jnp.concatenate([k_ref[0, 0, rows(first_own + d), :], kb_sc[...]], axis=1)
        rhs = both_maps(qz_sc, slice(None), d * kblk)
        dst[:, 0:rhs.shape[1]] = _dot(lhs, rhs)

    def own_softmax_pv(d, src):
        lo = d * kblk
        w = qblk - lo
        db = db_sc[lo:lo + kblk, lo:qblk]
        s = jnp.concatenate([src[:, 0:w] + db, src[:, w:2 * w] + db], axis=1)
        off = sl2 * jnp.float32(lo)
        m_prev = both_maps(m_sc, slice(None), lo) - off
        m_new = jnp.maximum(m_prev, jnp.max(s, axis=0, keepdims=True))
        alpha = jnp.exp2(m_prev - m_new)
        pb = jnp.exp2(s - m_new).astype(BF16)
        vt = jnp.concatenate([v_ref[0, 0, rows(first_own + d), :].T, ones_rows], axis=0)
        for half, first_lane in enumerate((lo, qblk + lo)):
            lanes = slice(first_lane, first_lane + w)
            part = slice(half * w, (half + 1) * w)
            acc_sc[:, lanes] = alpha[:, part] * acc_sc[:, lanes] + _dot(vt, pb[:, part])
            m_sc[:, lanes] = m_new[:, part] + off

    scores(0, sa_sc, ma_sc)

    def pair(j0):
        scores(j0 + 1, sb_sc, mb_sc)
        softmax_pv(j0, sa_sc[:, 0:2 * qblk], ma_sc[...])
        scores(j0 + 2, sa_sc, ma_sc)
        softmax_pv(j0 + 1, sb_sc[:, 0:2 * qblk], mb_sc[...])

    def two_pairs(qq, carry):
        pair(4 * qq)
        pair(4 * qq + 2)
        return carry

    n_pairs = first_own // 2
    lax.fori_loop(0, n_pairs // 2, two_pairs, 0)

    @pl.when(n_pairs % 2 == 1)
    def _last_pair():
        pair(first_own - 2)

    bufs = (sa_sc, sb_sc)
    for d in range(own):
        if d + 1 < own:
            own_scores(d + 1, bufs[(d + 1) % 2])
        own_softmax_pv(d, bufs[d % 2])

    inv = 1.0 / acc_sc[DVA:DVA + 1, :]
    lam = _lambda(lamv_ref[...], lam_init)
    o_ref[0] = (acc_sc[0:DVA, 0:qblk] * inv[:, 0:qblk]
                - acc_sc[0:DVA, qblk:2 * qblk] * (lam * inv[:, qblk:2 * qblk])).T


def _diff_attention_prompt(qa, kab, vab, slopes, lamv, lam_init):
    b, s, _ = qa.shape
    qblk = min(ATT_Q_BLOCK, s)
    kblk = qblk // 2
    assert s % qblk == 0 and kblk % CHUNK == 0
    head_rows = pl.BlockSpec((1, 1, s, DVA), lambda bb, h, i: (h, bb, 0, 0))
    block = pl.BlockSpec((1, qblk, DVA), lambda bb, h, i: (bb, i, h))
    return pl.pallas_call(
        functools.partial(_dattn_kernel, qblk=qblk, kblk=kblk, lam_init=lam_init),
        grid=(b, H_A, s // qblk),
        in_specs=[pl.BlockSpec(memory_space=pltpu.SMEM), pl.BlockSpec((4, DA), lambda bb, h, i: (0, 0)),
                  block, head_rows, head_rows],
        out_specs=block,
        out_shape=jax.ShapeDtypeStruct((b, s, W_A), F32),
        scratch_shapes=[pltpu.VMEM((2 * LANES, 2 * qblk), BF16), pltpu.VMEM((kblk, LANES), BF16),
                        pltpu.VMEM((qblk, qblk), F32),
                        pltpu.VMEM((kblk, 2 * qblk + LANES), F32), pltpu.VMEM((kblk, 2 * qblk + LANES), F32),
                        pltpu.VMEM((1, 2 * qblk), F32), pltpu.VMEM((1, 2 * qblk), F32),
                        pltpu.VMEM((1, 2 * qblk), F32), pltpu.VMEM((DVA + SUM_ROWS, 2 * qblk), F32)],
        compiler_params=pltpu.CompilerParams(
            dimension_semantics=("parallel", "parallel", "arbitrary"), vmem_limit_bytes=VMEM_LIMIT),
        name="diff_attention",
    )(slopes, lamv, qa, kab, vab)


def _dattn_sample_kernel(slopes_ref, lamv_ref, q_ref, kc_ref, kn_ref, vc_ref, vn_ref, o_ref,
                         kpad_sc, vpad_sc, *, t, p_len, lam_init):
    kpad_sc[...] = jnp.zeros(kpad_sc.shape, BF16)
    vpad_sc[...] = jnp.zeros(vpad_sc.shape, BF16)
    for hd in range(H_A):
        kpad_sc[0:t, hd * DVA:(hd + 1) * DVA] = kn_ref[hd, 0]
        vpad_sc[0:t, hd * DVA:(hd + 1) * DVA] = vn_ref[hd, 0]
    lam = _lambda(lamv_ref[...], lam_init)

    def distance(k0, nk, n_valid):
        qpos = lax.broadcasted_iota(jnp.int32, (t, nk), 0) + p_len
        kidx = lax.broadcasted_iota(jnp.int32, (t, nk), 1)
        kpos = kidx + k0
        visible = ((kpos // CHUNK) <= (qpos // CHUNK)) & (kidx < n_valid)
        return visible, jnp.abs(qpos - kpos).astype(F32)

    vis_c, dist_c = distance(0, p_len, p_len)
    vis_n, dist_n = distance(p_len, LANES, t)
    kc_all = kc_ref[0, 0].reshape(p_len, W_A).astype(BF16)
    vc_all = vc_ref[0, 0].reshape(p_len, W_A).astype(BF16)
    outs = []
    for hd in range(H_A):
        cols = slice(hd * DVA, (hd + 1) * DVA)
        slope = slopes_ref[hd] * LOG2E
        qz = _split_maps(q_ref[0, :, cols])

        def scores(k, vis, dist):
            s = _dot_nt(qz, k)
            bias_t = jnp.where(vis, -slope * dist, NEG_INF)
            return (s.reshape(2, t, -1) + bias_t[None]).reshape(2 * t, -1)

        s_c = scores(kc_all[:, cols], vis_c, dist_c)
        s_n = scores(kpad_sc[:, cols], vis_n, dist_n)
        m = jnp.maximum(jnp.max(s_c, axis=1, keepdims=True), jnp.max(s_n, axis=1, keepdims=True))
        p_c = jnp.exp2(s_c - m)
        p_n = jnp.exp2(s_n - m)
        l = jnp.sum(p_c, axis=1, keepdims=True) + jnp.sum(p_n, axis=1, keepdims=True)
        out = (_dot(p_c.astype(BF16), vc_all[:, cols]) + _dot(p_n.astype(BF16), vpad_sc[:, cols])) / l
        outs.append(out[0:t] - lam * out[t:2 * t])
    o_ref[0] = jnp.concatenate(outs, axis=1)


def _diff_attention_sample(qa, kab, vab, cache_k, cache_v, layer, slopes, lamv, lam_init):
    b, t, _ = qa.shape
    p_len = cache_k.shape[2]
    assert t <= LANES
    new = pl.BlockSpec((1, t, W_A), lambda bb: (bb, 0, 0))
    new_kv = pl.BlockSpec((H_A, 1, t, DVA), lambda bb: (0, bb, 0, 0))
    old = pl.BlockSpec((1, 1, p_len, H_A, DVA), lambda bb: (layer, bb, 0, 0, 0))
    return pl.pallas_call(
        functools.partial(_dattn_sample_kernel, t=t, p_len=p_len, lam_init=lam_init),
        grid=(b,),
        in_specs=[pl.BlockSpec(memory_space=pltpu.SMEM), pl.BlockSpec((4, DA), lambda bb: (0, 0)),
                  new, old, new_kv, old, new_kv],
        out_specs=new,
        out_shape=jax.ShapeDtypeStruct((b, t, W_A), F32),
        scratch_shapes=[pltpu.VMEM((LANES, W_A), BF16), pltpu.VMEM((LANES, W_A), BF16)],
        compiler_params=pltpu.CompilerParams(dimension_semantics=("parallel",),
                                             vmem_limit_bytes=VMEM_LIMIT),
        name="diff_attention_sample",
    )(slopes, lamv, qa, cache_k, kab, cache_v, vab)


def _pair_masks(q):
    lane = lax.broadcasted_iota(jnp.int32, q.shape, 1)
    zero = jnp.zeros_like(q)
    return lane, (jnp.where(lane < DB, q, zero), jnp.where(lane >= DB, q, zero))


def _battn_kernel(q_ref, kp_ref, kc_ref, vp_ref, vc_ref, bias_ref, o_ref,
                  sa_sc, sb_sc, sc_sc, sd_sc, *, blk):
    i = pl.program_id(2)
    win = BAND_PAST + BAND_TILE
    qt = q_ref[0].T
    row = lax.broadcasted_iota(jnp.int32, qt.shape, 0)
    zero = jnp.zeros_like(qt)
    qz = (jnp.where(row < DB, qt, zero), jnp.where(row >= DB, qt, zero))
    ones_rows = jnp.ones((SUM_ROWS, win), BF16)
    key_row = lax.broadcasted_iota(jnp.int32, (win, BAND_TILE), 0)
    out_row = lax.broadcasted_iota(jnp.int32, (LANES, BAND_TILE), 0)

    def window(prev_ref, cur_ref, lo):
        cur = cur_ref[0, max(lo - BAND_PAST, 0):lo + BAND_TILE, :]
        return jnp.concatenate([prev_ref[0, lo:BAND_PAST, :], cur], axis=0) if lo < BAND_PAST else cur

    def scores(item, dst):
        lo, e = item
        dst[...] = _dot(window(kp_ref, kc_ref, lo), qz[e][:, lo:lo + BAND_TILE])

    def softmax_pv(item, src):
        lo, e = item
        s = src[...] + bias_ref[e]
        n_prev = BAND_PAST - lo
        if n_prev > 0:
            s = jnp.where((i > 0) | (key_row >= n_prev), s, NEG_INF)
        pb = jnp.exp2(s - jnp.max(s, axis=0, keepdims=True)).astype(BF16)
        vt = jnp.concatenate([window(vp_ref, vc_ref, lo).T, ones_rows], axis=0)
        pv = _dot(vt, pb)
        return pv[0:LANES] / pv[LANES:LANES + 1]

    n_tiles = blk // BAND_TILE
    bufs = ((sa_sc, sb_sc), (sc_sc, sd_sc))
    tiles = []
    for e in range(2):
        scores((0, e), bufs[0][e])
    for nt in range(n_tiles):
        if nt + 1 < n_tiles:
            for e in range(2):
                scores(((nt + 1) * BAND_TILE, e), bufs[(nt + 1) % 2][e])
        outs = [softmax_pv((nt * BAND_TILE, e), bufs[nt % 2][e]) for e in range(2)]
        tiles.append(jnp.where(out_row < DB, outs[0], outs[1]))
    o_ref[0] = jnp.concatenate(tiles, axis=1).T


def _band_attention_prompt(qb, kbb, vbb, bias_t):
    b, s, _ = qb.shape
    blk = min(BAND_BLOCK, s)
    ratio = blk // BAND_PAST
    assert s % blk == 0 and blk % BAND_PAST == 0 and BAND_PAST % BAND_TILE == 0
    cur = pl.BlockSpec((1, blk, LANES), lambda pr, bb, i: (bb, i, pr))
    prev = pl.BlockSpec((1, BAND_PAST, LANES),
                        lambda pr, bb, i: (bb, jnp.maximum(i * ratio - 1, 0), pr))
    return pl.pallas_call(
        functools.partial(_battn_kernel, blk=blk),
        grid=(H_B // 2, b, s // blk),
        in_specs=[cur, prev, cur, prev, cur,
                  pl.BlockSpec((2,) + bias_t.shape[1:], lambda pr, bb, i: (pr, 0, 0))],
        out_specs=cur,
        out_shape=jax.ShapeDtypeStruct((b, s, W_B), F32),
        scratch_shapes=[pltpu.VMEM(bias_t.shape[1:], F32)] * 4,
        compiler_params=pltpu.CompilerParams(
            dimension_semantics=("parallel", "parallel", "parallel"), vmem_limit_bytes=VMEM_LIMIT),
        name="band_attention",
    )(qb, kbb, kbb, vbb, vbb, bias_t)


def _battn_sample_kernel(q_ref, kc_ref, kn_ref, vc_ref, vn_ref, biasc_ref, biasn_ref, o_ref,
                         kpad_sc, vpad_sc, *, t):
    kpad_sc[...] = jnp.zeros(kpad_sc.shape, BF16)
    vpad_sc[...] = jnp.zeros(vpad_sc.shape, BF16)
    kpad_sc[0:t, :] = kn_ref[0]
    vpad_sc[0:t, :] = vn_ref[0]
    pieces = []
    for pr in range(H_B // 2):
        cols = slice(pr * LANES, (pr + 1) * LANES)
        lane, qz = _pair_masks(q_ref[0, :, cols])
        kc = kc_ref[0, 0, :, cols].astype(BF16)
        vc = vc_ref[0, 0, :, cols].astype(BF16)
        kn = kpad_sc[:, cols]
        vn = vpad_sc[:, cols]
        outs = []
        for e in range(2):
            hd = 2 * pr + e
            s_c = _dot_nt(qz[e], kc) + biasc_ref[hd]
            s_n = _dot_nt(qz[e], kn) + biasn_ref[hd]
            m = jnp.maximum(jnp.max(s_c, axis=1, keepdims=True), jnp.max(s_n, axis=1, keepdims=True))
            p_c = jnp.exp2(s_c - m)
            p_n = jnp.exp2(s_n - m)
            l = jnp.sum(p_c, axis=1, keepdims=True) + jnp.sum(p_n, axis=1, keepdims=True)
            outs.append((_dot(p_c.astype(BF16), vc) + _dot(p_n.astype(BF16), vn)) / l)
        pieces.append(jnp.where(lane < DB, outs[0], outs[1]))
    o_ref[0] = jnp.concatenate(pieces, axis=1)


def _band_attention_sample(qb, kbb, vbb, cache_k, cache_v, layer, bias_c, bias_n):
    b, t, _ = qb.shape
    keep = cache_k.shape[2]
    new = pl.BlockSpec((1, t, W_B), lambda bb: (bb, 0, 0))
    old = pl.BlockSpec((1, 1, keep, W_B), lambda bb: (layer, bb, 0, 0))
    return pl.pallas_call(
        functools.partial(_battn_sample_kernel, t=t),
        grid=(b,),
        in_specs=[new, old, new, old, new, _const_spec(bias_c.shape), _const_spec(bias_n.shape)],
        out_specs=new,
        out_shape=jax.ShapeDtypeStruct((b, t, W_B), F32),
        scratch_shapes=[pltpu.VMEM((LANES, W_B), BF16), pltpu.VMEM((LANES, W_B), BF16)],
        compiler_params=pltpu.CompilerParams(dimension_semantics=("parallel",),
                                             vmem_limit_bytes=VMEM_LIMIT),
        name="band_attention_sample",
    )(qb, cache_k, kbb, cache_v, vbb, bias_c, bias_n)


def _band_bias(rel_bias, q0, nq, k0, nk, n_valid):
    qpos = q0 + np.arange(nq)[:, None]
    kidx = np.arange(nk)[None, :]
    kpos = k0 + kidx
    dchunk = qpos // CHUNK - kpos // CHUNK
    visible = (kpos >= 0) & (dchunk >= 0) & (dchunk <= BAND_CHUNKS) & (kidx < n_valid)
    dist = np.arange(q0 - k0 - (nk - 1), q0 - k0 + nq)
    vals = rel_bias.astype(F32)[:, np.clip(dist, -REL_CLIP, REL_CLIP) + REL_CLIP]
    w_pad = jnp.pad(vals[:, ::-1], ((0, 0), (0, 1)))
    width = nq + nk - 1
    skew = jnp.tile(w_pad, (1, nq))[:, :nq * width].reshape(-1, nq, width)
    return jnp.where(jnp.asarray(visible)[None], skew[:, :, nq - 1:nq - 1 + nk], NEG_INF)


def _merge_ffn_kernel(x_ref, oa_ref, ob_ref, g1_ref, gs_ref, wa_ref, wb_ref, wg_ref, wo_ref,
                      g2_ref, w1_ref, w2_ref, y_ref):
    x = x_ref[...]
    hb = _rms_rows(x, g1_ref[...]).astype(BF16)
    oa = oa_ref[...]
    heads = [_rms_rows(oa[:, c * DVA:(c + 1) * DVA], 1.0) for c in range(H_A)]
    oan = (jnp.concatenate(heads, axis=1) * gs_ref[...]).astype(BF16)
    ya = _dot(oan, wa_ref[...])
    yb = _dot(ob_ref[...].astype(BF16), wb_ref[...])
    gates = jax.nn.sigmoid(_dot(hb, wg_ref[...]))
    mix = gates[:, 0:D_MODEL] * ya + gates[:, D_MODEL:2 * D_MODEL] * yb
    x1 = x + _dot(mix.astype(BF16), wo_ref[...])
    h2 = _rms_rows(x1, g2_ref[...]).astype(BF16)
    u = jnp.maximum(_dot(h2, w1_ref[...]), 0.0)
    y_ref[...] = x1 + _dot((u * u).astype(BF16), w2_ref[...])


def _merge_ffn(x2d, oa2d, ob2d, g1, gs, wa, wb, wg, wo, g2, w1, w2):
    n = x2d.shape[0]
    tm = min(PROJ_ROWS, n)
    assert n % tm == 0
    row = lambda width: pl.BlockSpec((tm, width), lambda i: (i, 0))
    consts = (g1, gs, wa, wb, wg, wo, g2, w1, w2)
    return pl.pallas_call(
        _merge_ffn_kernel,
        grid=(n // tm,),
        in_specs=[row(D_MODEL), row(W_A), row(W_B)] + [_const_spec(c.shape) for c in consts],
        out_specs=row(D_MODEL),
        out_shape=jax.ShapeDtypeStruct((n, D_MODEL), F32),
        compiler_params=pltpu.CompilerParams(dimension_semantics=("parallel",),
                                             vmem_limit_bytes=VMEM_LIMIT),
        name="merge_ffn",
    )(x2d, oa2d, ob2d, *consts)


def kernel(x_prompt, x_sample, cache_a_k, cache_a_v, cache_b_k, cache_b_v, norm1_g, w_in, qn_a_g, kn_a_g, qn_b_g, kn_b_g, lam_q1, lam_k1, lam_q2, lam_k2, subln_a_g, rel_bias_b, w_br_a, w_br_b, w_gate, w_out, norm2_g, w_ff1, w_ff2):
    bp, s, d = x_prompt.shape
    bs, t, _ = x_sample.shape
    depth = w_in.shape[0]
    p_len = cache_a_k.shape[2]
    b_keep = cache_b_k.shape[2]
    keep_p = min(BAND_PAST, s)
    cdt = cache_a_k.dtype

    slopes = jnp.exp2(-8.0 * jnp.arange(1, H_A + 1, dtype=F32) / H_A)
    group = np.arange(NORM_CHUNK) // DA
    ones = jnp.asarray(group[:, None] == group[None, :], BF16)
    row = lambda v, reps, scale=1.0: (jnp.tile(v.astype(F32), reps) * scale)[None, :]
    cache_bk = cache_b_k.reshape(depth, bs, b_keep, W_B)
    cache_bv = cache_b_v.reshape(depth, bs, b_keep, W_B)

    xp = x_prompt.astype(F32).reshape(bp * s, d)
    xs = x_sample.astype(F32).reshape(bs * t, d)
    a_caches_p = ()
    a_caches_s = ()
    b_outs = [[] for _ in range(4)]
    for l in range(depth):
        lam_init = 0.8 - 0.6 * math.exp(-0.3 * l)
        lamv = jnp.stack([lam_q1[l], lam_k1[l], lam_q2[l], lam_k2[l]]).astype(F32)
        g1 = row(norm1_g[l], 1)
        proj_consts = (g1, w_in[l].astype(BF16), ones,
                       row(qn_a_g[l], 2 * H_A, DA ** -0.5 * LOG2E), row(kn_a_g[l], 2 * H_A),
                       row(qn_b_g[l], H_B, DB ** -0.5 * LOG2E), row(kn_b_g[l], H_B))
        rel_l2 = rel_bias_b[l].astype(F32) * LOG2E
        tail = (g1, row(subln_a_g[l], H_A, 1.0 - lam_init), w_br_a[l].astype(BF16),
                w_br_b[l].astype(BF16), w_gate[l].astype(BF16), w_out[l].astype(BF16),
                row(norm2_g[l], 1), w_ff1[l].astype(BF16), w_ff2[l].astype(BF16))

        qa, kaf, kab, vaf, vab, qb, kbf, kbb, vbf, vbb = _projection(xp, *proj_consts, l, depth, a_caches_p)
        a_caches_p = (kaf, vaf)
        r3 = lambda a, n: a.reshape(n, -1, a.shape[-1])
        by_head = lambda a, n: a.reshape(H_A, n, -1, DVA)
        oa = _diff_attention_prompt(r3(qa, bp), by_head(kab, bp), by_head(vab, bp), slopes, lamv, lam_init)
        win = BAND_PAST + BAND_TILE
        bias_t = _band_bias(rel_l2, BAND_PAST, BAND_TILE, 0, win, win).transpose(0, 2, 1)
        ob = _band_attention_prompt(r3(qb, bp), r3(kbb, bp), r3(vbb, bp), bias_t)
        xp = _merge_ffn(xp, oa.reshape(bp * s, W_A), ob.reshape(bp * s, W_B), *tail)
        b_outs[0].append(r3(kbf, bp)[:, s - keep_p:].reshape(bp, keep_p, H_B, DB))
        b_outs[1].append(r3(vbf, bp)[:, s - keep_p:].reshape(bp, keep_p, H_B, DB))

        qa, kaf, kab, vaf, vab, qb, kbf, kbb, vbf, vbb = _projection(xs, *proj_consts, l, depth, a_caches_s)
        a_caches_s = (kaf, vaf)
        oa = _diff_attention_sample(r3(qa, bs), by_head(kab, bs), by_head(vab, bs), cache_a_k, cache_a_v, l,
                                    slopes, lamv, lam_init)
        bias_c = _band_bias(rel_l2, p_len, t, p_len - b_keep, b_keep, b_keep)
        bias_n = _band_bias(rel_l2, p_len, t, p_len, LANES, t)
        ob = _band_attention_sample(r3(qb, bs), r3(kbb, bs), r3(vbb, bs), cache_bk, cache_bv, l,
                                    bias_c, bias_n)
        xs = _merge_ffn(xs, oa.reshape(bs * t, W_A), ob.reshape(bs * t, W_B), *tail)
        b_outs[2].append(kbf.reshape(bs, t, H_B, DB))
        b_outs[3].append(vbf.reshape(bs, t, H_B, DB))

    bk_p, bv_p, bk_s, bv_s = [jnp.stack(o).astype(cdt) for o in b_outs]
    ak_p, av_p = [a.reshape(depth, bp, s, H_A, DVA).astype(cdt) for a in a_caches_p]
    ak_s, av_s = [a.reshape(depth, bs, t, H_A, DVA).astype(cdt) for a in a_caches_s]
    return (xp.reshape(bp, s, d).astype(x_prompt.dtype), xs.reshape(bs, t, d).astype(x_sample.dtype),
            ak_p, av_p, bk_p, bv_p, ak_s, av_s, bk_s, bv_s)
```

```python
import functools
import math

import numpy as np
import jax
import jax.numpy as jnp
from jax import lax
from jax.experimental import pallas as pl
from jax.experimental.pallas import tpu as pltpu

F32 = jnp.float32
BF16 = jnp.bfloat16

D_MODEL = 1024
CHUNK = 64
H_A = 8
DA = 64
DVA = 2 * DA
H_B = 8
DB = 64
BAND_CHUNKS = 8
BAND_PAST = BAND_CHUNKS * CHUNK
REL_CLIP = 128
D_FF = 4 * D_MODEL
EPS = 1e-6
NEG_INF = -1e30
W_A = H_A * 2 * DA
W_B = H_B * DB

LANES = 128
NORM_CHUNK = 256
VMEM_LIMIT = 56 * 1024 * 1024

PROJ_ROWS = 256
ATT_Q_BLOCK = 1024
LOG2E = 1.4426950408889634
N_BIAS_COLS = 3
SUM_ROWS = 16
BAND_BLOCK = 1024
BAND_TILE = 256


def _rms_rows(x, gain_row):
    return x * lax.rsqrt(jnp.mean(x * x, axis=-1, keepdims=True) + EPS) * gain_row


def _dot(a, b):
    return jnp.dot(a, b, preferred_element_type=F32)


def _dot_nt(a, b):
    return lax.dot_general(a, b, (((1,), (1,)), ((), ())), preferred_element_type=F32)


def _head_norm(z, group_ones, gain_row):
    outs = []
    for c in range(z.shape[1] // NORM_CHUNK):
        zc = z[:, c * NORM_CHUNK:(c + 1) * NORM_CHUNK]
        ss = _dot((zc * zc).astype(BF16), group_ones)
        outs.append(zc * lax.rsqrt(ss * (1.0 / DA) + EPS))
    return jnp.concatenate(outs, axis=1) * gain_row


def _proj_kernel(x_ref, g1_ref, w_ref, ones_ref, gqa_ref, gka_ref, gqb_ref, gkb_ref, *rest, slab):
    (qa_ref, kaf_ref, kab_ref, vaf_ref, vab_ref,
     qb_ref, kbf_ref, kbb_ref, vbf_ref, vbb_ref) = rest[-10:]
    hb = _rms_rows(x_ref[...], g1_ref[...]).astype(BF16)
    ones = ones_ref[...]

    def store_heads(dst_ref, z):
        for d in range(dst_ref.shape[0]):
            if d == slab:
                dst_ref[d] = z.reshape(z.shape[0], H_A, DVA)
            else:
                dst_ref[d] = jnp.zeros(dst_ref.shape[1:], F32)

    qa = _head_norm(_dot(hb, w_ref[:, 0:W_A]), ones, gqa_ref[...])
    qa_ref[...] = qa.astype(BF16)
    ka = _head_norm(_dot(hb, w_ref[:, W_A:2 * W_A]), ones, gka_ref[...])
    def store_head_major(dst_ref, z):
        for hd in range(H_A):
            dst_ref[hd] = z[:, hd * DVA:(hd + 1) * DVA].astype(BF16)

    store_heads(kaf_ref, ka)
    store_head_major(kab_ref, ka)
    va = _dot(hb, w_ref[:, 2 * W_A:3 * W_A])
    store_heads(vaf_ref, va)
    store_head_major(vab_ref, va)

    o = 3 * W_A
    qb = _head_norm(_dot(hb, w_ref[:, o:o + W_B]), ones, gqb_ref[...])
    qb_ref[...] = qb.astype(BF16)
    kb = _head_norm(_dot(hb, w_ref[:, o + W_B:o + 2 * W_B]), ones, gkb_ref[...])
    kbf_ref[...] = kb
    kbb_ref[...] = kb.astype(BF16)
    vb = _dot(hb, w_ref[:, o + 2 * W_B:o + 3 * W_B])
    vbf_ref[...] = vb
    vbb_ref[...] = vb.astype(BF16)


def _const_spec(shape):
    return pl.BlockSpec(shape, lambda *_: (0,) * len(shape), pipeline_mode=pl.Buffered(1))


def _projection(x2d, g1, w_bf, ones, gqa, gka, gqb, gkb, layer, depth, caches):
    n = x2d.shape[0]
    tm = min(PROJ_ROWS, n)
    assert n % tm == 0
    row = lambda width: pl.BlockSpec((tm, width), lambda i: (i, 0))
    if caches:
        slab, slab_in_block = pl.BlockSpec((1, tm, H_A, DVA), lambda i: (layer, i, 0, 0)), 0
    else:
        slab, slab_in_block = pl.BlockSpec((depth, tm, H_A, DVA), lambda i: (0, i, 0, 0)), layer
    out_shapes = []
    out_specs = []
    for width, dtypes in ((W_A, (BF16,)), (W_A, (F32, BF16)), (W_A, (F32, BF16)),
                          (W_B, (BF16,)), (W_B, (F32, BF16)), (W_B, (F32, BF16))):
        for dt in dtypes:
            if width == W_A and dt == F32:
                out_shapes.append(jax.ShapeDtypeStruct((depth, n, H_A, DVA), F32))
                out_specs.append(slab)
            elif width == W_A and len(dtypes) == 2:
                out_shapes.append(jax.ShapeDtypeStruct((H_A, n, DVA), BF16))
                out_specs.append(pl.BlockSpec((H_A, tm, DVA), lambda i: (0, i, 0)))
            else:
                out_shapes.append(jax.ShapeDtypeStruct((n, width), dt))
                out_specs.append(row(width))
    n_in = 8
    return pl.pallas_call(
        functools.partial(_proj_kernel, slab=slab_in_block),
        grid=(n // tm,),
        in_specs=[row(D_MODEL), _const_spec((1, D_MODEL)), _const_spec(w_bf.shape),
                  _const_spec(ones.shape), _const_spec((1, W_A)), _const_spec((1, W_A)),
                  _const_spec((1, W_B)), _const_spec((1, W_B))]
                 + [pl.BlockSpec(memory_space=pl.ANY)] * len(caches),
        out_specs=out_specs,
        out_shape=out_shapes,
        input_output_aliases={n_in: 1, n_in + 1: 3} if caches else {},
        compiler_params=pltpu.CompilerParams(dimension_semantics=("parallel",),
                                             vmem_limit_bytes=VMEM_LIMIT),
        name="projection",
    )(x2d, g1, w_bf, ones, gqa, gka, gqb, gkb, *caches)


def _split_maps(q):
    lane = lax.broadcasted_iota(jnp.int32, q.shape, 1)
    zero = jnp.zeros_like(q)
    return jnp.concatenate([jnp.where(lane < DA, q, zero), jnp.where(lane >= DA, q, zero)], axis=0)


def _lambda(lamv, lam_init):
    a = jnp.sum(lamv[0:1, :] * lamv[1:2, :], axis=1, keepdims=True)
    b = jnp.sum(lamv[2:3, :] * lamv[3:4, :], axis=1, keepdims=True)
    return jnp.exp(a) - jnp.exp(b) + lam_init


def _dattn_kernel(slopes_ref, lamv_ref, q_ref, k_ref, v_ref, o_ref,
                  qz_sc, kb_sc, db_sc, sa_sc, sb_sc, ma_sc, mb_sc, m_sc, acc_sc, *, qblk, kblk, lam_init):
    h = pl.program_id(1)
    i = pl.program_id(2)
    own = qblk // kblk
    sl2 = slopes_ref[h] * LOG2E

    @pl.when(i == 0)
    def _head_tables():
        qz_sc[LANES:2 * LANES, 0:2 * qblk] = (
            lax.broadcasted_iota(jnp.int32, (LANES, 2 * qblk), 0) < N_BIAS_COLS).astype(BF16)
        r = lax.broadcasted_iota(jnp.int32, (kblk, LANES), 0)
        c = lax.broadcasted_iota(jnp.int32, (kblk, LANES), 1)
        rest = sl2 * r.astype(F32)
        kb = jnp.zeros((kblk, LANES), F32)
        for col in range(N_BIAS_COLS):
            part = rest.astype(BF16).astype(F32)
            kb = jnp.where(c == col, part, kb)
            rest = rest - part
        kb_sc[...] = kb.astype(BF16)
        ki = lax.broadcasted_iota(jnp.int32, (qblk, qblk), 0)
        qi = lax.broadcasted_iota(jnp.int32, (qblk, qblk), 1)
        db_sc[:, 0:qblk] = jnp.where((ki // CHUNK) <= (qi // CHUNK),
                                     sl2 * (qi - jnp.abs(qi - ki) - ki).astype(F32), NEG_INF)

    qt = q_ref[0].T
    row = lax.broadcasted_iota(jnp.int32, qt.shape, 0)
    zero = jnp.zeros_like(qt)
    qz_sc[0:LANES, 0:qblk] = jnp.where(row < DA, qt, zero)
    qz_sc[0:LANES, qblk:2 * qblk] = jnp.where(row >= DA, qt, zero)

    m_sc[...] = jnp.full(m_sc.shape, -jnp.inf, F32)
    acc_sc[...] = jnp.zeros(acc_sc.shape, F32)
    ones_rows = jnp.ones((SUM_ROWS, kblk), BF16)

    first_own = own * i

    def rows(j):
        return pl.ds(pl.multiple_of(j * kblk, kblk), kblk)

    def scores(j, dst, max_dst):
        s = _dot(jnp.concatenate([k_ref[0, 0, rows(j), :], kb_sc[...]], axis=1), qz_sc[:, 0:2 * qblk])
        dst[:, 0:2 * qblk] = s
        max_dst[...] = jnp.max(s, axis=0, keepdims=True)

    def softmax_pv(j, s, s_max):
        off = sl2 * (j * kblk - i * qblk).astype(F32)
        m_prev = m_sc[...] - off
        m_new = jnp.maximum(m_prev, s_max)
        alpha = jnp.exp2(m_prev - m_new)
        pb = jnp.exp2(s - m_new).astype(BF16)
        vt = jnp.concatenate([v_ref[0, 0, rows(j), :].T, ones_rows], axis=0)
        pv = jnp.concatenate([_dot(vt, pb[:, 0:qblk]), _dot(vt, pb[:, qblk:2 * qblk])], axis=1)
        acc_sc[:, 0:2 * qblk] = alpha * acc_sc[:, 0:2 * qblk] + pv
        m_sc[...] = m_new + off

    def both_maps(ref, rws, lo):
        return jnp.concatenate([ref[rws, lo:qblk], ref[rws, qblk + lo:2 * qblk]], axis=1)

    def own_scores(d, dst):
        lhs = jnp.concatenate([k_ref[0, 0, rows(first_own + d), :], kb_sc[...]], axis=1)
        rhs = both_maps(qz_sc, slice(None), d * kblk)
        dst[:, 0:rhs.shape[1]] = _dot(lhs, rhs)

    def own_softmax_pv(d, src):
        lo = d * kblk
        w = qblk - lo
        db = db_sc[lo:lo + kblk, lo:qblk]
        s = jnp.concatenate([src[:, 0:w] + db, src[:, w:2 * w] + db], axis=1)
        off = sl2 * jnp.float32(lo)
        m_prev = both_maps(m_sc, slice(None), lo) - off
        m_new = jnp.maximum(m_prev, jnp.max(s, axis=0, keepdims=True))
        alpha = jnp.exp2(m_prev - m_new)
        pb = jnp.exp2(s - m_new).astype(BF16)
        vt = jnp.concatenate([v_ref[0, 0, rows(first_own + d), :].T, ones_rows], axis=0)
        for half, first_lane in enumerate((lo, qblk + lo)):
            lanes = slice(first_lane, first_lane + w)
            part = slice(half * w, (half + 1) * w)
            acc_sc[:, lanes] = alpha[:, part] * acc_sc[:, lanes] + _dot(vt, pb[:, part])
            m_sc[:, lanes] = m_new[:, part] + off

    scores(0, sa_sc, ma_sc)

    def pair(j0):
        scores(j0 + 1, sb_sc, mb_sc)
        softmax_pv(j0, sa_sc[:, 0:2 * qblk], ma_sc[...])
        scores(j0 + 2, sa_sc, ma_sc)
        softmax_pv(j0 + 1, sb_sc[:, 0:2 * qblk], mb_sc[...])

    def two_pairs(qq, carry):
        pair(4 * qq)
        pair(4 * qq + 2)
        return carry

    n_pairs = first_own // 2
    lax.fori_loop(0, n_pairs // 2, two_pairs, 0)

    @pl.when(n_pairs % 2 == 1)
    def _last_pair():
        pair(first_own - 2)

    bufs = (sa_sc, sb_sc)
    for d in range(own):
        if d + 1 < own:
            own_scores(d + 1, bufs[(d + 1) % 2])
        own_softmax_pv(d, bufs[d % 2])

    inv = 1.0 / acc_sc[DVA:DVA + 1, 0:2 * qblk]
    lam = _lambda(lamv_ref[...], lam_init)
    o_ref[0] = (acc_sc[0:DVA, 0:qblk] * inv[:, 0:qblk]
                - acc_sc[0:DVA, qblk:2 * qblk] * (lam * inv[:, qblk:2 * qblk])).T


def _diff_attention_prompt(qa, kab, vab, slopes, lamv, lam_init):
    b, s, _ = qa.shape
    qblk = min(ATT_Q_BLOCK, s)
    kblk = qblk // 2
    assert s % qblk == 0 and kblk % CHUNK == 0
    head_rows = pl.BlockSpec((1, 1, s, DVA), lambda bb, h, i: (h, bb, 0, 0))
    block = pl.BlockSpec((1, qblk, DVA), lambda bb, h, i: (bb, i, h))
    return pl.pallas_call(
        functools.partial(_dattn_kernel, qblk=qblk, kblk=kblk, lam_init=lam_init),
        grid=(b, H_A, s // qblk),
        in_specs=[pl.BlockSpec(memory_space=pltpu.SMEM), pl.BlockSpec((4, DA), lambda bb, h, i: (0, 0)),
                  block, head_rows, head_rows],
        out_specs=block,
        out_shape=jax.ShapeDtypeStruct((b, s, W_A), F32),
        scratch_shapes=[pltpu.VMEM((2 * LANES, 2 * qblk + LANES), BF16), pltpu.VMEM((kblk, LANES), BF16),
                        pltpu.VMEM((qblk, qblk + LANES), F32),
                        pltpu.VMEM((kblk, 2 * qblk + LANES), F32), pltpu.VMEM((kblk, 2 * qblk + LANES), F32),
                        pltpu.VMEM((1, 2 * qblk), F32), pltpu.VMEM((1, 2 * qblk), F32),
                        pltpu.VMEM((1, 2 * qblk), F32),
                        pltpu.VMEM((DVA + SUM_ROWS, 2 * qblk + LANES), F32)],
        compiler_params=pltpu.CompilerParams(
            dimension_semantics=("parallel", "parallel", "arbitrary"), vmem_limit_bytes=VMEM_LIMIT),
        name="diff_attention",
    )(slopes, lamv, qa, kab, vab)


def _dattn_sample_kernel(slopes_ref, lamv_ref, q_ref, kc_ref, kn_ref, vc_ref, vn_ref, o_ref,
                         kpad_sc, vpad_sc, *, t, p_len, lam_init):
    kpad_sc[...] = jnp.zeros(kpad_sc.shape, BF16)
    vpad_sc[...] = jnp.zeros(vpad_sc.shape, BF16)
    for hd in range(H_A):
        kpad_sc[0:t, hd * DVA:(hd + 1) * DVA] = kn_ref[hd, 0]
        vpad_sc[0:t, hd * DVA:(hd + 1) * DVA] = vn_ref[hd, 0]
    lam = _lambda(lamv_ref[...], lam_init)

    def distance(k0, nk, n_valid):
        qpos = lax.broadcasted_iota(jnp.int32, (t, nk), 0) + p_len
        kidx = lax.broadcasted_iota(jnp.int32, (t, nk), 1)
        kpos = kidx + k0
        visible = ((kpos // CHUNK) <= (qpos // CHUNK)) & (kidx < n_valid)
        return visible, jnp.abs(qpos - kpos).astype(F32)

    vis_c, dist_c = distance(0, p_len, p_len)
    vis_n, dist_n = distance(p_len, LANES, t)
    kc_all = kc_ref[0, 0].reshape(p_len, W_A).astype(BF16)
    vc_all = vc_ref[0, 0].reshape(p_len, W_A).astype(BF16)
    outs = []
    for hd in range(H_A):
        cols = slice(hd * DVA, (hd + 1) * DVA)
        slope = slopes_ref[hd] * LOG2E
        qz = _split_maps(q_ref[0, :, cols])

        def scores(k, vis, dist):
            s = _dot_nt(qz, k)
            bias_t = jnp.where(vis, -slope * dist, NEG_INF)
            return (s.reshape(2, t, -1) + bias_t[None]).reshape(2 * t, -1)

        s_c = scores(kc_all[:, cols], vis_c, dist_c)
        s_n = scores(kpad_sc[:, cols], vis_n, dist_n)
        m = jnp.maximum(jnp.max(s_c, axis=1, keepdims=True), jnp.max(s_n, axis=1, keepdims=True))
        p_c = jnp.exp2(s_c - m)
        p_n = jnp.exp2(s_n - m)
        l = jnp.sum(p_c, axis=1, keepdims=True) + jnp.sum(p_n, axis=1, keepdims=True)
        out = (_dot(p_c.astype(BF16), vc_all[:, cols]) + _dot(p_n.astype(BF16), vpad_sc[:, cols])) / l
        outs.append(out[0:t] - lam * out[t:2 * t])
    o_ref[0] = jnp.concatenate(outs, axis=1)


def _diff_attention_sample(qa, kab, vab, cache_k, cache_v, layer, slopes, lamv, lam_init):
    b, t, _ = qa.shape
    p_len = cache_k.shape[2]
    assert t <= LANES
    new = pl.BlockSpec((1, t, W_A), lambda bb: (bb, 0, 0))
    new_kv = pl.BlockSpec((H_A, 1, t, DVA), lambda bb: (0, bb, 0, 0))
    old = pl.BlockSpec((1, 1, p_len, H_A, DVA), lambda bb: (layer, bb, 0, 0, 0))
    return pl.pallas_call(
        functools.partial(_dattn_sample_kernel, t=t, p_len=p_len, lam_init=lam_init),
        grid=(b,),
        in_specs=[pl.BlockSpec(memory_space=pltpu.SMEM), pl.BlockSpec((4, DA), lambda bb: (0, 0)),
                  new, old, new_kv, old, new_kv],
        out_specs=new,
        out_shape=jax.ShapeDtypeStruct((b, t, W_A), F32),
        scratch_shapes=[pltpu.VMEM((LANES, W_A), BF16), pltpu.VMEM((LANES, W_A), BF16)],
        compiler_params=pltpu.CompilerParams(dimension_semantics=("parallel",),
                                             vmem_limit_bytes=VMEM_LIMIT),
        name="diff_attention_sample",
    )(slopes, lamv, qa, cache_k, kab, cache_v, vab)


def _pair_masks(q):
    lane = lax.broadcasted_iota(jnp.int32, q.shape, 1)
    zero = jnp.zeros_like(q)
    return lane, (jnp.where(lane < DB, q, zero), jnp.where(lane >= DB, q, zero))


def _battn_kernel(q_ref, kp_ref, kc_ref, vp_ref, vc_ref, bias_ref, o_ref,
                  sa_sc, sb_sc, sc_sc, sd_sc, *, blk):
    i = pl.program_id(2)
    win = BAND_PAST + BAND_TILE
    qt = q_ref[0].T
    row = lax.broadcasted_iota(jnp.int32, qt.shape, 0)
    zero = jnp.zeros_like(qt)
    qz = (jnp.where(row < DB, qt, zero), jnp.where(row >= DB, qt, zero))
    ones_rows = jnp.ones((SUM_ROWS, win), BF16)
    key_row = lax.broadcasted_iota(jnp.int32, (win, BAND_TILE), 0)
    out_row = lax.broadcasted_iota(jnp.int32, (LANES, BAND_TILE), 0)

    def window(prev_ref, cur_ref, lo):
        cur = cur_ref[0, max(lo - BAND_PAST, 0):lo + BAND_TILE, :]
        return jnp.concatenate([prev_ref[0, lo:BAND_PAST, :], cur], axis=0) if lo < BAND_PAST else cur

    def scores(item, dst):
        lo, e = item
        dst[:, 0:BAND_TILE] = _dot(window(kp_ref, kc_ref, lo), qz[e][:, lo:lo + BAND_TILE])

    def softmax_pv(item, src):
        lo, e = item
        s = src[:, 0:BAND_TILE] + bias_ref[e, :, 0:BAND_TILE]
        n_prev = BAND_PAST - lo
        if n_prev > 0:
            s = jnp.where((i > 0) | (key_row >= n_prev), s, NEG_INF)
        pb = jnp.exp2(s - jnp.max(s, axis=0, keepdims=True)).astype(BF16)
        vt = jnp.concatenate([window(vp_ref, vc_ref, lo).T, ones_rows], axis=0)
        pv = _dot(vt, pb)
        return pv[0:LANES] / pv[LANES:LANES + 1]

    n_tiles = blk // BAND_TILE
    bufs = ((sa_sc, sb_sc), (sc_sc, sd_sc))
    tiles = []
    for e in range(2):
        scores((0, e), bufs[0][e])
    for nt in range(n_tiles):
        if nt + 1 < n_tiles:
            for e in range(2):
                scores(((nt + 1) * BAND_TILE, e), bufs[(nt + 1) % 2][e])
        outs = [softmax_pv((nt * BAND_TILE, e), bufs[nt % 2][e]) for e in range(2)]
        tiles.append(jnp.where(out_row < DB, outs[0], outs[1]))
    o_ref[0] = jnp.concatenate(tiles, axis=1).T


def _band_attention_prompt(qb, kbb, vbb, bias_t):
    b, s, _ = qb.shape
    blk = min(BAND_BLOCK, s)
    ratio = blk // BAND_PAST
    assert s % blk == 0 and blk % BAND_PAST == 0 and BAND_PAST % BAND_TILE == 0
    cur = pl.BlockSpec((1, blk, LANES), lambda pr, bb, i: (bb, i, pr))
    prev = pl.BlockSpec((1, BAND_PAST, LANES),
                        lambda pr, bb, i: (bb, jnp.maximum(i * ratio - 1, 0), pr))
    return pl.pallas_call(
        functools.partial(_battn_kernel, blk=blk),
        grid=(H_B // 2, b, s // blk),
        in_specs=[cur, prev, cur, prev, cur,
                  pl.BlockSpec((2,) + bias_t.shape[1:], lambda pr, bb, i: (pr, 0, 0))],
        out_specs=cur,
        out_shape=jax.ShapeDtypeStruct((b, s, W_B), F32),
        scratch_shapes=[pltpu.VMEM((bias_t.shape[1], BAND_TILE + LANES), F32)] * 4,
        compiler_params=pltpu.CompilerParams(
            dimension_semantics=("parallel", "parallel", "parallel"), vmem_limit_bytes=VMEM_LIMIT),
        name="band_attention",
    )(qb, kbb, kbb, vbb, vbb, bias_t)


def _battn_sample_kernel(q_ref, kc_ref, kn_ref, vc_ref, vn_ref, biasc_ref, biasn_ref, o_ref,
                         kpad_sc, vpad_sc, *, t):
    kpad_sc[...] = jnp.zeros(kpad_sc.shape, BF16)
    vpad_sc[...] = jnp.zeros(vpad_sc.shape, BF16)
    kpad_sc[0:t, :] = kn_ref[0]
    vpad_sc[0:t, :] = vn_ref[0]
    pieces = []
    for pr in range(H_B // 2):
        cols = slice(pr * LANES, (pr + 1) * LANES)
        lane, qz = _pair_masks(q_ref[0, :, cols])
        kc = kc_ref[0, 0, :, cols].astype(BF16)
        vc = vc_ref[0, 0, :, cols].astype(BF16)
        kn = kpad_sc[:, cols]
        vn = vpad_sc[:, cols]
        outs = []
        for e in range(2):
            hd = 2 * pr + e
            s_c = _dot_nt(qz[e], kc) + biasc_ref[hd]
            s_n = _dot_nt(qz[e], kn) + biasn_ref[hd]
            m = jnp.maximum(jnp.max(s_c, axis=1, keepdims=True), jnp.max(s_n, axis=1, keepdims=True))
            p_c = jnp.exp2(s_c - m)
            p_n = jnp.exp2(s_n - m)
            l = jnp.sum(p_c, axis=1, keepdims=True) + jnp.sum(p_n, axis=1, keepdims=True)
            outs.append((_dot(p_c.astype(BF16), vc) + _dot(p_n.astype(BF16), vn)) / l)
        pieces.append(jnp.where(lane < DB, outs[0], outs[1]))
    o_ref[0] = jnp.concatenate(pieces, axis=1)


def _band_attention_sample(qb, kbb, vbb, cache_k, cache_v, layer, bias_c, bias_n):
    b, t, _ = qb.shape
    keep = cache_k.shape[2]
    new = pl.BlockSpec((1, t, W_B), lambda bb: (bb, 0, 0))
    old = pl.BlockSpec((1, 1, keep, W_B), lambda bb: (layer, bb, 0, 0))
    return pl.pallas_call(
        functools.partial(_battn_sample_kernel, t=t),
        grid=(b,),
        in_specs=[new, old, new, old, new, _const_spec(bias_c.shape), _const_spec(bias_n.shape)],
        out_specs=new,
        out_shape=jax.ShapeDtypeStruct((b, t, W_B), F32),
        scratch_shapes=[pltpu.VMEM((LANES, W_B), BF16), pltpu.VMEM((LANES, W_B), BF16)],
        compiler_params=pltpu.CompilerParams(dimension_semantics=("parallel",),
                                             vmem_limit_bytes=VMEM_LIMIT),
        name="band_attention_sample",
    )(qb, cache_k, kbb, cache_v, vbb, bias_c, bias_n)


def _band_bias(rel_bias, q0, nq, k0, nk, n_valid):
    qpos = q0 + np.arange(nq)[:, None]
    kidx = np.arange(nk)[None, :]
    kpos = k0 + kidx
    dchunk = qpos // CHUNK - kpos // CHUNK
    visible = (kpos >= 0) & (dchunk >= 0) & (dchunk <= BAND_CHUNKS) & (kidx < n_valid)
    dist = np.arange(q0 - k0 - (nk - 1), q0 - k0 + nq)
    vals = rel_bias.astype(F32)[:, np.clip(dist, -REL_CLIP, REL_CLIP) + REL_CLIP]
    w_pad = jnp.pad(vals[:, ::-1], ((0, 0), (0, 1)))
    width = nq + nk - 1
    skew = jnp.tile(w_pad, (1, nq))[:, :nq * width].reshape(-1, nq, width)
    return jnp.where(jnp.asarray(visible)[None], skew[:, :, nq - 1:nq - 1 + nk], NEG_INF)


def _merge_ffn_kernel(x_ref, oa_ref, ob_ref, g1_ref, gs_ref, wa_ref, wb_ref, wg_ref, wo_ref,
                      g2_ref, w1_ref, w2_ref, y_ref):
    x = x_ref[...]
    hb = _rms_rows(x, g1_ref[...]).astype(BF16)
    oa = oa_ref[...]
    heads = [_rms_rows(oa[:, c * DVA:(c + 1) * DVA], 1.0) for c in range(H_A)]
    oan = (jnp.concatenate(heads, axis=1) * gs_ref[...]).astype(BF16)
    ya = _dot(oan, wa_ref[...])
    yb = _dot(ob_ref[...].astype(BF16), wb_ref[...])
    gates = jax.nn.sigmoid(_dot(hb, wg_ref[...]))
    mix = gates[:, 0:D_MODEL] * ya + gates[:, D_MODEL:2 * D_MODEL] * yb
    x1 = x + _dot(mix.astype(BF16), wo_ref[...])
    h2 = _rms_rows(x1, g2_ref[...]).astype(BF16)
    u = jnp.maximum(_dot(h2, w1_ref[...]), 0.0)
    y_ref[...] = x1 + _dot((u * u).astype(BF16), w2_ref[...])


def _merge_ffn(x2d, oa2d, ob2d, g1, gs, wa, wb, wg, wo, g2, w1, w2):
    n = x2d.shape[0]
    tm = min(PROJ_ROWS, n)
    assert n % tm == 0
    row = lambda width: pl.BlockSpec((tm, width), lambda i: (i, 0))
    consts = (g1, gs, wa, wb, wg, wo, g2, w1, w2)
    return pl.pallas_call(
        _merge_ffn_kernel,
        grid=(n // tm,),
        in_specs=[row(D_MODEL), row(W_A), row(W_B)] + [_const_spec(c.shape) for c in consts],
        out_specs=row(D_MODEL),
        out_shape=jax.ShapeDtypeStruct((n, D_MODEL), F32),
        compiler_params=pltpu.CompilerParams(dimension_semantics=("parallel",),
                                             vmem_limit_bytes=VMEM_LIMIT),
        name="merge_ffn",
    )(x2d, oa2d, ob2d, *consts)


def kernel(x_prompt, x_sample, cache_a_k, cache_a_v, cache_b_k, cache_b_v, norm1_g, w_in, qn_a_g, kn_a_g, qn_b_g, kn_b_g, lam_q1, lam_k1, lam_q2, lam_k2, subln_a_g, rel_bias_b, w_br_a, w_br_b, w_gate, w_out, norm2_g, w_ff1, w_ff2):
    bp, s, d = x_prompt.shape
    bs, t, _ = x_sample.shape
    depth = w_in.shape[0]
    p_len = cache_a_k.shape[2]
    b_keep = cache_b_k.shape[2]
    keep_p = min(BAND_PAST, s)
    cdt = cache_a_k.dtype

    slopes = jnp.exp2(-8.0 * jnp.arange(1, H_A + 1, dtype=F32) / H_A)
    group = np.arange(NORM_CHUNK) // DA
    ones = jnp.asarray(group[:, None] == group[None, :], BF16)
    row = lambda v, reps, scale=1.0: (jnp.tile(v.astype(F32), reps) * scale)[None, :]
    cache_bk = cache_b_k.reshape(depth, bs, b_keep, W_B)
    cache_bv = cache_b_v.reshape(depth, bs, b_keep, W_B)

    xp = x_prompt.astype(F32).reshape(bp * s, d)
    xs = x_sample.astype(F32).reshape(bs * t, d)
    a_caches_p = ()
    a_caches_s = ()
    b_outs = [[] for _ in range(4)]
    for l in range(depth):
        lam_init = 0.8 - 0.6 * math.exp(-0.3 * l)
        lamv = jnp.stack([lam_q1[l], lam_k1[l], lam_q2[l], lam_k2[l]]).astype(F32)
        g1 = row(norm1_g[l], 1)
        proj_consts = (g1, w_in[l].astype(BF16), ones,
                       row(qn_a_g[l], 2 * H_A, DA ** -0.5 * LOG2E), row(kn_a_g[l], 2 * H_A),
                       row(qn_b_g[l], H_B, DB ** -0.5 * LOG2E), row(kn_b_g[l], H_B))
        rel_l2 = rel_bias_b[l].astype(F32) * LOG2E
        tail = (g1, row(subln_a_g[l], H_A, 1.0 - lam_init), w_br_a[l].astype(BF16),
                w_br_b[l].astype(BF16), w_gate[l].astype(BF16), w_out[l].astype(BF16),
                row(norm2_g[l], 1), w_ff1[l].astype(BF16), w_ff2[l].astype(BF16))

        qa, kaf, kab, vaf, vab, qb, kbf, kbb, vbf, vbb = _projection(xp, *proj_consts, l, depth, a_caches_p)
        a_caches_p = (kaf, vaf)
        r3 = lambda a, n: a.reshape(n, -1, a.shape[-1])
        by_head = lambda a, n: a.reshape(H_A, n, -1, DVA)
        oa = _diff_attention_prompt(r3(qa, bp), by_head(kab, bp), by_head(vab, bp), slopes, lamv, lam_init)
        win = BAND_PAST + BAND_TILE
        bias_t = _band_bias(rel_l2, BAND_PAST, BAND_TILE, 0, win, win).transpose(0, 2, 1)
        bias_t = jnp.pad(bias_t, ((0, 0), (0, 0), (0, LANES)))
        ob = _band_attention_prompt(r3(qb, bp), r3(kbb, bp), r3(vbb, bp), bias_t)
        xp = _merge_ffn(xp, oa.reshape(bp * s, W_A), ob.reshape(bp * s, W_B), *tail)
        b_outs[0].append(r3(kbf, bp)[:, s - keep_p:].reshape(bp, keep_p, H_B, DB))
        b_outs[1].append(r3(vbf, bp)[:, s - keep_p:].reshape(bp, keep_p, H_B, DB))

        qa, kaf, kab, vaf, vab, qb, kbf, kbb, vbf, vbb = _projection(xs, *proj_consts, l, depth, a_caches_s)
        a_caches_s = (kaf, vaf)
        oa = _diff_attention_sample(r3(qa, bs), by_head(kab, bs), by_head(vab, bs), cache_a_k, cache_a_v, l,
                                    slopes, lamv, lam_init)
        bias_c = _band_bias(rel_l2, p_len, t, p_len - b_keep, b_keep, b_keep)
        bias_n = _band_bias(rel_l2, p_len, t, p_len, LANES, t)
        ob = _band_attention_sample(r3(qb, bs), r3(kbb, bs), r3(vbb, bs), cache_bk, cache_bv, l,
                                    bias_c, bias_n)
        xs = _merge_ffn(xs, oa.reshape(bs * t, W_A), ob.reshape(bs * t, W_B), *tail)
        b_outs[2].append(kbf.reshape(bs, t, H_B, DB))
        b_outs[3].append(vbf.reshape(bs, t, H_B, DB))

    bk_p, bv_p, bk_s, bv_s = [jnp.stack(o).astype(cdt) for o in b_outs]
    ak_p, av_p = [a.reshape(depth, bp, s, H_A, DVA).astype(cdt) for a in a_caches_p]
    ak_s, av_s = [a.reshape(depth, bs, t, H_A, DVA).astype(cdt) for a in a_caches_s]
    return (xp.reshape(bp, s, d).astype(x_prompt.dtype), xs.reshape(bs, t, d).astype(x_sample.dtype),
            ak_p, av_p, bk_p, bv_p, ak_s, av_s, bk_s, bv_s)
```

```python
import functools
import math

import numpy as np
import jax
import jax.numpy as jnp
from jax import lax
from jax.experimental import pallas as pl
from jax.experimental.pallas import tpu as pltpu

F32 = jnp.float32
BF16 = jnp.bfloat16

D_MODEL = 1024
CHUNK = 64
H_A = 8
DA = 64
DVA = 2 * DA
H_B = 8
DB = 64
BAND_CHUNKS = 8
BAND_PAST = BAND_CHUNKS * CHUNK
REL_CLIP = 128
D_FF = 4 * D_MODEL
EPS = 1e-6
NEG_INF = -1e30
W_A = H_A * 2 * DA
W_B = H_B * DB

LANES = 128
NORM_CHUNK = 256
VMEM_LIMIT = 56 * 1024 * 1024

PROJ_ROWS = 256
ATT_Q_BLOCK = 1024
LOG2E = 1.4426950408889634
N_BIAS_COLS = 3
SUM_ROWS = 16
LANE_CHUNKS = 8
BAND_BLOCK = 1024
BAND_TILE = 256


def _rms_rows(x, gain_row):
    return x * lax.rsqrt(jnp.mean(x * x, axis=-1, keepdims=True) + EPS) * gain_row


def _dot(a, b):
    return jnp.dot(a, b, preferred_element_type=F32)


def _dot_nt(a, b):
    return lax.dot_general(a, b, (((1,), (1,)), ((), ())), preferred_element_type=F32)


def _head_norm(z, group_ones, gain_row):
    outs = []
    for c in range(z.shape[1] // NORM_CHUNK):
        zc = z[:, c * NORM_CHUNK:(c + 1) * NORM_CHUNK]
        ss = _dot((zc * zc).astype(BF16), group_ones)
        outs.append(zc * lax.rsqrt(ss * (1.0 / DA) + EPS))
    return jnp.concatenate(outs, axis=1) * gain_row


def _proj_kernel(x_ref, g1_ref, w_ref, ones_ref, gqa_ref, gka_ref, gqb_ref, gkb_ref, *rest, slab):
    (qa_ref, kaf_ref, kab_ref, vaf_ref, vab_ref,
     qb_ref, kbf_ref, kbb_ref, vbf_ref, vbb_ref) = rest[-10:]
    hb = _rms_rows(x_ref[...], g1_ref[...]).astype(BF16)
    ones = ones_ref[...]

    def store_heads(dst_ref, z):
        for d in range(dst_ref.shape[0]):
            if d == slab:
                dst_ref[d] = z.reshape(z.shape[0], H_A, DVA)
            else:
                dst_ref[d] = jnp.zeros(dst_ref.shape[1:], F32)

    qa = _head_norm(_dot(hb, w_ref[:, 0:W_A]), ones, gqa_ref[...])
    qa_ref[...] = qa.astype(BF16)
    ka = _head_norm(_dot(hb, w_ref[:, W_A:2 * W_A]), ones, gka_ref[...])
    def store_head_major(dst_ref, z):
        for hd in range(H_A):
            dst_ref[hd] = z[:, hd * DVA:(hd + 1) * DVA].astype(BF16)

    store_heads(kaf_ref, ka)
    store_head_major(kab_ref, ka)
    va = _dot(hb, w_ref[:, 2 * W_A:3 * W_A])
    store_heads(vaf_ref, va)
    store_head_major(vab_ref, va)

    o = 3 * W_A
    qb = _head_norm(_dot(hb, w_ref[:, o:o + W_B]), ones, gqb_ref[...])
    qb_ref[...] = qb.astype(BF16)
    kb = _head_norm(_dot(hb, w_ref[:, o + W_B:o + 2 * W_B]), ones, gkb_ref[...])
    kbf_ref[...] = kb
    kbb_ref[...] = kb.astype(BF16)
    vb = _dot(hb, w_ref[:, o + 2 * W_B:o + 3 * W_B])
    vbf_ref[...] = vb
    vbb_ref[...] = vb.astype(BF16)


def _const_spec(shape):
    return pl.BlockSpec(shape, lambda *_: (0,) * len(shape), pipeline_mode=pl.Buffered(1))


def _projection(x2d, g1, w_bf, ones, gqa, gka, gqb, gkb, layer, depth, caches):
    n = x2d.shape[0]
    tm = min(PROJ_ROWS, n)
    assert n % tm == 0
    row = lambda width: pl.BlockSpec((tm, width), lambda i: (i, 0))
    if caches:
        slab, slab_in_block = pl.BlockSpec((1, tm, H_A, DVA), lambda i: (layer, i, 0, 0)), 0
    else:
        slab, slab_in_block = pl.BlockSpec((depth, tm, H_A, DVA), lambda i: (0, i, 0, 0)), layer
    out_shapes = []
    out_specs = []
    for width, dtypes in ((W_A, (BF16,)), (W_A, (F32, BF16)), (W_A, (F32, BF16)),
                          (W_B, (BF16,)), (W_B, (F32, BF16)), (W_B, (F32, BF16))):
        for dt in dtypes:
            if width == W_A and dt == F32:
                out_shapes.append(jax.ShapeDtypeStruct((depth, n, H_A, DVA), F32))
                out_specs.append(slab)
            elif width == W_A and len(dtypes) == 2:
                out_shapes.append(jax.ShapeDtypeStruct((H_A, n, DVA), BF16))
                out_specs.append(pl.BlockSpec((H_A, tm, DVA), lambda i: (0, i, 0)))
            else:
                out_shapes.append(jax.ShapeDtypeStruct((n, width), dt))
                out_specs.append(row(width))
    n_in = 8
    return pl.pallas_call(
        functools.partial(_proj_kernel, slab=slab_in_block),
        grid=(n // tm,),
        in_specs=[row(D_MODEL), _const_spec((1, D_MODEL)), _const_spec(w_bf.shape),
                  _const_spec(ones.shape), _const_spec((1, W_A)), _const_spec((1, W_A)),
                  _const_spec((1, W_B)), _const_spec((1, W_B))]
                 + [pl.BlockSpec(memory_space=pl.ANY)] * len(caches),
        out_specs=out_specs,
        out_shape=out_shapes,
        input_output_aliases={n_in: 1, n_in + 1: 3} if caches else {},
        compiler_params=pltpu.CompilerParams(dimension_semantics=("parallel",),
                                             vmem_limit_bytes=VMEM_LIMIT),
        name="projection",
    )(x2d, g1, w_bf, ones, gqa, gka, gqb, gkb, *caches)


def _split_maps(q):
    lane = lax.broadcasted_iota(jnp.int32, q.shape, 1)
    zero = jnp.zeros_like(q)
    return jnp.concatenate([jnp.where(lane < DA, q, zero), jnp.where(lane >= DA, q, zero)], axis=0)


def _lambda(lamv, lam_init):
    a = jnp.sum(lamv[0:1, :] * lamv[1:2, :], axis=1, keepdims=True)
    b = jnp.sum(lamv[2:3, :] * lamv[3:4, :], axis=1, keepdims=True)
    return jnp.exp(a) - jnp.exp(b) + lam_init


def _dattn_kernel(slopes_ref, lamv_ref, q_ref, k_ref, v_ref, o_ref,
                  qz_sc, kb_sc, db_sc, sa_sc, sb_sc, ma_sc, mb_sc, m_sc, acc_sc, *, qblk, kblk, lam_init):
    h = pl.program_id(1)
    i = pl.program_id(2)
    own = qblk // kblk
    sl2 = slopes_ref[h] * LOG2E

    @pl.when(i == 0)
    def _head_tables():
        qz_sc[LANES:2 * LANES, 0:2 * qblk] = (
            lax.broadcasted_iota(jnp.int32, (LANES, 2 * qblk), 0) < N_BIAS_COLS).astype(BF16)
        r = lax.broadcasted_iota(jnp.int32, (kblk, LANES), 0)
        c = lax.broadcasted_iota(jnp.int32, (kblk, LANES), 1)
        rest = sl2 * r.astype(F32)
        kb = jnp.zeros((kblk, LANES), F32)
        for col in range(N_BIAS_COLS):
            part = rest.astype(BF16).astype(F32)
            kb = jnp.where(c == col, part, kb)
            rest = rest - part
        kb_sc[...] = kb.astype(BF16)
        ki = lax.broadcasted_iota(jnp.int32, (qblk, qblk), 0)
        qi = lax.broadcasted_iota(jnp.int32, (qblk, qblk), 1)
        db_sc[:, 0:qblk] = jnp.where((ki // CHUNK) <= (qi // CHUNK),
                                     sl2 * (qi - jnp.abs(qi - ki) - ki).astype(F32), NEG_INF)

    qt = q_ref[0].T
    row = lax.broadcasted_iota(jnp.int32, qt.shape, 0)
    zero = jnp.zeros_like(qt)
    qz_sc[0:LANES, 0:qblk] = jnp.where(row < DA, qt, zero)
    qz_sc[0:LANES, qblk:2 * qblk] = jnp.where(row >= DA, qt, zero)

    m_sc[...] = jnp.full(m_sc.shape, -jnp.inf, F32)
    acc_sc[...] = jnp.zeros(acc_sc.shape, F32)
    ones_rows = jnp.ones((SUM_ROWS, kblk), BF16)

    first_own = own * i

    def rows(j):
        return pl.ds(pl.multiple_of(j * kblk, kblk), kblk)

    def scores(j, dst, max_dst):
        s = _dot(jnp.concatenate([k_ref[0, 0, rows(j), :], kb_sc[...]], axis=1), qz_sc[:, 0:2 * qblk])
        dst[:, 0:2 * qblk] = s
        max_dst[...] = jnp.max(s, axis=0, keepdims=True)

    def softmax_pv(j, s, s_max):
        off = sl2 * (j * kblk - i * qblk).astype(F32)
        m_prev = m_sc[...] - off
        m_new = jnp.maximum(m_prev, s_max)
        alpha = jnp.exp2(m_prev - m_new)
        pb = jnp.exp2(s - m_new).astype(BF16)
        vt = jnp.concatenate([v_ref[0, 0, rows(j), :].T, ones_rows], axis=0)
        pv = jnp.concatenate([_dot(vt, pb[:, 0:qblk]), _dot(vt, pb[:, qblk:2 * qblk])], axis=1)
        acc_sc[:, 0:2 * qblk] = alpha * acc_sc[:, 0:2 * qblk] + pv
        m_sc[...] = m_new + off

    chunk = 2 * qblk // LANE_CHUNKS

    def own_chunks(d):
        return 2 * (qblk - d * kblk) // chunk

    def own_lanes(d, cc):
        per_map = own_chunks(d) // 2
        first_query = d * kblk + (cc % per_map) * chunk
        first_lane = (cc // per_map) * qblk + first_query
        return (slice(cc * chunk, (cc + 1) * chunk), slice(first_lane, first_lane + chunk),
                slice(first_query, first_query + chunk))

    def own_scores(d, dst, cc):
        packed, lanes, _ = own_lanes(d, cc)
        lhs = jnp.concatenate([k_ref[0, 0, rows(first_own + d), :], kb_sc[...]], axis=1)
        dst[:, packed] = _dot(lhs, qz_sc[:, lanes])

    def own_softmax_pv(d, src, cc):
        packed, lanes, queries = own_lanes(d, cc)
        s = src[:, packed] + db_sc[d * kblk:(d + 1) * kblk, queries]
        off = sl2 * jnp.float32(d * kblk)
        m_prev = m_sc[:, lanes] - off
        m_new = jnp.maximum(m_prev, jnp.max(s, axis=0, keepdims=True))
        alpha = jnp.exp2(m_prev - m_new)
        pb = jnp.exp2(s - m_new).astype(BF16)
        vt = jnp.concatenate([v_ref[0, 0, rows(first_own + d), :].T, ones_rows], axis=0)
        acc_sc[:, lanes] = alpha * acc_sc[:, lanes] + _dot(vt, pb)
        m_sc[:, lanes] = m_new + off

    scores(0, sa_sc, ma_sc)

    def stage(j_next, next_s, next_max, j_cur, cur_s, cur_max):
        lhs = jnp.concatenate([k_ref[0, 0, rows(j_next), :], kb_sc[...]], axis=1)
        vt = jnp.concatenate([v_ref[0, 0, rows(j_cur), :].T, ones_rows], axis=0)
        off = sl2 * (j_cur * kblk - i * qblk).astype(F32)
        width = 2 * qblk // LANE_CHUNKS
        for c in range(LANE_CHUNKS):
            lanes = slice(c * width, (c + 1) * width)
            s = _dot(lhs, qz_sc[:, lanes])
            next_s[:, lanes] = s
            next_max[:, lanes] = jnp.max(s, axis=0, keepdims=True)
            m_prev = m_sc[:, lanes] - off
            m_new = jnp.maximum(m_prev, cur_max[:, lanes])
            alpha = jnp.exp2(m_prev - m_new)
            pb = jnp.exp2(cur_s[:, lanes] - m_new).astype(BF16)
            acc_sc[:, lanes] = alpha * acc_sc[:, lanes] + _dot(vt, pb)
            m_sc[:, lanes] = m_new + off

    def pair(j0):
        stage(j0 + 1, sb_sc, mb_sc, j0, sa_sc, ma_sc)
        stage(j0 + 2, sa_sc, ma_sc, j0 + 1, sb_sc, mb_sc)

    def two_pairs(qq, carry):
        pair(4 * qq)
        pair(4 * qq + 2)
        return carry

    n_pairs = first_own // 2
    lax.fori_loop(0, n_pairs // 2, two_pairs, 0)

    @pl.when(n_pairs % 2 == 1)
    def _last_pair():
        pair(first_own - 2)

    bufs = (sa_sc, sb_sc)
    for d in range(own):
        n_soft = own_chunks(d)
        n_next = own_chunks(d + 1) if d + 1 < own else 0
        issued = 0
        for cc in range(n_soft):
            while issued < n_next and issued * n_soft <= cc * n_next:
                own_scores(d + 1, bufs[(d + 1) % 2], issued)
                issued += 1
            own_softmax_pv(d, bufs[d % 2], cc)

    inv = 1.0 / acc_sc[DVA:DVA + 1, 0:2 * qblk]
    lam = _lambda(lamv_ref[...], lam_init)
    o_ref[0] = (acc_sc[0:DVA, 0:qblk] * inv[:, 0:qblk]
                - acc_sc[0:DVA, qblk:2 * qblk] * (lam * inv[:, qblk:2 * qblk])).T


def _diff_attention_prompt(qa, kab, vab, slopes, lamv, lam_init):
    b, s, _ = qa.shape
    qblk = min(ATT_Q_BLOCK, s)
    kblk = qblk // 2
    assert s % qblk == 0 and kblk % CHUNK == 0
    head_rows = pl.BlockSpec((1, 1, s, DVA), lambda bb, h, i: (h, bb, 0, 0))
    block = pl.BlockSpec((1, qblk, DVA), lambda bb, h, i: (bb, i, h))
    return pl.pallas_call(
        functools.partial(_dattn_kernel, qblk=qblk, kblk=kblk, lam_init=lam_init),
        grid=(b, H_A, s // qblk),
        in_specs=[pl.BlockSpec(memory_space=pltpu.SMEM), pl.BlockSpec((4, DA), lambda bb, h, i: (0, 0)),
                  block, head_rows, head_rows],
        out_specs=block,
        out_shape=jax.ShapeDtypeStruct((b, s, W_A), F32),
        scratch_shapes=[pltpu.VMEM((2 * LANES, 2 * qblk + LANES), BF16), pltpu.VMEM((kblk, LANES), BF16),
                        pltpu.VMEM((qblk, qblk + LANES), F32),
                        pltpu.VMEM((kblk, 2 * qblk + LANES), F32), pltpu.VMEM((kblk, 2 * qblk + LANES), F32),
                        pltpu.VMEM((1, 2 * qblk), F32), pltpu.VMEM((1, 2 * qblk), F32),
                        pltpu.VMEM((1, 2 * qblk), F32),
                        pltpu.VMEM((DVA + SUM_ROWS, 2 * qblk + LANES), F32)],
        compiler_params=pltpu.CompilerParams(
            dimension_semantics=("parallel", "parallel", "arbitrary"), vmem_limit_bytes=VMEM_LIMIT),
        name="diff_attention",
    )(slopes, lamv, qa, kab, vab)


def _dattn_sample_kernel(slopes_ref, lamv_ref, q_ref, kc_ref, kn_ref, vc_ref, vn_ref, o_ref,
                         kpad_sc, vpad_sc, *, t, p_len, lam_init):
    kpad_sc[...] = jnp.zeros(kpad_sc.shape, BF16)
    vpad_sc[...] = jnp.zeros(vpad_sc.shape, BF16)
    for hd in range(H_A):
        kpad_sc[0:t, hd * DVA:(hd + 1) * DVA] = kn_ref[hd, 0]
        vpad_sc[0:t, hd * DVA:(hd + 1) * DVA] = vn_ref[hd, 0]
    lam = _lambda(lamv_ref[...], lam_init)

    def distance(k0, nk, n_valid):
        qpos = lax.broadcasted_iota(jnp.int32, (t, nk), 0) + p_len
        kidx = lax.broadcasted_iota(jnp.int32, (t, nk), 1)
        kpos = kidx + k0
        visible = ((kpos // CHUNK) <= (qpos // CHUNK)) & (kidx < n_valid)
        return visible, jnp.abs(qpos - kpos).astype(F32)

    vis_c, dist_c = distance(0, p_len, p_len)
    vis_n, dist_n = distance(p_len, LANES, t)
    kc_all = kc_ref[0, 0].reshape(p_len, W_A).astype(BF16)
    vc_all = vc_ref[0, 0].reshape(p_len, W_A).astype(BF16)
    outs = []
    for hd in range(H_A):
        cols = slice(hd * DVA, (hd + 1) * DVA)
        slope = slopes_ref[hd] * LOG2E
        qz = _split_maps(q_ref[0, :, cols])

        def scores(k, vis, dist):
            s = _dot_nt(qz, k)
            bias_t = jnp.where(vis, -slope * dist, NEG_INF)
            return (s.reshape(2, t, -1) + bias_t[None]).reshape(2 * t, -1)

        s_c = scores(kc_all[:, cols], vis_c, dist_c)
        s_n = scores(kpad_sc[:, cols], vis_n, dist_n)
        m = jnp.maximum(jnp.max(s_c, axis=1, keepdims=True), jnp.max(s_n, axis=1, keepdims=True))
        p_c = jnp.exp2(s_c - m)
        p_n = jnp.exp2(s_n - m)
        l = jnp.sum(p_c, axis=1, keepdims=True) + jnp.sum(p_n, axis=1, keepdims=True)
        out = (_dot(p_c.astype(BF16), vc_all[:, cols]) + _dot(p_n.astype(BF16), vpad_sc[:, cols])) / l
        outs.append(out[0:t] - lam * out[t:2 * t])
    o_ref[0] = jnp.concatenate(outs, axis=1)


def _diff_attention_sample(qa, kab, vab, cache_k, cache_v, layer, slopes, lamv, lam_init):
    b, t, _ = qa.shape
    p_len = cache_k.shape[2]
    assert t <= LANES
    new = pl.BlockSpec((1, t, W_A), lambda bb: (bb, 0, 0))
    new_kv = pl.BlockSpec((H_A, 1, t, DVA), lambda bb: (0, bb, 0, 0))
    old = pl.BlockSpec((1, 1, p_len, H_A, DVA), lambda bb: (layer, bb, 0, 0, 0))
    return pl.pallas_call(
        functools.partial(_dattn_sample_kernel, t=t, p_len=p_len, lam_init=lam_init),
        grid=(b,),
        in_specs=[pl.BlockSpec(memory_space=pltpu.SMEM), pl.BlockSpec((4, DA), lambda bb: (0, 0)),
                  new, old, new_kv, old, new_kv],
        out_specs=new,
        out_shape=jax.ShapeDtypeStruct((b, t, W_A), F32),
        scratch_shapes=[pltpu.VMEM((LANES, W_A), BF16), pltpu.VMEM((LANES, W_A), BF16)],
        compiler_params=pltpu.CompilerParams(dimension_semantics=("parallel",),
                                             vmem_limit_bytes=VMEM_LIMIT),
        name="diff_attention_sample",
    )(slopes, lamv, qa, cache_k, kab, cache_v, vab)


def _pair_masks(q):
    lane = lax.broadcasted_iota(jnp.int32, q.shape, 1)
    zero = jnp.zeros_like(q)
    return lane, (jnp.where(lane < DB, q, zero), jnp.where(lane >= DB, q, zero))


def _battn_kernel(q_ref, kp_ref, kc_ref, vp_ref, vc_ref, bias_ref, o_ref,
                  sa_sc, sb_sc, sc_sc, sd_sc, *, blk):
    i = pl.program_id(2)
    win = BAND_PAST + BAND_TILE
    qt = q_ref[0].T
    row = lax.broadcasted_iota(jnp.int32, qt.shape, 0)
    zero = jnp.zeros_like(qt)
    qz = (jnp.where(row < DB, qt, zero), jnp.where(row >= DB, qt, zero))
    ones_rows = jnp.ones((SUM_ROWS, win), BF16)
    key_row = lax.broadcasted_iota(jnp.int32, (win, BAND_TILE), 0)
    out_row = lax.broadcasted_iota(jnp.int32, (LANES, BAND_TILE), 0)

    def window(prev_ref, cur_ref, lo):
        cur = cur_ref[0, max(lo - BAND_PAST, 0):lo + BAND_TILE, :]
        return jnp.concatenate([prev_ref[0, lo:BAND_PAST, :], cur], axis=0) if lo < BAND_PAST else cur

    def scores(item, dst):
        lo, e = item
        dst[:, 0:BAND_TILE] = _dot(window(kp_ref, kc_ref, lo), qz[e][:, lo:lo + BAND_TILE])

    def softmax_pv(item, src):
        lo, e = item
        s = src[:, 0:BAND_TILE] + bias_ref[e, :, 0:BAND_TILE]
        n_prev = BAND_PAST - lo
        if n_prev > 0:
            s = jnp.where((i > 0) | (key_row >= n_prev), s, NEG_INF)
        pb = jnp.exp2(s - jnp.max(s, axis=0, keepdims=True)).astype(BF16)
        vt = jnp.concatenate([window(vp_ref, vc_ref, lo).T, ones_rows], axis=0)
        pv = _dot(vt, pb)
        return pv[0:LANES] / pv[LANES:LANES + 1]

    n_tiles = blk // BAND_TILE
    bufs = ((sa_sc, sb_sc), (sc_sc, sd_sc))
    tiles = []
    for e in range(2):
        scores((0, e), bufs[0][e])
    for nt in range(n_tiles):
        if nt + 1 < n_tiles:
            for e in range(2):
                scores(((nt + 1) * BAND_TILE, e), bufs[(nt + 1) % 2][e])
        outs = [softmax_pv((nt * BAND_TILE, e), bufs[nt % 2][e]) for e in range(2)]
        tiles.append(jnp.where(out_row < DB, outs[0], outs[1]))
    o_ref[0] = jnp.concatenate(tiles, axis=1).T


def _band_attention_prompt(qb, kbb, vbb, bias_t):
    b, s, _ = qb.shape
    blk = min(BAND_BLOCK, s)
    ratio = blk // BAND_PAST
    assert s % blk == 0 and blk % BAND_PAST == 0 and BAND_PAST % BAND_TILE == 0
    cur = pl.BlockSpec((1, blk, LANES), lambda pr, bb, i: (bb, i, pr))
    prev = pl.BlockSpec((1, BAND_PAST, LANES),
                        lambda pr, bb, i: (bb, jnp.maximum(i * ratio - 1, 0), pr))
    return pl.pallas_call(
        functools.partial(_battn_kernel, blk=blk),
        grid=(H_B // 2, b, s // blk),
        in_specs=[cur, prev, cur, prev, cur,
                  pl.BlockSpec((2,) + bias_t.shape[1:], lambda pr, bb, i: (pr, 0, 0))],
        out_specs=cur,
        out_shape=jax.ShapeDtypeStruct((b, s, W_B), F32),
        scratch_shapes=[pltpu.VMEM((bias_t.shape[1], BAND_TILE + LANES), F32)] * 4,
        compiler_params=pltpu.CompilerParams(
            dimension_semantics=("parallel", "parallel", "parallel"), vmem_limit_bytes=VMEM_LIMIT),
        name="band_attention",
    )(qb, kbb, kbb, vbb, vbb, bias_t)


def _battn_sample_kernel(q_ref, kc_ref, kn_ref, vc_ref, vn_ref, biasc_ref, biasn_ref, o_ref,
                         kpad_sc, vpad_sc, *, t):
    kpad_sc[...] = jnp.zeros(kpad_sc.shape, BF16)
    vpad_sc[...] = jnp.zeros(vpad_sc.shape, BF16)
    kpad_sc[0:t, :] = kn_ref[0]
    vpad_sc[0:t, :] = vn_ref[0]
    pieces = []
    for pr in range(H_B // 2):
        cols = slice(pr * LANES, (pr + 1) * LANES)
        lane, qz = _pair_masks(q_ref[0, :, cols])
        kc = kc_ref[0, 0, :, cols].astype(BF16)
        vc = vc_ref[0, 0, :, cols].astype(BF16)
        kn = kpad_sc[:, cols]
        vn = vpad_sc[:, cols]
        outs = []
        for e in range(2):
            hd = 2 * pr + e
            s_c = _dot_nt(qz[e], kc) + biasc_ref[hd]
            s_n = _dot_nt(qz[e], kn) + biasn_ref[hd]
            m = jnp.maximum(jnp.max(s_c, axis=1, keepdims=True), jnp.max(s_n, axis=1, keepdims=True))
            p_c = jnp.exp2(s_c - m)
            p_n = jnp.exp2(s_n - m)
            l = jnp.sum(p_c, axis=1, keepdims=True) + jnp.sum(p_n, axis=1, keepdims=True)
            outs.append((_dot(p_c.astype(BF16), vc) + _dot(p_n.astype(BF16), vn)) / l)
        pieces.append(jnp.where(lane < DB, outs[0], outs[1]))
    o_ref[0] = jnp.concatenate(pieces, axis=1)


def _band_attention_sample(qb, kbb, vbb, cache_k, cache_v, layer, bias_c, bias_n):
    b, t, _ = qb.shape
    keep = cache_k.shape[2]
    new = pl.BlockSpec((1, t, W_B), lambda bb: (bb, 0, 0))
    old = pl.BlockSpec((1, 1, keep, W_B), lambda bb: (layer, bb, 0, 0))
    return pl.pallas_call(
        functools.partial(_battn_sample_kernel, t=t),
        grid=(b,),
        in_specs=[new, old, new, old, new, _const_spec(bias_c.shape), _const_spec(bias_n.shape)],
        out_specs=new,
        out_shape=jax.ShapeDtypeStruct((b, t, W_B), F32),
        scratch_shapes=[pltpu.VMEM((LANES, W_B), BF16), pltpu.VMEM((LANES, W_B), BF16)],
        compiler_params=pltpu.CompilerParams(dimension_semantics=("parallel",),
                                             vmem_limit_bytes=VMEM_LIMIT),
        name="band_attention_sample",
    )(qb, cache_k, kbb, cache_v, vbb, bias_c, bias_n)


def _band_bias(rel_bias, q0, nq, k0, nk, n_valid):
    qpos = q0 + np.arange(nq)[:, None]
    kidx = np.arange(nk)[None, :]
    kpos = k0 + kidx
    dchunk = qpos // CHUNK - kpos // CHUNK
    visible = (kpos >= 0) & (dchunk >= 0) & (dchunk <= BAND_CHUNKS) & (kidx < n_valid)
    dist = np.arange(q0 - k0 - (nk - 1), q0 - k0 + nq)
    vals = rel_bias.astype(F32)[:, np.clip(dist, -REL_CLIP, REL_CLIP) + REL_CLIP]
    w_pad = jnp.pad(vals[:, ::-1], ((0, 0), (0, 1)))
    width = nq + nk - 1
    skew = jnp.tile(w_pad, (1, nq))[:, :nq * width].reshape(-1, nq, width)
    return jnp.where(jnp.asarray(visible)[None], skew[:, :, nq - 1:nq - 1 + nk], NEG_INF)


def _merge_ffn_kernel(x_ref, oa_ref, ob_ref, g1_ref, gs_ref, wa_ref, wb_ref, wg_ref, wo_ref,
                      g2_ref, w1_ref, w2_ref, y_ref):
    x = x_ref[...]
    hb = _rms_rows(x, g1_ref[...]).astype(BF16)
    oa = oa_ref[...]
    heads = [_rms_rows(oa[:, c * DVA:(c + 1) * DVA], 1.0) for c in range(H_A)]
    oan = (jnp.concatenate(heads, axis=1) * gs_ref[...]).astype(BF16)
    ya = _dot(oan, wa_ref[...])
    yb = _dot(ob_ref[...].astype(BF16), wb_ref[...])
    gates = jax.nn.sigmoid(_dot(hb, wg_ref[...]))
    mix = gates[:, 0:D_MODEL] * ya + gates[:, D_MODEL:2 * D_MODEL] * yb
    x1 = x + _dot(mix.astype(BF16), wo_ref[...])
    h2 = _rms_rows(x1, g2_ref[...]).astype(BF16)
    u = jnp.maximum(_dot(h2, w1_ref[...]), 0.0)
    y_ref[...] = x1 + _dot((u * u).astype(BF16), w2_ref[...])


def _merge_ffn(x2d, oa2d, ob2d, g1, gs, wa, wb, wg, wo, g2, w1, w2):
    n = x2d.shape[0]
    tm = min(PROJ_ROWS, n)
    assert n % tm == 0
    row = lambda width: pl.BlockSpec((tm, width), lambda i: (i, 0))
    consts = (g1, gs, wa, wb, wg, wo, g2, w1, w2)
    return pl.pallas_call(
        _merge_ffn_kernel,
        grid=(n // tm,),
        in_specs=[row(D_MODEL), row(W_A), row(W_B)] + [_const_spec(c.shape) for c in consts],
        out_specs=row(D_MODEL),
        out_shape=jax.ShapeDtypeStruct((n, D_MODEL), F32),
        compiler_params=pltpu.CompilerParams(dimension_semantics=("parallel",),
                                             vmem_limit_bytes=VMEM_LIMIT),
        name="merge_ffn",
    )(x2d, oa2d, ob2d, *consts)


def kernel(x_prompt, x_sample, cache_a_k, cache_a_v, cache_b_k, cache_b_v, norm1_g, w_in, qn_a_g, kn_a_g, qn_b_g, kn_b_g, lam_q1, lam_k1, lam_q2, lam_k2, subln_a_g, rel_bias_b, w_br_a, w_br_b, w_gate, w_out, norm2_g, w_ff1, w_ff2):
    bp, s, d = x_prompt.shape
    bs, t, _ = x_sample.shape
    depth = w_in.shape[0]
    p_len = cache_a_k.shape[2]
    b_keep = cache_b_k.shape[2]
    keep_p = min(BAND_PAST, s)
    cdt = cache_a_k.dtype

    slopes = jnp.exp2(-8.0 * jnp.arange(1, H_A + 1, dtype=F32) / H_A)
    group = np.arange(NORM_CHUNK) // DA
    ones = jnp.asarray(group[:, None] == group[None, :], BF16)
    row = lambda v, reps, scale=1.0: (jnp.tile(v.astype(F32), reps) * scale)[None, :]
    cache_bk = cache_b_k.reshape(depth, bs, b_keep, W_B)
    cache_bv = cache_b_v.reshape(depth, bs, b_keep, W_B)

    xp = x_prompt.astype(F32).reshape(bp * s, d)
    xs = x_sample.astype(F32).reshape(bs * t, d)
    a_caches_p = ()
    a_caches_s = ()
    b_outs = [[] for _ in range(4)]
    for l in range(depth):
        lam_init = 0.8 - 0.6 * math.exp(-0.3 * l)
        lamv = jnp.stack([lam_q1[l], lam_k1[l], lam_q2[l], lam_k2[l]]).astype(F32)
        g1 = row(norm1_g[l], 1)
        proj_consts = (g1, w_in[l].astype(BF16), ones,
                       row(qn_a_g[l], 2 * H_A, DA ** -0.5 * LOG2E), row(kn_a_g[l], 2 * H_A),
                       row(qn_b_g[l], H_B, DB ** -0.5 * LOG2E), row(kn_b_g[l], H_B))
        rel_l2 = rel_bias_b[l].astype(F32) * LOG2E
        tail = (g1, row(subln_a_g[l], H_A, 1.0 - lam_init), w_br_a[l].astype(BF16),
                w_br_b[l].astype(BF16), w_gate[l].astype(BF16), w_out[l].astype(BF16),
                row(norm2_g[l], 1), w_ff1[l].astype(BF16), w_ff2[l].astype(BF16))

        qa, kaf, kab, vaf, vab, qb, kbf, kbb, vbf, vbb = _projection(xp, *proj_consts, l, depth, a_caches_p)
        a_caches_p = (kaf, vaf)
        r3 = lambda a, n: a.reshape(n, -1, a.shape[-1])
        by_head = lambda a, n: a.reshape(H_A, n, -1, DVA)
        oa = _diff_attention_prompt(r3(qa, bp), by_head(kab, bp), by_head(vab, bp), slopes, lamv, lam_init)
        win = BAND_PAST + BAND_TILE
        bias_t = _band_bias(rel_l2, BAND_PAST, BAND_TILE, 0, win, win).transpose(0, 2, 1)
        bias_t = jnp.pad(bias_t, ((0, 0), (0, 0), (0, LANES)))
        ob = _band_attention_prompt(r3(qb, bp), r3(kbb, bp), r3(vbb, bp), bias_t)
        xp = _merge_ffn(xp, oa.reshape(bp * s, W_A), ob.reshape(bp * s, W_B), *tail)
        b_outs[0].append(r3(kbf, bp)[:, s - keep_p:].reshape(bp, keep_p, H_B, DB))
        b_outs[1].append(r3(vbf, bp)[:, s - keep_p:].reshape(bp, keep_p, H_B, DB))

        qa, kaf, kab, vaf, vab, qb, kbf, kbb, vbf, vbb = _projection(xs, *proj_consts, l, depth, a_caches_s)
        a_caches_s = (kaf, vaf)
        oa = _diff_attention_sample(r3(qa, bs), by_head(kab, bs), by_head(vab, bs), cache_a_k, cache_a_v, l,
                                    slopes, lamv, lam_init)
        bias_c = _band_bias(rel_l2, p_len, t, p_len - b_keep, b_keep, b_keep)
        bias_n = _band_bias(rel_l2, p_len, t, p_len, LANES, t)
        ob = _band_attention_sample(r3(qb, bs), r3(kbb, bs), r3(vbb, bs), cache_bk, cache_bv, l,
                                    bias_c, bias_n)
        xs = _merge_ffn(xs, oa.reshape(bs * t, W_A), ob.reshape(bs * t, W_B), *tail)
        b_outs[2].append(kbf.reshape(bs, t, H_B, DB))
        b_outs[3].append(vbf.reshape(bs, t, H_B, DB))

    bk_p, bv_p, bk_s, bv_s = [jnp.stack(o).astype(cdt) for o in b_outs]
    ak_p, av_p = [a.reshape(depth, bp, s, H_A, DVA).astype(cdt) for a in a_caches_p]
    ak_s, av_s = [a.reshape(depth, bs, t, H_A, DVA).astype(cdt) for a in a_caches_s]
    return (xp.reshape(bp, s, d).astype(x_prompt.dtype), xs.reshape(bs, t, d).astype(x_sample.dtype),
            ak_p, av_p, bk_p, bv_p, ak_s, av_s, bk_s, bv_s)
```

```python
import functools
import math

import numpy as np
import jax
import jax.numpy as jnp
from jax import lax
from jax.experimental import pallas as pl
from jax.experimental.pallas import tpu as pltpu

F32 = jnp.float32
BF16 = jnp.bfloat16

D_MODEL = 1024
CHUNK = 64
H_A = 8
DA = 64
DVA = 2 * DA
H_B = 8
DB = 64
BAND_CHUNKS = 8
BAND_PAST = BAND_CHUNKS * CHUNK
REL_CLIP = 128
EPS = 1e-6
NEG_INF = -1e30
W_A = H_A * 2 * DA
W_B = H_B * DB

LANES = 128
NORM_CHUNK = 256
VMEM_LIMIT = 56 * 1024 * 1024

PROJ_ROWS = 256
ATT_Q_BLOCK = 1024
LOG2E = 1.4426950408889634
N_BIAS_COLS = 3
SUM_ROWS = 16
LANE_CHUNKS = 8
BAND_BLOCK = 1024
BAND_TILE = 256


def _rms_rows(x, gain_row):
    return x * lax.rsqrt(jnp.mean(x * x, axis=-1, keepdims=True) + EPS) * gain_row


def _dot(a, b):
    return jnp.dot(a, b, preferred_element_type=F32)


def _dot_nt(a, b):
    return lax.dot_general(a, b, (((1,), (1,)), ((), ())), preferred_element_type=F32)


def _head_norm(z, group_ones, gain_row):
    outs = []
    for c in range(z.shape[1] // NORM_CHUNK):
        zc = z[:, c * NORM_CHUNK:(c + 1) * NORM_CHUNK]
        ss = _dot((zc * zc).astype(BF16), group_ones)
        outs.append(zc * lax.rsqrt(ss * (1.0 / DA) + EPS))
    return jnp.concatenate(outs, axis=1) * gain_row


def _proj_kernel(x_ref, g1_ref, w_ref, ones_ref, gqa_ref, gka_ref, gqb_ref, gkb_ref, *rest, slab):
    (qa_ref, kaf_ref, kab_ref, vaf_ref, vab_ref,
     qb_ref, kbf_ref, kbb_ref, vbf_ref, vbb_ref) = rest[-10:]
    hb = _rms_rows(x_ref[...], g1_ref[...]).astype(BF16)
    ones = ones_ref[...]

    def store_heads(dst_ref, z):
        for d in range(dst_ref.shape[0]):
            if d == slab:
                dst_ref[d] = z.reshape(z.shape[0], H_A, DVA)
            else:
                dst_ref[d] = jnp.zeros(dst_ref.shape[1:], F32)

    def store_head_major(dst_ref, z):
        for hd in range(H_A):
            dst_ref[hd] = z[:, hd * DVA:(hd + 1) * DVA].astype(BF16)

    o = 3 * W_A
    z_qa = _dot(hb, w_ref[:, 0:W_A])
    z_ka = _dot(hb, w_ref[:, W_A:2 * W_A])
    qa_ref[...] = _head_norm(z_qa, ones, gqa_ref[...]).astype(BF16)
    va = _dot(hb, w_ref[:, 2 * W_A:3 * W_A])
    ka = _head_norm(z_ka, ones, gka_ref[...])
    store_heads(kaf_ref, ka)
    store_head_major(kab_ref, ka)
    z_qb = _dot(hb, w_ref[:, o:o + W_B])
    store_heads(vaf_ref, va)
    store_head_major(vab_ref, va)
    z_kb = _dot(hb, w_ref[:, o + W_B:o + 2 * W_B])
    qb_ref[...] = _head_norm(z_qb, ones, gqb_ref[...]).astype(BF16)
    vb = _dot(hb, w_ref[:, o + 2 * W_B:o + 3 * W_B])
    kb = _head_norm(z_kb, ones, gkb_ref[...])
    kbf_ref[...] = kb
    kbb_ref[...] = kb.astype(BF16)
    vbf_ref[...] = vb
    vbb_ref[...] = vb.astype(BF16)


def _const_spec(shape):
    return pl.BlockSpec(shape, lambda *_: (0,) * len(shape), pipeline_mode=pl.Buffered(1))


def _projection(x2d, g1, w_bf, ones, gqa, gka, gqb, gkb, layer, depth, caches):
    n = x2d.shape[0]
    tm = min(PROJ_ROWS, n)
    assert n % tm == 0
    row = lambda width: pl.BlockSpec((tm, width), lambda i: (i, 0))
    if caches:
        slab, slab_in_block = pl.BlockSpec((1, tm, H_A, DVA), lambda i: (layer, i, 0, 0)), 0
    else:
        slab, slab_in_block = pl.BlockSpec((depth, tm, H_A, DVA), lambda i: (0, i, 0, 0)), layer
    out_shapes = []
    out_specs = []
    for width, dtypes in ((W_A, (BF16,)), (W_A, (F32, BF16)), (W_A, (F32, BF16)),
                          (W_B, (BF16,)), (W_B, (F32, BF16)), (W_B, (F32, BF16))):
        for dt in dtypes:
            if width == W_A and dt == F32:
                out_shapes.append(jax.ShapeDtypeStruct((depth, n, H_A, DVA), F32))
                out_specs.append(slab)
            elif width == W_A and len(dtypes) == 2:
                out_shapes.append(jax.ShapeDtypeStruct((H_A, n, DVA), BF16))
                out_specs.append(pl.BlockSpec((H_A, tm, DVA), lambda i: (0, i, 0)))
            else:
                out_shapes.append(jax.ShapeDtypeStruct((n, width), dt))
                out_specs.append(row(width))
    n_in = 8
    return pl.pallas_call(
        functools.partial(_proj_kernel, slab=slab_in_block),
        grid=(n // tm,),
        in_specs=[row(D_MODEL), _const_spec((1, D_MODEL)), _const_spec(w_bf.shape),
                  _const_spec(ones.shape), _const_spec((1, W_A)), _const_spec((1, W_A)),
                  _const_spec((1, W_B)), _const_spec((1, W_B))]
                 + [pl.BlockSpec(memory_space=pl.ANY)] * len(caches),
        out_specs=out_specs,
        out_shape=out_shapes,
        input_output_aliases={n_in: 1, n_in + 1: 3} if caches else {},
        compiler_params=pltpu.CompilerParams(dimension_semantics=("parallel",),
                                             vmem_limit_bytes=VMEM_LIMIT),
        name="projection",
    )(x2d, g1, w_bf, ones, gqa, gka, gqb, gkb, *caches)


def _split_maps(q):
    lane = lax.broadcasted_iota(jnp.int32, q.shape, 1)
    zero = jnp.zeros_like(q)
    return jnp.concatenate([jnp.where(lane < DA, q, zero), jnp.where(lane >= DA, q, zero)], axis=0)


def _lambda(lamv, lam_init):
    a = jnp.sum(lamv[0:1, :] * lamv[1:2, :], axis=1, keepdims=True)
    b = jnp.sum(lamv[2:3, :] * lamv[3:4, :], axis=1, keepdims=True)
    return jnp.exp(a) - jnp.exp(b) + lam_init


def _dattn_kernel(slopes_ref, lamv_ref, q_ref, k_ref, v_ref, o_ref,
                  qz_sc, kb_sc, db_sc, sa_sc, sb_sc, ma_sc, mb_sc, m_sc, acc_sc, *, qblk, kblk, lam_init):
    h = pl.program_id(1)
    i = pl.program_id(2)
    own = qblk // kblk
    sl2 = slopes_ref[h] * LOG2E

    @pl.when(i == 0)
    def _head_tables():
        qz_sc[LANES:2 * LANES, 0:2 * qblk] = (
            lax.broadcasted_iota(jnp.int32, (LANES, 2 * qblk), 0) < N_BIAS_COLS).astype(BF16)
        r = lax.broadcasted_iota(jnp.int32, (kblk, LANES), 0)
        c = lax.broadcasted_iota(jnp.int32, (kblk, LANES), 1)
        rest = sl2 * r.astype(F32)
        kb = jnp.zeros((kblk, LANES), F32)
        for col in range(N_BIAS_COLS):
            part = rest.astype(BF16).astype(F32)
            kb = jnp.where(c == col, part, kb)
            rest = rest - part
        kb_sc[...] = kb.astype(BF16)
        ki = lax.broadcasted_iota(jnp.int32, (qblk, qblk), 0)
        qi = lax.broadcasted_iota(jnp.int32, (qblk, qblk), 1)
        db_sc[:, 0:qblk] = jnp.where((ki // CHUNK) <= (qi // CHUNK),
                                     sl2 * (qi - jnp.abs(qi - ki) - ki).astype(F32), NEG_INF)

    qt = q_ref[0].T
    row = lax.broadcasted_iota(jnp.int32, qt.shape, 0)
    zero = jnp.zeros_like(qt)
    qz_sc[0:LANES, 0:qblk] = jnp.where(row < DA, qt, zero)
    qz_sc[0:LANES, qblk:2 * qblk] = jnp.where(row >= DA, qt, zero)

    m_sc[...] = jnp.full(m_sc.shape, -jnp.inf, F32)
    acc_sc[...] = jnp.zeros(acc_sc.shape, F32)
    ones_rows = jnp.ones((SUM_ROWS, kblk), BF16)

    first_own = own * i

    def rows(j):
        return pl.ds(pl.multiple_of(j * kblk, kblk), kblk)

    def scores(j, dst, max_dst):
        s = _dot(jnp.concatenate([k_ref[0, 0, rows(j), :], kb_sc[...]], axis=1), qz_sc[:, 0:2 * qblk])
        dst[:, 0:2 * qblk] = s
        max_dst[...] = jnp.max(s, axis=0, keepdims=True)

    chunk = 2 * qblk // LANE_CHUNKS

    def own_chunks(d):
        return 2 * (qblk - d * kblk) // chunk

    def own_lanes(d, cc):
        per_map = own_chunks(d) // 2
        first_query = d * kblk + (cc % per_map) * chunk
        first_lane = (cc // per_map) * qblk + first_query
        return (slice(cc * chunk, (cc + 1) * chunk), slice(first_lane, first_lane + chunk),
                slice(first_query, first_query + chunk))

    def own_scores(d, dst, cc):
        packed, lanes, _ = own_lanes(d, cc)
        lhs = jnp.concatenate([k_ref[0, 0, rows(first_own + d), :], kb_sc[...]], axis=1)
        dst[:, packed] = _dot(lhs, qz_sc[:, lanes])

    def own_softmax_pv(d, src, cc):
        packed, lanes, queries = own_lanes(d, cc)
        s = src[:, packed] + db_sc[d * kblk:(d + 1) * kblk, queries]
        off = sl2 * jnp.float32(d * kblk)
        m_prev = m_sc[:, lanes] - off
        m_new = jnp.maximum(m_prev, jnp.max(s, axis=0, keepdims=True))
        alpha = jnp.exp2(m_prev - m_new)
        pb = jnp.exp2(s - m_new).astype(BF16)
        vt = jnp.concatenate([v_ref[0, 0, rows(first_own + d), :].T, ones_rows], axis=0)
        acc_sc[:, lanes] = alpha * acc_sc[:, lanes] + _dot(vt, pb)
        m_sc[:, lanes] = m_new + off

    scores(0, sa_sc, ma_sc)

    def stage(j_next, next_s, next_max, j_cur, cur_s, cur_max):
        lhs = jnp.concatenate([k_ref[0, 0, rows(j_next), :], kb_sc[...]], axis=1)
        vt = jnp.concatenate([v_ref[0, 0, rows(j_cur), :].T, ones_rows], axis=0)
        off = sl2 * (j_cur * kblk - i * qblk).astype(F32)
        width = 2 * qblk // LANE_CHUNKS
        for c in range(LANE_CHUNKS):
            lanes = slice(c * width, (c + 1) * width)
            s = _dot(lhs, qz_sc[:, lanes])
            next_s[:, lanes] = s
            next_max[:, lanes] = jnp.max(s, axis=0, keepdims=True)
            m_prev = m_sc[:, lanes] - off
            m_new = jnp.maximum(m_prev, cur_max[:, lanes])
            alpha = jnp.exp2(m_prev - m_new)
            pb = jnp.exp2(cur_s[:, lanes] - m_new).astype(BF16)
            acc_sc[:, lanes] = alpha * acc_sc[:, lanes] + _dot(vt, pb)
            m_sc[:, lanes] = m_new + off

    def pair(j0):
        stage(j0 + 1, sb_sc, mb_sc, j0, sa_sc, ma_sc)
        stage(j0 + 2, sa_sc, ma_sc, j0 + 1, sb_sc, mb_sc)

    def two_pairs(qq, carry):
        pair(4 * qq)
        pair(4 * qq + 2)
        return carry

    n_pairs = first_own // 2
    lax.fori_loop(0, n_pairs // 2, two_pairs, 0)

    @pl.when(n_pairs % 2 == 1)
    def _last_pair():
        pair(first_own - 2)

    bufs = (sa_sc, sb_sc)
    for d in range(own):
        n_soft = own_chunks(d)
        n_next = own_chunks(d + 1) if d + 1 < own else 0
        issued = 0
        for cc in range(n_soft):
            while issued < n_next and issued * n_soft <= cc * n_next:
                own_scores(d + 1, bufs[(d + 1) % 2], issued)
                issued += 1
            own_softmax_pv(d, bufs[d % 2], cc)

    inv = 1.0 / acc_sc[DVA:DVA + 1, 0:2 * qblk]
    lam = _lambda(lamv_ref[...], lam_init)
    o_ref[0] = (acc_sc[0:DVA, 0:qblk] * inv[:, 0:qblk]
                - acc_sc[0:DVA, qblk:2 * qblk] * (lam * inv[:, qblk:2 * qblk])).T


def _diff_attention_prompt(qa, kab, vab, slopes, lamv, lam_init):
    b, s, _ = qa.shape
    qblk = min(ATT_Q_BLOCK, s)
    kblk = qblk // 2
    assert s % qblk == 0 and kblk % CHUNK == 0
    head_rows = pl.BlockSpec((1, 1, s, DVA), lambda bb, h, i: (h, bb, 0, 0))
    block = pl.BlockSpec((1, qblk, DVA), lambda bb, h, i: (bb, i, h))
    return pl.pallas_call(
        functools.partial(_dattn_kernel, qblk=qblk, kblk=kblk, lam_init=lam_init),
        grid=(b, H_A, s // qblk),
        in_specs=[pl.BlockSpec(memory_space=pltpu.SMEM), pl.BlockSpec((4, DA), lambda bb, h, i: (0, 0)),
                  block, head_rows, head_rows],
        out_specs=block,
        out_shape=jax.ShapeDtypeStruct((b, s, W_A), F32),
        scratch_shapes=[pltpu.VMEM((2 * LANES, 2 * qblk + LANES), BF16), pltpu.VMEM((kblk, LANES), BF16),
                        pltpu.VMEM((qblk, qblk + LANES), F32),
                        pltpu.VMEM((kblk, 2 * qblk + LANES), F32), pltpu.VMEM((kblk, 2 * qblk + LANES), F32),
                        pltpu.VMEM((1, 2 * qblk), F32), pltpu.VMEM((1, 2 * qblk), F32),
                        pltpu.VMEM((1, 2 * qblk), F32),
                        pltpu.VMEM((DVA + SUM_ROWS, 2 * qblk + LANES), F32)],
        compiler_params=pltpu.CompilerParams(
            dimension_semantics=("parallel", "parallel", "arbitrary"), vmem_limit_bytes=VMEM_LIMIT),
        name="diff_attention",
    )(slopes, lamv, qa, kab, vab)


def _dattn_sample_kernel(slopes_ref, lamv_ref, q_ref, kc_ref, kn_ref, vc_ref, vn_ref, o_ref,
                         kpad_sc, vpad_sc, *, t, p_len, lam_init):
    kpad_sc[...] = jnp.zeros(kpad_sc.shape, BF16)
    vpad_sc[...] = jnp.zeros(vpad_sc.shape, BF16)
    for hd in range(H_A):
        kpad_sc[0:t, hd * DVA:(hd + 1) * DVA] = kn_ref[hd, 0]
        vpad_sc[0:t, hd * DVA:(hd + 1) * DVA] = vn_ref[hd, 0]
    lam = _lambda(lamv_ref[...], lam_init)

    def distance(k0, nk, n_valid):
        qpos = lax.broadcasted_iota(jnp.int32, (t, nk), 0) + p_len
        kidx = lax.broadcasted_iota(jnp.int32, (t, nk), 1)
        kpos = kidx + k0
        visible = ((kpos // CHUNK) <= (qpos // CHUNK)) & (kidx < n_valid)
        return visible, jnp.abs(qpos - kpos).astype(F32)

    vis_c, dist_c = distance(0, p_len, p_len)
    vis_n, dist_n = distance(p_len, LANES, t)
    kc_all = kc_ref[0, 0].reshape(p_len, W_A).astype(BF16)
    vc_all = vc_ref[0, 0].reshape(p_len, W_A).astype(BF16)
    outs = []
    for hd in range(H_A):
        cols = slice(hd * DVA, (hd + 1) * DVA)
        slope = slopes_ref[hd] * LOG2E
        qz = _split_maps(q_ref[0, :, cols])

        def scores(k, vis, dist):
            s = _dot_nt(qz, k)
            bias_t = jnp.where(vis, -slope * dist, NEG_INF)
            return (s.reshape(2, t, -1) + bias_t[None]).reshape(2 * t, -1)

        s_c = scores(kc_all[:, cols], vis_c, dist_c)
        s_n = scores(kpad_sc[:, cols], vis_n, dist_n)
        m = jnp.maximum(jnp.max(s_c, axis=1, keepdims=True), jnp.max(s_n, axis=1, keepdims=True))
        p_c = jnp.exp2(s_c - m)
        p_n = jnp.exp2(s_n - m)
        l = jnp.sum(p_c, axis=1, keepdims=True) + jnp.sum(p_n, axis=1, keepdims=True)
        out = (_dot(p_c.astype(BF16), vc_all[:, cols]) + _dot(p_n.astype(BF16), vpad_sc[:, cols])) / l
        outs.append(out[0:t] - lam * out[t:2 * t])
    o_ref[0] = jnp.concatenate(outs, axis=1)


def _diff_attention_sample(qa, kab, vab, cache_k, cache_v, layer, slopes, lamv, lam_init):
    b, t, _ = qa.shape
    p_len = cache_k.shape[2]
    assert t <= LANES
    new = pl.BlockSpec((1, t, W_A), lambda bb: (bb, 0, 0))
    new_kv = pl.BlockSpec((H_A, 1, t, DVA), lambda bb: (0, bb, 0, 0))
    old = pl.BlockSpec((1, 1, p_len, H_A, DVA), lambda bb: (layer, bb, 0, 0, 0))
    return pl.pallas_call(
        functools.partial(_dattn_sample_kernel, t=t, p_len=p_len, lam_init=lam_init),
        grid=(b,),
        in_specs=[pl.BlockSpec(memory_space=pltpu.SMEM), pl.BlockSpec((4, DA), lambda bb: (0, 0)),
                  new, old, new_kv, old, new_kv],
        out_specs=new,
        out_shape=jax.ShapeDtypeStruct((b, t, W_A), F32),
        scratch_shapes=[pltpu.VMEM((LANES, W_A), BF16), pltpu.VMEM((LANES, W_A), BF16)],
        compiler_params=pltpu.CompilerParams(dimension_semantics=("parallel",),
                                             vmem_limit_bytes=VMEM_LIMIT),
        name="diff_attention_sample",
    )(slopes, lamv, qa, cache_k, kab, cache_v, vab)


def _pair_masks(q):
    lane = lax.broadcasted_iota(jnp.int32, q.shape, 1)
    zero = jnp.zeros_like(q)
    return lane, (jnp.where(lane < DB, q, zero), jnp.where(lane >= DB, q, zero))


def _battn_kernel(q_ref, kp_ref, kc_ref, vp_ref, vc_ref, bias_ref, o_ref,
                  sa_sc, sb_sc, sc_sc, sd_sc, *, blk):
    i = pl.program_id(2)
    win = BAND_PAST + BAND_TILE
    qt = q_ref[0].T
    row = lax.broadcasted_iota(jnp.int32, qt.shape, 0)
    zero = jnp.zeros_like(qt)
    qz = (jnp.where(row < DB, qt, zero), jnp.where(row >= DB, qt, zero))
    ones_rows = jnp.ones((SUM_ROWS, win), BF16)
    key_row = lax.broadcasted_iota(jnp.int32, (win, BAND_TILE), 0)
    out_row = lax.broadcasted_iota(jnp.int32, (LANES, BAND_TILE), 0)

    def window(prev_ref, cur_ref, lo):
        cur = cur_ref[0, max(lo - BAND_PAST, 0):lo + BAND_TILE, :]
        return jnp.concatenate([prev_ref[0, lo:BAND_PAST, :], cur], axis=0) if lo < BAND_PAST else cur

    def scores(item, dst):
        lo, e = item
        dst[:, 0:BAND_TILE] = _dot(window(kp_ref, kc_ref, lo), qz[e][:, lo:lo + BAND_TILE])

    def softmax_pv(item, src):
        lo, e = item
        s = src[:, 0:BAND_TILE] + bias_ref[e, :, 0:BAND_TILE]
        n_prev = BAND_PAST - lo
        if n_prev > 0:
            s = jnp.where((i > 0) | (key_row >= n_prev), s, NEG_INF)
        pb = jnp.exp2(s - jnp.max(s, axis=0, keepdims=True)).astype(BF16)
        vt = jnp.concatenate([window(vp_ref, vc_ref, lo).T, ones_rows], axis=0)
        pv = _dot(vt, pb)
        return pv[0:LANES] / pv[LANES:LANES + 1]

    n_tiles = blk // BAND_TILE
    bufs = ((sa_sc, sb_sc), (sc_sc, sd_sc))
    tiles = []
    for e in range(2):
        scores((0, e), bufs[0][e])
    for nt in range(n_tiles):
        if nt + 1 < n_tiles:
            for e in range(2):
                scores(((nt + 1) * BAND_TILE, e), bufs[(nt + 1) % 2][e])
        outs = [softmax_pv((nt * BAND_TILE, e), bufs[nt % 2][e]) for e in range(2)]
        tiles.append(jnp.where(out_row < DB, outs[0], outs[1]))
    o_ref[0] = jnp.concatenate(tiles, axis=1).T


def _band_attention_prompt(qb, kbb, vbb, bias_t):
    b, s, _ = qb.shape
    blk = min(BAND_BLOCK, s)
    ratio = blk // BAND_PAST
    assert s % blk == 0 and blk % BAND_PAST == 0 and BAND_PAST % BAND_TILE == 0
    cur = pl.BlockSpec((1, blk, LANES), lambda pr, bb, i: (bb, i, pr))
    prev = pl.BlockSpec((1, BAND_PAST, LANES),
                        lambda pr, bb, i: (bb, jnp.maximum(i * ratio - 1, 0), pr))
    return pl.pallas_call(
        functools.partial(_battn_kernel, blk=blk),
        grid=(H_B // 2, b, s // blk),
        in_specs=[cur, prev, cur, prev, cur,
                  pl.BlockSpec((2,) + bias_t.shape[1:], lambda pr, bb, i: (pr, 0, 0))],
        out_specs=cur,
        out_shape=jax.ShapeDtypeStruct((b, s, W_B), F32),
        scratch_shapes=[pltpu.VMEM((bias_t.shape[1], BAND_TILE + LANES), F32)] * 4,
        compiler_params=pltpu.CompilerParams(
            dimension_semantics=("parallel", "parallel", "parallel"), vmem_limit_bytes=VMEM_LIMIT),
        name="band_attention",
    )(qb, kbb, kbb, vbb, vbb, bias_t)


def _battn_sample_kernel(q_ref, kc_ref, kn_ref, vc_ref, vn_ref, biasc_ref, biasn_ref, o_ref,
                         kpad_sc, vpad_sc, *, t):
    kpad_sc[...] = jnp.zeros(kpad_sc.shape, BF16)
    vpad_sc[...] = jnp.zeros(vpad_sc.shape, BF16)
    kpad_sc[0:t, :] = kn_ref[0]
    vpad_sc[0:t, :] = vn_ref[0]
    pieces = []
    for pr in range(H_B // 2):
        cols = slice(pr * LANES, (pr + 1) * LANES)
        lane, qz = _pair_masks(q_ref[0, :, cols])
        kc = kc_ref[0, 0, :, cols].astype(BF16)
        vc = vc_ref[0, 0, :, cols].astype(BF16)
        kn = kpad_sc[:, cols]
        vn = vpad_sc[:, cols]
        outs = []
        for e in range(2):
            hd = 2 * pr + e
            s_c = _dot_nt(qz[e], kc) + biasc_ref[hd]
            s_n = _dot_nt(qz[e], kn) + biasn_ref[hd]
            m = jnp.maximum(jnp.max(s_c, axis=1, keepdims=True), jnp.max(s_n, axis=1, keepdims=True))
            p_c = jnp.exp2(s_c - m)
            p_n = jnp.exp2(s_n - m)
            l = jnp.sum(p_c, axis=1, keepdims=True) + jnp.sum(p_n, axis=1, keepdims=True)
            outs.append((_dot(p_c.astype(BF16), vc) + _dot(p_n.astype(BF16), vn)) / l)
        pieces.append(jnp.where(lane < DB, outs[0], outs[1]))
    o_ref[0] = jnp.concatenate(pieces, axis=1)


def _band_attention_sample(qb, kbb, vbb, cache_k, cache_v, layer, bias_c, bias_n):
    b, t, _ = qb.shape
    keep = cache_k.shape[2]
    new = pl.BlockSpec((1, t, W_B), lambda bb: (bb, 0, 0))
    old = pl.BlockSpec((1, 1, keep, W_B), lambda bb: (layer, bb, 0, 0))
    return pl.pallas_call(
        functools.partial(_battn_sample_kernel, t=t),
        grid=(b,),
        in_specs=[new, old, new, old, new, _const_spec(bias_c.shape), _const_spec(bias_n.shape)],
        out_specs=new,
        out_shape=jax.ShapeDtypeStruct((b, t, W_B), F32),
        scratch_shapes=[pltpu.VMEM((LANES, W_B), BF16), pltpu.VMEM((LANES, W_B), BF16)],
        compiler_params=pltpu.CompilerParams(dimension_semantics=("parallel",),
                                             vmem_limit_bytes=VMEM_LIMIT),
        name="band_attention_sample",
    )(qb, cache_k, kbb, cache_v, vbb, bias_c, bias_n)


def _band_bias(rel_bias, q0, nq, k0, nk, n_valid):
    qpos = q0 + np.arange(nq)[:, None]
    kidx = np.arange(nk)[None, :]
    kpos = k0 + kidx
    dchunk = qpos // CHUNK - kpos // CHUNK
    visible = (kpos >= 0) & (dchunk >= 0) & (dchunk <= BAND_CHUNKS) & (kidx < n_valid)
    dist = np.arange(q0 - k0 - (nk - 1), q0 - k0 + nq)
    vals = rel_bias.astype(F32)[:, np.clip(dist, -REL_CLIP, REL_CLIP) + REL_CLIP]
    w_pad = jnp.pad(vals[:, ::-1], ((0, 0), (0, 1)))
    width = nq + nk - 1
    skew = jnp.tile(w_pad, (1, nq))[:, :nq * width].reshape(-1, nq, width)
    return jnp.where(jnp.asarray(visible)[None], skew[:, :, nq - 1:nq - 1 + nk], NEG_INF)


def _merge_ffn_kernel(x_ref, oa_ref, ob_ref, g1_ref, gs_ref, wa_ref, wb_ref, wg_ref, wo_ref,
                      g2_ref, w1_ref, w2_ref, y_ref):
    x = x_ref[...]
    hb = _rms_rows(x, g1_ref[...]).astype(BF16)
    oa = oa_ref[...]
    heads = [_rms_rows(oa[:, c * DVA:(c + 1) * DVA], 1.0) for c in range(H_A)]
    oan = (jnp.concatenate(heads, axis=1) * gs_ref[...]).astype(BF16)
    ya = _dot(oan, wa_ref[...])
    yb = _dot(ob_ref[...].astype(BF16), wb_ref[...])
    gates = jax.nn.sigmoid(_dot(hb, wg_ref[...]))
    mix = gates[:, 0:D_MODEL] * ya + gates[:, D_MODEL:2 * D_MODEL] * yb
    x1 = x + _dot(mix.astype(BF16), wo_ref[...])
    h2 = _rms_rows(x1, g2_ref[...]).astype(BF16)
    u = jnp.maximum(_dot(h2, w1_ref[...]), 0.0)
    y_ref[...] = x1 + _dot((u * u).astype(BF16), w2_ref[...])


def _merge_ffn(x2d, oa2d, ob2d, g1, gs, wa, wb, wg, wo, g2, w1, w2):
    n = x2d.shape[0]
    tm = min(PROJ_ROWS, n)
    assert n % tm == 0
    row = lambda width: pl.BlockSpec((tm, width), lambda i: (i, 0))
    consts = (g1, gs, wa, wb, wg, wo, g2, w1, w2)
    return pl.pallas_call(
        _merge_ffn_kernel,
        grid=(n // tm,),
        in_specs=[row(D_MODEL), row(W_A), row(W_B)] + [_const_spec(c.shape) for c in consts],
        out_specs=row(D_MODEL),
        out_shape=jax.ShapeDtypeStruct((n, D_MODEL), F32),
        compiler_params=pltpu.CompilerParams(dimension_semantics=("parallel",),
                                             vmem_limit_bytes=VMEM_LIMIT),
        name="merge_ffn",
    )(x2d, oa2d, ob2d, *consts)


def kernel(x_prompt, x_sample, cache_a_k, cache_a_v, cache_b_k, cache_b_v, norm1_g, w_in, qn_a_g, kn_a_g, qn_b_g, kn_b_g, lam_q1, lam_k1, lam_q2, lam_k2, subln_a_g, rel_bias_b, w_br_a, w_br_b, w_gate, w_out, norm2_g, w_ff1, w_ff2):
    bp, s, d = x_prompt.shape
    bs, t, _ = x_sample.shape
    depth = w_in.shape[0]
    p_len = cache_a_k.shape[2]
    b_keep = cache_b_k.shape[2]
    keep_p = min(BAND_PAST, s)
    cdt = cache_a_k.dtype

    slopes = jnp.exp2(-8.0 * jnp.arange(1, H_A + 1, dtype=F32) / H_A)
    group = np.arange(NORM_CHUNK) // DA
    ones = jnp.asarray(group[:, None] == group[None, :], BF16)
    row = lambda v, reps, scale=1.0: (jnp.tile(v.astype(F32), reps) * scale)[None, :]
    cache_bk = cache_b_k.reshape(depth, bs, b_keep, W_B)
    cache_bv = cache_b_v.reshape(depth, bs, b_keep, W_B)

    xp = x_prompt.astype(F32).reshape(bp * s, d)
    xs = x_sample.astype(F32).reshape(bs * t, d)
    a_caches_p = ()
    a_caches_s = ()
    b_outs = [[] for _ in range(4)]
    for l in range(depth):
        lam_init = 0.8 - 0.6 * math.exp(-0.3 * l)
        lamv = jnp.stack([lam_q1[l], lam_k1[l], lam_q2[l], lam_k2[l]]).astype(F32)
        g1 = row(norm1_g[l], 1)
        proj_consts = (g1, w_in[l].astype(BF16), ones,
                       row(qn_a_g[l], 2 * H_A, DA ** -0.5 * LOG2E), row(kn_a_g[l], 2 * H_A),
                       row(qn_b_g[l], H_B, DB ** -0.5 * LOG2E), row(kn_b_g[l], H_B))
        rel_l2 = rel_bias_b[l].astype(F32) * LOG2E
        tail = (g1, row(subln_a_g[l], H_A, 1.0 - lam_init), w_br_a[l].astype(BF16),
                w_br_b[l].astype(BF16), w_gate[l].astype(BF16), w_out[l].astype(BF16),
                row(norm2_g[l], 1), w_ff1[l].astype(BF16), w_ff2[l].astype(BF16))

        qa, kaf, kab, vaf, vab, qb, kbf, kbb, vbf, vbb = _projection(xp, *proj_consts, l, depth, a_caches_p)
        a_caches_p = (kaf, vaf)
        r3 = lambda a, n: a.reshape(n, -1, a.shape[-1])
        by_head = lambda a, n: a.reshape(H_A, n, -1, DVA)
        oa = _diff_attention_prompt(r3(qa, bp), by_head(kab, bp), by_head(vab, bp), slopes, lamv, lam_init)
        win = BAND_PAST + BAND_TILE
        bias_t = _band_bias(rel_l2, BAND_PAST, BAND_TILE, 0, win, win).transpose(0, 2, 1)
        bias_t = jnp.pad(bias_t, ((0, 0), (0, 0), (0, LANES)))
        ob = _band_attention_prompt(r3(qb, bp), r3(kbb, bp), r3(vbb, bp), bias_t)
        xp = _merge_ffn(xp, oa.reshape(bp * s, W_A), ob.reshape(bp * s, W_B), *tail)
        b_outs[0].append(r3(kbf, bp)[:, s - keep_p:].reshape(bp, keep_p, H_B, DB))
        b_outs[1].append(r3(vbf, bp)[:, s - keep_p:].reshape(bp, keep_p, H_B, DB))

        qa, kaf, kab, vaf, vab, qb, kbf, kbb, vbf, vbb = _projection(xs, *proj_consts, l, depth, a_caches_s)
        a_caches_s = (kaf, vaf)
        oa = _diff_attention_sample(r3(qa, bs), by_head(kab, bs), by_head(vab, bs), cache_a_k, cache_a_v, l,
                                    slopes, lamv, lam_init)
        bias_c = _band_bias(rel_l2, p_len, t, p_len - b_keep, b_keep, b_keep)
        bias_n = _band_bias(rel_l2, p_len, t, p_len, LANES, t)
        ob = _band_attention_sample(r3(qb, bs), r3(kbb, bs), r3(vbb, bs), cache_bk, cache_bv, l,
                                    bias_c, bias_n)
        xs = _merge_ffn(xs, oa.reshape(bs * t, W_A), ob.reshape(bs * t, W_B), *tail)
        b_outs[2].append(kbf.reshape(bs, t, H_B, DB))
        b_outs[3].append(vbf.reshape(bs, t, H_B, DB))

    bk_p, bv_p, bk_s, bv_s = [jnp.stack(o).astype(cdt) for o in b_outs]
    ak_p, av_p = [a.reshape(depth, bp, s, H_A, DVA).astype(cdt) for a in a_caches_p]
    ak_s, av_s = [a.reshape(depth, bs, t, H_A, DVA).astype(cdt) for a in a_caches_s]
    return (xp.reshape(bp, s, d).astype(x_prompt.dtype), xs.reshape(bs, t, d).astype(x_sample.dtype),
            ak_p, av_p, bk_p, bv_p, ak_s, av_s, bk_s, bv_s)
```

```python
import functools
import math

import numpy as np
import jax
import jax.numpy as jnp
from jax import lax
from jax.experimental import pallas as pl
from jax.experimental.pallas import tpu as pltpu

F32 = jnp.float32
BF16 = jnp.bfloat16

D_MODEL = 1024
CHUNK = 64
H_A = 8
DA = 64
DVA = 2 * DA
H_B = 8
DB = 64
BAND_CHUNKS = 8
BAND_PAST = BAND_CHUNKS * CHUNK
REL_CLIP = 128
EPS = 1e-6
NEG_INF = -1e30
W_A = H_A * 2 * DA
W_B = H_B * DB

LANES = 128
NORM_CHUNK = 256
VMEM_LIMIT = 56 * 1024 * 1024

PROJ_ROWS = 256
ATT_Q_BLOCK = 2048
ATT_K_BLOCK = 512
LOG2E = 1.4426950408889634
N_BIAS_COLS = 3
SUM_ROWS = 16
CHUNK_LANES = 256
BAND_BLOCK = 1024
BAND_TILE = 256


def _rms_rows(x, gain_row):
    return x * lax.rsqrt(jnp.mean(x * x, axis=-1, keepdims=True) + EPS) * gain_row


def _dot(a, b):
    return jnp.dot(a, b, preferred_element_type=F32)


def _dot_nt(a, b):
    return lax.dot_general(a, b, (((1,), (1,)), ((), ())), preferred_element_type=F32)


def _head_norm(z, group_ones, gain_row):
    outs = []
    for c in range(z.shape[1] // NORM_CHUNK):
        zc = z[:, c * NORM_CHUNK:(c + 1) * NORM_CHUNK]
        ss = _dot((zc * zc).astype(BF16), group_ones)
        outs.append(zc * lax.rsqrt(ss * (1.0 / DA) + EPS))
    return jnp.concatenate(outs, axis=1) * gain_row


def _proj_kernel(x_ref, g1_ref, w_ref, ones_ref, gqa_ref, gka_ref, gqb_ref, gkb_ref, *rest, slab):
    (qa_ref, kaf_ref, kab_ref, vaf_ref, vab_ref,
     qb_ref, kbf_ref, kbb_ref, vbf_ref, vbb_ref) = rest[-10:]
    hb = _rms_rows(x_ref[...], g1_ref[...]).astype(BF16)
    ones = ones_ref[...]

    def store_heads(dst_ref, z):
        for d in range(dst_ref.shape[0]):
            if d == slab:
                dst_ref[d] = z.reshape(z.shape[0], H_A, DVA)
            else:
                dst_ref[d] = jnp.zeros(dst_ref.shape[1:], F32)

    def store_head_major(dst_ref, z):
        for hd in range(H_A):
            dst_ref[hd] = z[:, hd * DVA:(hd + 1) * DVA].astype(BF16)

    o = 3 * W_A
    z_qa = _dot(hb, w_ref[:, 0:W_A])
    z_ka = _dot(hb, w_ref[:, W_A:2 * W_A])
    qa_ref[...] = _head_norm(z_qa, ones, gqa_ref[...]).astype(BF16)
    va = _dot(hb, w_ref[:, 2 * W_A:3 * W_A])
    ka = _head_norm(z_ka, ones, gka_ref[...])
    store_heads(kaf_ref, ka)
    store_head_major(kab_ref, ka)
    z_qb = _dot(hb, w_ref[:, o:o + W_B])
    store_heads(vaf_ref, va)
    store_head_major(vab_ref, va)
    z_kb = _dot(hb, w_ref[:, o + W_B:o + 2 * W_B])
    qb_ref[...] = _head_norm(z_qb, ones, gqb_ref[...]).astype(BF16)
    vb = _dot(hb, w_ref[:, o + 2 * W_B:o + 3 * W_B])
    kb = _head_norm(z_kb, ones, gkb_ref[...])
    kbf_ref[...] = kb
    kbb_ref[...] = kb.astype(BF16)
    vbf_ref[...] = vb
    vbb_ref[...] = vb.astype(BF16)


def _const_spec(shape):
    return pl.BlockSpec(shape, lambda *_: (0,) * len(shape), pipeline_mode=pl.Buffered(1))


def _projection(x2d, g1, w_bf, ones, gqa, gka, gqb, gkb, layer, depth, caches):
    n = x2d.shape[0]
    tm = min(PROJ_ROWS, n)
    assert n % tm == 0
    row = lambda width: pl.BlockSpec((tm, width), lambda i: (i, 0))
    if caches:
        slab, slab_in_block = pl.BlockSpec((1, tm, H_A, DVA), lambda i: (layer, i, 0, 0)), 0
    else:
        slab, slab_in_block = pl.BlockSpec((depth, tm, H_A, DVA), lambda i: (0, i, 0, 0)), layer
    out_shapes = []
    out_specs = []
    for width, dtypes in ((W_A, (BF16,)), (W_A, (F32, BF16)), (W_A, (F32, BF16)),
                          (W_B, (BF16,)), (W_B, (F32, BF16)), (W_B, (F32, BF16))):
        for dt in dtypes:
            if width == W_A and dt == F32:
                out_shapes.append(jax.ShapeDtypeStruct((depth, n, H_A, DVA), F32))
                out_specs.append(slab)
            elif width == W_A and len(dtypes) == 2:
                out_shapes.append(jax.ShapeDtypeStruct((H_A, n, DVA), BF16))
                out_specs.append(pl.BlockSpec((H_A, tm, DVA), lambda i: (0, i, 0)))
            else:
                out_shapes.append(jax.ShapeDtypeStruct((n, width), dt))
                out_specs.append(row(width))
    n_in = 8
    return pl.pallas_call(
        functools.partial(_proj_kernel, slab=slab_in_block),
        grid=(n // tm,),
        in_specs=[row(D_MODEL), _const_spec((1, D_MODEL)), _const_spec(w_bf.shape),
                  _const_spec(ones.shape), _const_spec((1, W_A)), _const_spec((1, W_A)),
                  _const_spec((1, W_B)), _const_spec((1, W_B))]
                 + [pl.BlockSpec(memory_space=pl.ANY)] * len(caches),
        out_specs=out_specs,
        out_shape=out_shapes,
        input_output_aliases={n_in: 1, n_in + 1: 3} if caches else {},
        compiler_params=pltpu.CompilerParams(dimension_semantics=("parallel",),
                                             vmem_limit_bytes=VMEM_LIMIT),
        name="projection",
    )(x2d, g1, w_bf, ones, gqa, gka, gqb, gkb, *caches)


def _split_maps(q):
    lane = lax.broadcasted_iota(jnp.int32, q.shape, 1)
    zero = jnp.zeros_like(q)
    return jnp.concatenate([jnp.where(lane < DA, q, zero), jnp.where(lane >= DA, q, zero)], axis=0)


def _lambda(lamv, lam_init):
    a = jnp.sum(lamv[0:1, :] * lamv[1:2, :], axis=1, keepdims=True)
    b = jnp.sum(lamv[2:3, :] * lamv[3:4, :], axis=1, keepdims=True)
    return jnp.exp(a) - jnp.exp(b) + lam_init


def _dattn_kernel(slopes_ref, lamv_ref, q_ref, k_ref, v_ref, o_ref,
                  qz_sc, kb_sc, db_sc, sa_sc, sb_sc, ma_sc, mb_sc, m_sc, acc_sc, *, qblk, kblk, lam_init):
    h = pl.program_id(1)
    i = pl.program_id(2)
    own = qblk // kblk
    sl2 = slopes_ref[h] * LOG2E

    @pl.when(i == 0)
    def _head_tables():
        qz_sc[LANES:2 * LANES, 0:2 * qblk] = (
            lax.broadcasted_iota(jnp.int32, (LANES, 2 * qblk), 0) < N_BIAS_COLS).astype(BF16)
        r = lax.broadcasted_iota(jnp.int32, (kblk, LANES), 0)
        c = lax.broadcasted_iota(jnp.int32, (kblk, LANES), 1)
        rest = sl2 * r.astype(F32)
        kb = jnp.zeros((kblk, LANES), F32)
        for col in range(N_BIAS_COLS):
            part = rest.astype(BF16).astype(F32)
            kb = jnp.where(c == col, part, kb)
            rest = rest - part
        kb_sc[...] = kb.astype(BF16)
        ki = lax.broadcasted_iota(jnp.int32, (kblk, qblk), 0)
        qi = lax.broadcasted_iota(jnp.int32, (kblk, qblk), 1)
        db_sc[:, 0:qblk] = jnp.where((ki // CHUNK) <= (qi // CHUNK),
                                     sl2 * (qi - jnp.abs(qi - ki) - ki).astype(F32), NEG_INF)

    qt = q_ref[0].T
    row = lax.broadcasted_iota(jnp.int32, qt.shape, 0)
    zero = jnp.zeros_like(qt)
    qz_sc[0:LANES, 0:qblk] = jnp.where(row < DA, qt, zero)
    qz_sc[0:LANES, qblk:2 * qblk] = jnp.where(row >= DA, qt, zero)

    m_sc[...] = jnp.full(m_sc.shape, -jnp.inf, F32)
    acc_sc[...] = jnp.zeros(acc_sc.shape, F32)
    ones_rows = jnp.ones((SUM_ROWS, kblk), BF16)

    first_own = own * i

    def rows(j):
        return pl.ds(pl.multiple_of(j * kblk, kblk), kblk)

    def scores(j, dst, max_dst):
        s = _dot(jnp.concatenate([k_ref[0, 0, rows(j), :], kb_sc[...]], axis=1), qz_sc[:, 0:2 * qblk])
        dst[:, 0:2 * qblk] = s
        max_dst[...] = jnp.max(s, axis=0, keepdims=True)

    chunk = CHUNK_LANES

    def own_chunks(d):
        return 2 * (qblk - d * kblk) // chunk

    def own_lanes(d, cc):
        per_map = own_chunks(d) // 2
        first_query = d * kblk + (cc % per_map) * chunk
        first_lane = (cc // per_map) * qblk + first_query
        first_bias = first_query - d * kblk
        return (slice(cc * chunk, (cc + 1) * chunk), slice(first_lane, first_lane + chunk),
                slice(first_bias, first_bias + chunk))

    def own_scores(d, dst, cc):
        packed, lanes, _ = own_lanes(d, cc)
        lhs = jnp.concatenate([k_ref[0, 0, rows(first_own + d), :], kb_sc[...]], axis=1)
        dst[:, packed] = _dot(lhs, qz_sc[:, lanes])

    def own_softmax_pv(d, src, cc):
        packed, lanes, bias_cols = own_lanes(d, cc)
        s = src[:, packed] + db_sc[:, bias_cols]
        off = sl2 * jnp.float32(d * kblk)
        m_prev = m_sc[:, lanes] - off
        m_new = jnp.maximum(m_prev, jnp.max(s, axis=0, keepdims=True))
        alpha = jnp.exp2(m_prev - m_new)
        pb = jnp.exp2(s - m_new).astype(BF16)
        vt = jnp.concatenate([v_ref[0, 0, rows(first_own + d), :].T, ones_rows], axis=0)
        acc_sc[:, lanes] = alpha * acc_sc[:, lanes] + _dot(vt, pb)
        m_sc[:, lanes] = m_new + off

    scores(0, sa_sc, ma_sc)

    def stage(j_next, next_s, next_max, j_cur, cur_s, cur_max):
        lhs = jnp.concatenate([k_ref[0, 0, rows(j_next), :], kb_sc[...]], axis=1)
        vt = jnp.concatenate([v_ref[0, 0, rows(j_cur), :].T, ones_rows], axis=0)
        off = sl2 * (j_cur * kblk - i * qblk).astype(F32)
        width = CHUNK_LANES
        for c in range(2 * qblk // width):
            lanes = slice(c * width, (c + 1) * width)
            s = _dot(lhs, qz_sc[:, lanes])
            next_s[:, lanes] = s
            next_max[:, lanes] = jnp.max(s, axis=0, keepdims=True)
            m_prev = m_sc[:, lanes] - off
            m_new = jnp.maximum(m_prev, cur_max[:, lanes])
            alpha = jnp.exp2(m_prev - m_new)
            pb = jnp.exp2(cur_s[:, lanes] - m_new).astype(BF16)
            acc_sc[:, lanes] = alpha * acc_sc[:, lanes] + _dot(vt, pb)
            m_sc[:, lanes] = m_new + off

    def pair(j0):
        stage(j0 + 1, sb_sc, mb_sc, j0, sa_sc, ma_sc)
        stage(j0 + 2, sa_sc, ma_sc, j0 + 1, sb_sc, mb_sc)

    def two_pairs(qq, carry):
        pair(4 * qq)
        pair(4 * qq + 2)
        return carry

    n_pairs = first_own // 2
    lax.fori_loop(0, n_pairs // 2, two_pairs, 0)

    @pl.when(n_pairs % 2 == 1)
    def _last_pair():
        pair(first_own - 2)

    bufs = (sa_sc, sb_sc)
    for d in range(own):
        n_soft = own_chunks(d)
        n_next = own_chunks(d + 1) if d + 1 < own else 0
        issued = 0
        for cc in range(n_soft):
            while issued < n_next and issued * n_soft <= cc * n_next:
                own_scores(d + 1, bufs[(d + 1) % 2], issued)
                issued += 1
            own_softmax_pv(d, bufs[d % 2], cc)

    inv = 1.0 / acc_sc[DVA:DVA + 1, 0:2 * qblk]
    lam = _lambda(lamv_ref[...], lam_init)
    o_ref[0] = (acc_sc[0:DVA, 0:qblk] * inv[:, 0:qblk]
                - acc_sc[0:DVA, qblk:2 * qblk] * (lam * inv[:, qblk:2 * qblk])).T


def _diff_attention_prompt(qa, kab, vab, slopes, lamv, lam_init):
    b, s, _ = qa.shape
    qblk = min(ATT_Q_BLOCK, s)
    kblk = min(ATT_K_BLOCK, qblk // 2)
    assert s % qblk == 0 and kblk % CHUNK == 0 and (qblk // kblk) % 2 == 0 and kblk % CHUNK_LANES == 0
    head_rows = pl.BlockSpec((1, 1, s, DVA), lambda bb, h, i: (h, bb, 0, 0))
    block = pl.BlockSpec((1, qblk, DVA), lambda bb, h, i: (bb, i, h))
    return pl.pallas_call(
        functools.partial(_dattn_kernel, qblk=qblk, kblk=kblk, lam_init=lam_init),
        grid=(b, H_A, s // qblk),
        in_specs=[pl.BlockSpec(memory_space=pltpu.SMEM), pl.BlockSpec((4, DA), lambda bb, h, i: (0, 0)),
                  block, head_rows, head_rows],
        out_specs=block,
        out_shape=jax.ShapeDtypeStruct((b, s, W_A), F32),
        scratch_shapes=[pltpu.VMEM((2 * LANES, 2 * qblk + LANES), BF16), pltpu.VMEM((kblk, LANES), BF16),
                        pltpu.VMEM((kblk, qblk + LANES), F32),
                        pltpu.VMEM((kblk, 2 * qblk + LANES), F32), pltpu.VMEM((kblk, 2 * qblk + LANES), F32),
                        pltpu.VMEM((1, 2 * qblk), F32), pltpu.VMEM((1, 2 * qblk), F32),
                        pltpu.VMEM((1, 2 * qblk), F32),
                        pltpu.VMEM((DVA + SUM_ROWS, 2 * qblk + LANES), F32)],
        compiler_params=pltpu.CompilerParams(
            dimension_semantics=("parallel", "parallel", "arbitrary"), vmem_limit_bytes=VMEM_LIMIT),
        name="diff_attention",
    )(slopes, lamv, qa, kab, vab)


def _dattn_sample_kernel(slopes_ref, lamv_ref, q_ref, kc_ref, kn_ref, vc_ref, vn_ref, o_ref,
                         kpad_sc, vpad_sc, *, t, p_len, lam_init):
    kpad_sc[...] = jnp.zeros(kpad_sc.shape, BF16)
    vpad_sc[...] = jnp.zeros(vpad_sc.shape, BF16)
    for hd in range(H_A):
        kpad_sc[0:t, hd * DVA:(hd + 1) * DVA] = kn_ref[hd, 0]
        vpad_sc[0:t, hd * DVA:(hd + 1) * DVA] = vn_ref[hd, 0]
    lam = _lambda(lamv_ref[...], lam_init)

    def distance(k0, nk, n_valid):
        qpos = lax.broadcasted_iota(jnp.int32, (t, nk), 0) + p_len
        kidx = lax.broadcasted_iota(jnp.int32, (t, nk), 1)
        kpos = kidx + k0
        visible = ((kpos // CHUNK) <= (qpos // CHUNK)) & (kidx < n_valid)
        return visible, jnp.abs(qpos - kpos).astype(F32)

    vis_c, dist_c = distance(0, p_len, p_len)
    vis_n, dist_n = distance(p_len, LANES, t)
    kc_all = kc_ref[0, 0].reshape(p_len, W_A).astype(BF16)
    vc_all = vc_ref[0, 0].reshape(p_len, W_A).astype(BF16)
    outs = []
    for hd in range(H_A):
        cols = slice(hd * DVA, (hd + 1) * DVA)
        slope = slopes_ref[hd] * LOG2E
        qz = _split_maps(q_ref[0, :, cols])

        def scores(k, vis, dist):
            s = _dot_nt(qz, k)
            bias_t = jnp.where(vis, -slope * dist, NEG_INF)
            return (s.reshape(2, t, -1) + bias_t[None]).reshape(2 * t, -1)

        s_c = scores(kc_all[:, cols], vis_c, dist_c)
        s_n = scores(kpad_sc[:, cols], vis_n, dist_n)
        m = jnp.maximum(jnp.max(s_c, axis=1, keepdims=True), jnp.max(s_n, axis=1, keepdims=True))
        p_c = jnp.exp2(s_c - m)
        p_n = jnp.exp2(s_n - m)
        l = jnp.sum(p_c, axis=1, keepdims=True) + jnp.sum(p_n, axis=1, keepdims=True)
        out = (_dot(p_c.astype(BF16), vc_all[:, cols]) + _dot(p_n.astype(BF16), vpad_sc[:, cols])) / l
        outs.append(out[0:t] - lam * out[t:2 * t])
    o_ref[0] = jnp.concatenate(outs, axis=1)


def _diff_attention_sample(qa, kab, vab, cache_k, cache_v, layer, slopes, lamv, lam_init):
    b, t, _ = qa.shape
    p_len = cache_k.shape[2]
    assert t <= LANES
    new = pl.BlockSpec((1, t, W_A), lambda bb: (bb, 0, 0))
    new_kv = pl.BlockSpec((H_A, 1, t, DVA), lambda bb: (0, bb, 0, 0))
    old = pl.BlockSpec((1, 1, p_len, H_A, DVA), lambda bb: (layer, bb, 0, 0, 0))
    return pl.pallas_call(
        functools.partial(_dattn_sample_kernel, t=t, p_len=p_len, lam_init=lam_init),
        grid=(b,),
        in_specs=[pl.BlockSpec(memory_space=pltpu.SMEM), pl.BlockSpec((4, DA), lambda bb: (0, 0)),
                  new, old, new_kv, old, new_kv],
        out_specs=new,
        out_shape=jax.ShapeDtypeStruct((b, t, W_A), F32),
        scratch_shapes=[pltpu.VMEM((LANES, W_A), BF16), pltpu.VMEM((LANES, W_A), BF16)],
        compiler_params=pltpu.CompilerParams(dimension_semantics=("parallel",),
                                             vmem_limit_bytes=VMEM_LIMIT),
        name="diff_attention_sample",
    )(slopes, lamv, qa, cache_k, kab, cache_v, vab)


def _pair_masks(q):
    lane = lax.broadcasted_iota(jnp.int32, q.shape, 1)
    zero = jnp.zeros_like(q)
    return lane, (jnp.where(lane < DB, q, zero), jnp.where(lane >= DB, q, zero))


def _battn_kernel(q_ref, kp_ref, kc_ref, vp_ref, vc_ref, bias_ref, o_ref,
                  sa_sc, sb_sc, sc_sc, sd_sc, *, blk):
    i = pl.program_id(2)
    win = BAND_PAST + BAND_TILE
    qt = q_ref[0].T
    row = lax.broadcasted_iota(jnp.int32, qt.shape, 0)
    zero = jnp.zeros_like(qt)
    qz = (jnp.where(row < DB, qt, zero), jnp.where(row >= DB, qt, zero))
    ones_rows = jnp.ones((SUM_ROWS, win), BF16)
    key_row = lax.broadcasted_iota(jnp.int32, (win, BAND_TILE), 0)
    out_row = lax.broadcasted_iota(jnp.int32, (LANES, BAND_TILE), 0)

    def window(prev_ref, cur_ref, lo):
        cur = cur_ref[0, max(lo - BAND_PAST, 0):lo + BAND_TILE, :]
        return jnp.concatenate([prev_ref[0, lo:BAND_PAST, :], cur], axis=0) if lo < BAND_PAST else cur

    def scores(item, dst):
        lo, e = item
        dst[:, 0:BAND_TILE] = _dot(window(kp_ref, kc_ref, lo), qz[e][:, lo:lo + BAND_TILE])

    def softmax_pv(item, src):
        lo, e = item
        s = src[:, 0:BAND_TILE] + bias_ref[e, :, 0:BAND_TILE]
        n_prev = BAND_PAST - lo
        if n_prev > 0:
            s = jnp.where((i > 0) | (key_row >= n_prev), s, NEG_INF)
        pb = jnp.exp2(s - jnp.max(s, axis=0, keepdims=True)).astype(BF16)
        vt = jnp.concatenate([window(vp_ref, vc_ref, lo).T, ones_rows], axis=0)
        pv = _dot(vt, pb)
        return pv[0:LANES] / pv[LANES:LANES + 1]

    n_tiles = blk // BAND_TILE
    bufs = ((sa_sc, sb_sc), (sc_sc, sd_sc))
    tiles = []
    for e in range(2):
        scores((0, e), bufs[0][e])
    for nt in range(n_tiles):
        if nt + 1 < n_tiles:
            for e in range(2):
                scores(((nt + 1) * BAND_TILE, e), bufs[(nt + 1) % 2][e])
        outs = [softmax_pv((nt * BAND_TILE, e), bufs[nt % 2][e]) for e in range(2)]
        tiles.append(jnp.where(out_row < DB, outs[0], outs[1]))
    o_ref[0] = jnp.concatenate(tiles, axis=1).T


def _band_attention_prompt(qb, kbb, vbb, bias_t):
    b, s, _ = qb.shape
    blk = min(BAND_BLOCK, s)
    ratio = blk // BAND_PAST
    assert s % blk == 0 and blk % BAND_PAST == 0 and BAND_PAST % BAND_TILE == 0
    cur = pl.BlockSpec((1, blk, LANES), lambda pr, bb, i: (bb, i, pr))
    prev = pl.BlockSpec((1, BAND_PAST, LANES),
                        lambda pr, bb, i: (bb, jnp.maximum(i * ratio - 1, 0), pr))
    return pl.pallas_call(
        functools.partial(_battn_kernel, blk=blk),
        grid=(H_B // 2, b, s // blk),
        in_specs=[cur, prev, cur, prev, cur,
                  pl.BlockSpec((2,) + bias_t.shape[1:], lambda pr, bb, i: (pr, 0, 0))],
        out_specs=cur,
        out_shape=jax.ShapeDtypeStruct((b, s, W_B), F32),
        scratch_shapes=[pltpu.VMEM((bias_t.shape[1], BAND_TILE + LANES), F32)] * 4,
        compiler_params=pltpu.CompilerParams(
            dimension_semantics=("parallel", "parallel", "parallel"), vmem_limit_bytes=VMEM_LIMIT),
        name="band_attention",
    )(qb, kbb, kbb, vbb, vbb, bias_t)


def _battn_sample_kernel(q_ref, kc_ref, kn_ref, vc_ref, vn_ref, biasc_ref, biasn_ref, o_ref,
                         kpad_sc, vpad_sc, *, t):
    kpad_sc[...] = jnp.zeros(kpad_sc.shape, BF16)
    vpad_sc[...] = jnp.zeros(vpad_sc.shape, BF16)
    kpad_sc[0:t, :] = kn_ref[0]
    vpad_sc[0:t, :] = vn_ref[0]
    pieces = []
    for pr in range(H_B // 2):
        cols = slice(pr * LANES, (pr + 1) * LANES)
        lane, qz = _pair_masks(q_ref[0, :, cols])
        kc = kc_ref[0, 0, :, cols].astype(BF16)
        vc = vc_ref[0, 0, :, cols].astype(BF16)
        kn = kpad_sc[:, cols]
        vn = vpad_sc[:, cols]
        outs = []
        for e in range(2):
            hd = 2 * pr + e
            s_c = _dot_nt(qz[e], kc) + biasc_ref[hd]
            s_n = _dot_nt(qz[e], kn) + biasn_ref[hd]
            m = jnp.maximum(jnp.max(s_c, axis=1, keepdims=True), jnp.max(s_n, axis=1, keepdims=True))
            p_c = jnp.exp2(s_c - m)
            p_n = jnp.exp2(s_n - m)
            l = jnp.sum(p_c, axis=1, keepdims=True) + jnp.sum(p_n, axis=1, keepdims=True)
            outs.append((_dot(p_c.astype(BF16), vc) + _dot(p_n.astype(BF16), vn)) / l)
        pieces.append(jnp.where(lane < DB, outs[0], outs[1]))
    o_ref[0] = jnp.concatenate(pieces, axis=1)


def _band_attention_sample(qb, kbb, vbb, cache_k, cache_v, layer, bias_c, bias_n):
    b, t, _ = qb.shape
    keep = cache_k.shape[2]
    new = pl.BlockSpec((1, t, W_B), lambda bb: (bb, 0, 0))
    old = pl.BlockSpec((1, 1, keep, W_B), lambda bb: (layer, bb, 0, 0))
    return pl.pallas_call(
        functools.partial(_battn_sample_kernel, t=t),
        grid=(b,),
        in_specs=[new, old, new, old, new, _const_spec(bias_c.shape), _const_spec(bias_n.shape)],
        out_specs=new,
        out_shape=jax.ShapeDtypeStruct((b, t, W_B), F32),
        scratch_shapes=[pltpu.VMEM((LANES, W_B), BF16), pltpu.VMEM((LANES, W_B), BF16)],
        compiler_params=pltpu.CompilerParams(dimension_semantics=("parallel",),
                                             vmem_limit_bytes=VMEM_LIMIT),
        name="band_attention_sample",
    )(qb, cache_k, kbb, cache_v, vbb, bias_c, bias_n)


def _band_bias(rel_bias, q0, nq, k0, nk, n_valid):
    qpos = q0 + np.arange(nq)[:, None]
    kidx = np.arange(nk)[None, :]
    kpos = k0 + kidx
    dchunk = qpos // CHUNK - kpos // CHUNK
    visible = (kpos >= 0) & (dchunk >= 0) & (dchunk <= BAND_CHUNKS) & (kidx < n_valid)
    dist = np.arange(q0 - k0 - (nk - 1), q0 - k0 + nq)
    vals = rel_bias.astype(F32)[:, np.clip(dist, -REL_CLIP, REL_CLIP) + REL_CLIP]
    w_pad = jnp.pad(vals[:, ::-1], ((0, 0), (0, 1)))
    width = nq + nk - 1
    skew = jnp.tile(w_pad, (1, nq))[:, :nq * width].reshape(-1, nq, width)
    return jnp.where(jnp.asarray(visible)[None], skew[:, :, nq - 1:nq - 1 + nk], NEG_INF)


def _merge_ffn_kernel(x_ref, oa_ref, ob_ref, g1_ref, gs_ref, wa_ref, wb_ref, wg_ref, wo_ref,
                      g2_ref, w1_ref, w2_ref, y_ref):
    x = x_ref[...]
    hb = _rms_rows(x, g1_ref[...]).astype(BF16)
    oa = oa_ref[...]
    heads = [_rms_rows(oa[:, c * DVA:(c + 1) * DVA], 1.0) for c in range(H_A)]
    oan = (jnp.concatenate(heads, axis=1) * gs_ref[...]).astype(BF16)
    ya = _dot(oan, wa_ref[...])
    yb = _dot(ob_ref[...].astype(BF16), wb_ref[...])
    gates = jax.nn.sigmoid(_dot(hb, wg_ref[...]))
    mix = gates[:, 0:D_MODEL] * ya + gates[:, D_MODEL:2 * D_MODEL] * yb
    x1 = x + _dot(mix.astype(BF16), wo_ref[...])
    h2 = _rms_rows(x1, g2_ref[...]).astype(BF16)
    u = jnp.maximum(_dot(h2, w1_ref[...]), 0.0)
    y_ref[...] = x1 + _dot((u * u).astype(BF16), w2_ref[...])


def _merge_ffn(x2d, oa2d, ob2d, g1, gs, wa, wb, wg, wo, g2, w1, w2):
    n = x2d.shape[0]
    tm = min(PROJ_ROWS, n)
    assert n % tm == 0
    row = lambda width: pl.BlockSpec((tm, width), lambda i: (i, 0))
    consts = (g1, gs, wa, wb, wg, wo, g2, w1, w2)
    return pl.pallas_call(
        _merge_ffn_kernel,
        grid=(n // tm,),
        in_specs=[row(D_MODEL), row(W_A), row(W_B)] + [_const_spec(c.shape) for c in consts],
        out_specs=row(D_MODEL),
        out_shape=jax.ShapeDtypeStruct((n, D_MODEL), F32),
        compiler_params=pltpu.CompilerParams(dimension_semantics=("parallel",),
                                             vmem_limit_bytes=VMEM_LIMIT),
        name="merge_ffn",
    )(x2d, oa2d, ob2d, *consts)


def kernel(x_prompt, x_sample, cache_a_k, cache_a_v, cache_b_k, cache_b_v, norm1_g, w_in, qn_a_g, kn_a_g, qn_b_g, kn_b_g, lam_q1, lam_k1, lam_q2, lam_k2, subln_a_g, rel_bias_b, w_br_a, w_br_b, w_gate, w_out, norm2_g, w_ff1, w_ff2):
    bp, s, d = x_prompt.shape
    bs, t, _ = x_sample.shape
    depth = w_in.shape[0]
    p_len = cache_a_k.shape[2]
    b_keep = cache_b_k.shape[2]
    keep_p = min(BAND_PAST, s)
    cdt = cache_a_k.dtype

    slopes = jnp.exp2(-8.0 * jnp.arange(1, H_A + 1, dtype=F32) / H_A)
    group = np.arange(NORM_CHUNK) // DA
    ones = jnp.asarray(group[:, None] == group[None, :], BF16)
    row = lambda v, reps, scale=1.0: (jnp.tile(v.astype(F32), reps) * scale)[None, :]
    cache_bk = cache_b_k.reshape(depth, bs, b_keep, W_B)
    cache_bv = cache_b_v.reshape(depth, bs, b_keep, W_B)

    xp = x_prompt.astype(F32).reshape(bp * s, d)
    xs = x_sample.astype(F32).reshape(bs * t, d)
    a_caches_p = ()
    a_caches_s = ()
    b_outs = [[] for _ in range(4)]
    for l in range(depth):
        lam_init = 0.8 - 0.6 * math.exp(-0.3 * l)
        lamv = jnp.stack([lam_q1[l], lam_k1[l], lam_q2[l], lam_k2[l]]).astype(F32)
        g1 = row(norm1_g[l], 1)
        proj_consts = (g1, w_in[l].astype(BF16), ones,
                       row(qn_a_g[l], 2 * H_A, DA ** -0.5 * LOG2E), row(kn_a_g[l], 2 * H_A),
                       row(qn_b_g[l], H_B, DB ** -0.5 * LOG2E), row(kn_b_g[l], H_B))
        rel_l2 = rel_bias_b[l].astype(F32) * LOG2E
        tail = (g1, row(subln_a_g[l], H_A, 1.0 - lam_init), w_br_a[l].astype(BF16),
                w_br_b[l].astype(BF16), w_gate[l].astype(BF16), w_out[l].astype(BF16),
                row(norm2_g[l], 1), w_ff1[l].astype(BF16), w_ff2[l].astype(BF16))

        qa, kaf, kab, vaf, vab, qb, kbf, kbb, vbf, vbb = _projection(xp, *proj_consts, l, depth, a_caches_p)
        a_caches_p = (kaf, vaf)
        r3 = lambda a, n: a.reshape(n, -1, a.shape[-1])
        by_head = lambda a, n: a.reshape(H_A, n, -1, DVA)
        oa = _diff_attention_prompt(r3(qa, bp), by_head(kab, bp), by_head(vab, bp), slopes, lamv, lam_init)
        win = BAND_PAST + BAND_TILE
        bias_t = _band_bias(rel_l2, BAND_PAST, BAND_TILE, 0, win, win).transpose(0, 2, 1)
        bias_t = jnp.pad(bias_t, ((0, 0), (0, 0), (0, LANES)))
        ob = _band_attention_prompt(r3(qb, bp), r3(kbb, bp), r3(vbb, bp), bias_t)
        xp = _merge_ffn(xp, oa.reshape(bp * s, W_A), ob.reshape(bp * s, W_B), *tail)
        b_outs[0].append(r3(kbf, bp)[:, s - keep_p:].reshape(bp, keep_p, H_B, DB))
        b_outs[1].append(r3(vbf, bp)[:, s - keep_p:].reshape(bp, keep_p, H_B, DB))

        qa, kaf, kab, vaf, vab, qb, kbf, kbb, vbf, vbb = _projection(xs, *proj_consts, l, depth, a_caches_s)
        a_caches_s = (kaf, vaf)
        oa = _diff_attention_sample(r3(qa, bs), by_head(kab, bs), by_head(vab, bs), cache_a_k, cache_a_v, l,
                                    slopes, lamv, lam_init)
        bias_c = _band_bias(rel_l2, p_len, t, p_len - b_keep, b_keep, b_keep)
        bias_n = _band_bias(rel_l2, p_len, t, p_len, LANES, t)
        ob = _band_attention_sample(r3(qb, bs), r3(kbb, bs), r3(vbb, bs), cache_bk, cache_bv, l,
                                    bias_c, bias_n)
        xs = _merge_ffn(xs, oa.reshape(bs * t, W_A), ob.reshape(bs * t, W_B), *tail)
        b_outs[2].append(kbf.reshape(bs, t, H_B, DB))
        b_outs[3].append(vbf.reshape(bs, t, H_B, DB))

    bk_p, bv_p, bk_s, bv_s = [jnp.stack(o).astype(cdt) for o in b_outs]
    ak_p, av_p = [a.reshape(depth, bp, s, H_A, DVA).astype(cdt) for a in a_caches_p]
    ak_s, av_s = [a.reshape(depth, bs, t, H_A, DVA).astype(cdt) for a in a_caches_s]
    return (xp.reshape(bp, s, d).astype(x_prompt.dtype), xs.reshape(bs, t, d).astype(x_sample.dtype),
            ak_p, av_p, bk_p, bv_p, ak_s, av_s, bk_s, bv_s)
```

```python
import functools
import math

import numpy as np
import jax
import jax.numpy as jnp
from jax import lax
from jax.experimental import pallas as pl
from jax.experimental.pallas import tpu as pltpu

F32 = jnp.float32
BF16 = jnp.bfloat16

D_MODEL = 1024
CHUNK = 64
H_A = 8
DA = 64
DVA = 2 * DA
H_B = 8
DB = 64
BAND_CHUNKS = 8
BAND_PAST = BAND_CHUNKS * CHUNK
REL_CLIP = 128
EPS = 1e-6
NEG_INF = -1e30
W_A = H_A * 2 * DA
W_B = H_B * DB

LANES = 128
NORM_CHUNK = 256
VMEM_LIMIT = 56 * 1024 * 1024

PROJ_ROWS = 256
ATT_Q_BLOCK = 2048
ATT_K_BLOCK = 512
LOG2E = 1.4426950408889634
N_BIAS_COLS = 3
SUM_ROWS = 16
CHUNK_LANES = 256
BAND_BLOCK = 1024
BAND_TILE = 256


def _rms_rows(x, gain_row):
    return x * lax.rsqrt(jnp.mean(x * x, axis=-1, keepdims=True) + EPS) * gain_row


def _dot(a, b):
    return jnp.dot(a, b, preferred_element_type=F32)


def _dot_nt(a, b):
    return lax.dot_general(a, b, (((1,), (1,)), ((), ())), preferred_element_type=F32)


def _head_norm(z, group_ones, gain_row):
    outs = []
    for c in range(z.shape[1] // NORM_CHUNK):
        zc = z[:, c * NORM_CHUNK:(c + 1) * NORM_CHUNK]
        ss = _dot((zc * zc).astype(BF16), group_ones)
        outs.append(zc * lax.rsqrt(ss * (1.0 / DA) + EPS))
    return jnp.concatenate(outs, axis=1) * gain_row


def _proj_kernel(x_ref, g1_ref, w_ref, ones_ref, gqa_ref, gka_ref, gqb_ref, gkb_ref, *rest, slab):
    (qa_ref, kaf_ref, kab_ref, vaf_ref, vab_ref,
     qb_ref, kbf_ref, kbb_ref, vbf_ref, vbb_ref) = rest[-10:]
    hb = _rms_rows(x_ref[...], g1_ref[...]).astype(BF16)
    ones = ones_ref[...]

    def store_heads(dst_ref, z):
        for d in range(dst_ref.shape[0]):
            if d == slab:
                dst_ref[d] = z.reshape(z.shape[0], H_A, DVA)
            else:
                dst_ref[d] = jnp.zeros(dst_ref.shape[1:], F32)

    def store_head_major(dst_ref, z):
        for hd in range(H_A):
            dst_ref[hd] = z[:, hd * DVA:(hd + 1) * DVA].astype(BF16)

    o = 3 * W_A
    z_qa = _dot(hb, w_ref[:, 0:W_A])
    z_ka = _dot(hb, w_ref[:, W_A:2 * W_A])
    qa_ref[...] = _head_norm(z_qa, ones, gqa_ref[...]).astype(BF16)
    va = _dot(hb, w_ref[:, 2 * W_A:3 * W_A])
    ka = _head_norm(z_ka, ones, gka_ref[...])
    store_heads(kaf_ref, ka)
    store_head_major(kab_ref, ka)
    z_qb = _dot(hb, w_ref[:, o:o + W_B])
    store_heads(vaf_ref, va)
    store_head_major(vab_ref, va)
    z_kb = _dot(hb, w_ref[:, o + W_B:o + 2 * W_B])
    qb_ref[...] = _head_norm(z_qb, ones, gqb_ref[...]).astype(BF16)
    vb = _dot(hb, w_ref[:, o + 2 * W_B:o + 3 * W_B])
    kb = _head_norm(z_kb, ones, gkb_ref[...])
    kbf_ref[...] = kb
    kbb_ref[...] = kb.astype(BF16)
    vbf_ref[...] = vb
    vbb_ref[...] = vb.astype(BF16)


def _const_spec(shape):
    return pl.BlockSpec(shape, lambda *_: (0,) * len(shape), pipeline_mode=pl.Buffered(1))


def _projection(x2d, g1, w_bf, ones, gqa, gka, gqb, gkb, layer, depth, caches):
    n = x2d.shape[0]
    tm = min(PROJ_ROWS, n)
    assert n % tm == 0
    row = lambda width: pl.BlockSpec((tm, width), lambda i: (i, 0))
    if caches:
        slab, slab_in_block = pl.BlockSpec((1, tm, H_A, DVA), lambda i: (layer, i, 0, 0)), 0
    else:
        slab, slab_in_block = pl.BlockSpec((depth, tm, H_A, DVA), lambda i: (0, i, 0, 0)), layer
    out_shapes = []
    out_specs = []
    for width, dtypes in ((W_A, (BF16,)), (W_A, (F32, BF16)), (W_A, (F32, BF16)),
                          (W_B, (BF16,)), (W_B, (F32, BF16)), (W_B, (F32, BF16))):
        for dt in dtypes:
            if width == W_A and dt == F32:
                out_shapes.append(jax.ShapeDtypeStruct((depth, n, H_A, DVA), F32))
                out_specs.append(slab)
            elif width == W_A and len(dtypes) == 2:
                out_shapes.append(jax.ShapeDtypeStruct((H_A, n, DVA), BF16))
                out_specs.append(pl.BlockSpec((H_A, tm, DVA), lambda i: (0, i, 0)))
            else:
                out_shapes.append(jax.ShapeDtypeStruct((n, width), dt))
                out_specs.append(row(width))
    n_in = 8
    return pl.pallas_call(
        functools.partial(_proj_kernel, slab=slab_in_block),
        grid=(n // tm,),
        in_specs=[row(D_MODEL), _const_spec((1, D_MODEL)), _const_spec(w_bf.shape),
                  _const_spec(ones.shape), _const_spec((1, W_A)), _const_spec((1, W_A)),
                  _const_spec((1, W_B)), _const_spec((1, W_B))]
                 + [pl.BlockSpec(memory_space=pl.ANY)] * len(caches),
        out_specs=out_specs,
        out_shape=out_shapes,
        input_output_aliases={n_in: 1, n_in + 1: 3} if caches else {},
        compiler_params=pltpu.CompilerParams(dimension_semantics=("parallel",),
                                             vmem_limit_bytes=VMEM_LIMIT),
        name="projection",
    )(x2d, g1, w_bf, ones, gqa, gka, gqb, gkb, *caches)


def _split_maps(q):
    lane = lax.broadcasted_iota(jnp.int32, q.shape, 1)
    zero = jnp.zeros_like(q)
    return jnp.concatenate([jnp.where(lane < DA, q, zero), jnp.where(lane >= DA, q, zero)], axis=0)


def _lambda(lamv, lam_init):
    a = jnp.sum(lamv[0:1, :] * lamv[1:2, :], axis=1, keepdims=True)
    b = jnp.sum(lamv[2:3, :] * lamv[3:4, :], axis=1, keepdims=True)
    return jnp.exp(a) - jnp.exp(b) + lam_init


def _dattn_kernel(slopes_ref, lamv_ref, q_ref, k_ref, v_ref, o_ref,
                  qz_sc, kb_sc, db_sc, sa_sc, sb_sc, ma_sc, mb_sc, m_sc, acc_sc, *, qblk, kblk, lam_init):
    h = pl.program_id(1)
    i = pl.program_id(2)
    own = qblk // kblk
    sl2 = slopes_ref[h] * LOG2E

    @pl.when(i == 0)
    def _head_tables():
        qz_sc[LANES:2 * LANES, 0:2 * qblk] = (
            lax.broadcasted_iota(jnp.int32, (LANES, 2 * qblk), 0) < N_BIAS_COLS).astype(BF16)
        r = lax.broadcasted_iota(jnp.int32, (kblk, LANES), 0)
        c = lax.broadcasted_iota(jnp.int32, (kblk, LANES), 1)
        rest = sl2 * r.astype(F32)
        kb = jnp.zeros((kblk, LANES), F32)
        for col in range(N_BIAS_COLS):
            part = rest.astype(BF16).astype(F32)
            kb = jnp.where(c == col, part, kb)
            rest = rest - part
        kb_sc[...] = kb.astype(BF16)
        ki = lax.broadcasted_iota(jnp.int32, (kblk, qblk), 0)
        qi = lax.broadcasted_iota(jnp.int32, (kblk, qblk), 1)
        db_sc[:, 0:qblk] = jnp.where((ki // CHUNK) <= (qi // CHUNK),
                                     sl2 * (qi - jnp.abs(qi - ki) - ki).astype(F32), NEG_INF)

    m_sc[...] = jnp.full(m_sc.shape, -jnp.inf, F32)
    acc_sc[...] = jnp.zeros(acc_sc.shape, F32)
    ones_rows = jnp.ones((SUM_ROWS, kblk), BF16)

    first_own = own * i

    def rows(j):
        return pl.ds(pl.multiple_of(j * kblk, kblk), kblk)

    lhs0 = jnp.concatenate([k_ref[0, 0, rows(0), :], kb_sc[...]], axis=1)
    for c in range(qblk // CHUNK_LANES):
        qt = q_ref[0, c * CHUNK_LANES:(c + 1) * CHUNK_LANES, :].T
        row = lax.broadcasted_iota(jnp.int32, qt.shape, 0)
        zero = jnp.zeros_like(qt)
        for first_lane, part in ((0, jnp.where(row < DA, qt, zero)), (qblk, jnp.where(row >= DA, qt, zero))):
            lanes = slice(first_lane + c * CHUNK_LANES, first_lane + (c + 1) * CHUNK_LANES)
            qz_sc[0:LANES, lanes] = part
            s = _dot(lhs0, qz_sc[:, lanes])
            sa_sc[:, lanes] = s
            ma_sc[:, lanes] = jnp.max(s, axis=0, keepdims=True)

    chunk = CHUNK_LANES

    def own_chunks(d):
        return 2 * (qblk - d * kblk) // chunk

    def own_lanes(d, cc):
        per_map = own_chunks(d) // 2
        first_query = d * kblk + (cc % per_map) * chunk
        first_lane = (cc // per_map) * qblk + first_query
        first_bias = first_query - d * kblk
        return (slice(cc * chunk, (cc + 1) * chunk), slice(first_lane, first_lane + chunk),
                slice(first_bias, first_bias + chunk))

    def own_scores(d, dst, cc):
        packed, lanes, _ = own_lanes(d, cc)
        lhs = jnp.concatenate([k_ref[0, 0, rows(first_own + d), :], kb_sc[...]], axis=1)
        dst[:, packed] = _dot(lhs, qz_sc[:, lanes])

    def own_softmax_pv(d, src, cc):
        packed, lanes, bias_cols = own_lanes(d, cc)
        s = src[:, packed] + db_sc[:, bias_cols]
        off = sl2 * jnp.float32(d * kblk)
        m_prev = m_sc[:, lanes] - off
        m_new = jnp.maximum(m_prev, jnp.max(s, axis=0, keepdims=True))
        alpha = jnp.exp2(m_prev - m_new)
        pb = jnp.exp2(s - m_new).astype(BF16)
        vt = jnp.concatenate([v_ref[0, 0, rows(first_own + d), :].T, ones_rows], axis=0)
        acc_sc[:, lanes] = alpha * acc_sc[:, lanes] + _dot(vt, pb)
        m_sc[:, lanes] = m_new + off


    def stage(j_next, next_s, next_max, j_cur, cur_s, cur_max):
        lhs = jnp.concatenate([k_ref[0, 0, rows(j_next), :], kb_sc[...]], axis=1)
        vt = jnp.concatenate([v_ref[0, 0, rows(j_cur), :].T, ones_rows], axis=0)
        off = sl2 * (j_cur * kblk - i * qblk).astype(F32)
        width = CHUNK_LANES
        for c in range(2 * qblk // width):
            lanes = slice(c * width, (c + 1) * width)
            s = _dot(lhs, qz_sc[:, lanes])
            next_s[:, lanes] = s
            next_max[:, lanes] = jnp.max(s, axis=0, keepdims=True)
            m_prev = m_sc[:, lanes] - off
            m_new = jnp.maximum(m_prev, cur_max[:, lanes])
            alpha = jnp.exp2(m_prev - m_new)
            pb = jnp.exp2(cur_s[:, lanes] - m_new).astype(BF16)
            acc_sc[:, lanes] = alpha * acc_sc[:, lanes] + _dot(vt, pb)
            m_sc[:, lanes] = m_new + off

    def pair(j0):
        stage(j0 + 1, sb_sc, mb_sc, j0, sa_sc, ma_sc)
        stage(j0 + 2, sa_sc, ma_sc, j0 + 1, sb_sc, mb_sc)

    def two_pairs(qq, carry):
        pair(4 * qq)
        pair(4 * qq + 2)
        return carry

    n_pairs = first_own // 2
    lax.fori_loop(0, n_pairs // 2, two_pairs, 0)

    @pl.when(n_pairs % 2 == 1)
    def _last_pair():
        pair(first_own - 2)

    bufs = (sa_sc, sb_sc)
    for d in range(own):
        n_soft = own_chunks(d)
        n_next = own_chunks(d + 1) if d + 1 < own else 0
        issued = 0
        for cc in range(n_soft):
            while issued < n_next and issued * n_soft <= cc * n_next:
                own_scores(d + 1, bufs[(d + 1) % 2], issued)
                issued += 1
            own_softmax_pv(d, bufs[d % 2], cc)

    inv = 1.0 / acc_sc[DVA:DVA + 1, 0:2 * qblk]
    lam = _lambda(lamv_ref[...], lam_init)
    o_ref[0] = (acc_sc[0:DVA, 0:qblk] * inv[:, 0:qblk]
                - acc_sc[0:DVA, qblk:2 * qblk] * (lam * inv[:, qblk:2 * qblk])).T


def _diff_attention_prompt(qa, kab, vab, slopes, lamv, lam_init):
    b, s, _ = qa.shape
    qblk = min(ATT_Q_BLOCK, s)
    kblk = min(ATT_K_BLOCK, qblk // 2)
    assert s % qblk == 0 and kblk % CHUNK == 0 and (qblk // kblk) % 2 == 0 and kblk % CHUNK_LANES == 0
    head_rows = pl.BlockSpec((1, 1, s, DVA), lambda bb, h, i: (h, bb, 0, 0))
    block = pl.BlockSpec((1, qblk, DVA), lambda bb, h, i: (bb, i, h))
    return pl.pallas_call(
        functools.partial(_dattn_kernel, qblk=qblk, kblk=kblk, lam_init=lam_init),
        grid=(b, H_A, s // qblk),
        in_specs=[pl.BlockSpec(memory_space=pltpu.SMEM), pl.BlockSpec((4, DA), lambda bb, h, i: (0, 0)),
                  block, head_rows, head_rows],
        out_specs=block,
        out_shape=jax.ShapeDtypeStruct((b, s, W_A), F32),
        scratch_shapes=[pltpu.VMEM((2 * LANES, 2 * qblk + LANES), BF16), pltpu.VMEM((kblk, LANES), BF16),
                        pltpu.VMEM((kblk, qblk + LANES), F32),
                        pltpu.VMEM((kblk, 2 * qblk + LANES), F32), pltpu.VMEM((kblk, 2 * qblk + LANES), F32),
                        pltpu.VMEM((1, 2 * qblk), F32), pltpu.VMEM((1, 2 * qblk), F32),
                        pltpu.VMEM((1, 2 * qblk), F32),
                        pltpu.VMEM((DVA + SUM_ROWS, 2 * qblk + LANES), F32)],
        compiler_params=pltpu.CompilerParams(
            dimension_semantics=("parallel", "parallel", "arbitrary"), vmem_limit_bytes=VMEM_LIMIT),
        name="diff_attention",
    )(slopes, lamv, qa, kab, vab)


def _dattn_sample_kernel(slopes_ref, lamv_ref, q_ref, kc_ref, kn_ref, vc_ref, vn_ref, o_ref,
                         kpad_sc, vpad_sc, *, t, p_len, lam_init):
    kpad_sc[...] = jnp.zeros(kpad_sc.shape, BF16)
    vpad_sc[...] = jnp.zeros(vpad_sc.shape, BF16)
    for hd in range(H_A):
        kpad_sc[0:t, hd * DVA:(hd + 1) * DVA] = kn_ref[hd, 0]
        vpad_sc[0:t, hd * DVA:(hd + 1) * DVA] = vn_ref[hd, 0]
    lam = _lambda(lamv_ref[...], lam_init)

    def distance(k0, nk, n_valid):
        qpos = lax.broadcasted_iota(jnp.int32, (t, nk), 0) + p_len
        kidx = lax.broadcasted_iota(jnp.int32, (t, nk), 1)
        kpos = kidx + k0
        visible = ((kpos // CHUNK) <= (qpos // CHUNK)) & (kidx < n_valid)
        return visible, jnp.abs(qpos - kpos).astype(F32)

    vis_c, dist_c = distance(0, p_len, p_len)
    vis_n, dist_n = distance(p_len, LANES, t)
    kc_all = kc_ref[0, 0].reshape(p_len, W_A).astype(BF16)
    vc_all = vc_ref[0, 0].reshape(p_len, W_A).astype(BF16)
    outs = []
    for hd in range(H_A):
        cols = slice(hd * DVA, (hd + 1) * DVA)
        slope = slopes_ref[hd] * LOG2E
        qz = _split_maps(q_ref[0, :, cols])

        def scores(k, vis, dist):
            s = _dot_nt(qz, k)
            bias_t = jnp.where(vis, -slope * dist, NEG_INF)
            return (s.reshape(2, t, -1) + bias_t[None]).reshape(2 * t, -1)

        s_c = scores(kc_all[:, cols], vis_c, dist_c)
        s_n = scores(kpad_sc[:, cols], vis_n, dist_n)
        m = jnp.maximum(jnp.max(s_c, axis=1, keepdims=True), jnp.max(s_n, axis=1, keepdims=True))
        p_c = jnp.exp2(s_c - m)
        p_n = jnp.exp2(s_n - m)
        l = jnp.sum(p_c, axis=1, keepdims=True) + jnp.sum(p_n, axis=1, keepdims=True)
        out = (_dot(p_c.astype(BF16), vc_all[:, cols]) + _dot(p_n.astype(BF16), vpad_sc[:, cols])) / l
        outs.append(out[0:t] - lam * out[t:2 * t])
    o_ref[0] = jnp.concatenate(outs, axis=1)


def _diff_attention_sample(qa, kab, vab, cache_k, cache_v, layer, slopes, lamv, lam_init):
    b, t, _ = qa.shape
    p_len = cache_k.shape[2]
    assert t <= LANES
    new = pl.BlockSpec((1, t, W_A), lambda bb: (bb, 0, 0))
    new_kv = pl.BlockSpec((H_A, 1, t, DVA), lambda bb: (0, bb, 0, 0))
    old = pl.BlockSpec((1, 1, p_len, H_A, DVA), lambda bb: (layer, bb, 0, 0, 0))
    return pl.pallas_call(
        functools.partial(_dattn_sample_kernel, t=t, p_len=p_len, lam_init=lam_init),
        grid=(b,),
        in_specs=[pl.BlockSpec(memory_space=pltpu.SMEM), pl.BlockSpec((4, DA), lambda bb: (0, 0)),
                  new, old, new_kv, old, new_kv],
        out_specs=new,
        out_shape=jax.ShapeDtypeStruct((b, t, W_A), F32),
        scratch_shapes=[pltpu.VMEM((LANES, W_A), BF16), pltpu.VMEM((LANES, W_A), BF16)],
        compiler_params=pltpu.CompilerParams(dimension_semantics=("parallel",),
                                             vmem_limit_bytes=VMEM_LIMIT),
        name="diff_attention_sample",
    )(slopes, lamv, qa, cache_k, kab, cache_v, vab)


def _pair_masks(q):
    lane = lax.broadcasted_iota(jnp.int32, q.shape, 1)
    zero = jnp.zeros_like(q)
    return lane, (jnp.where(lane < DB, q, zero), jnp.where(lane >= DB, q, zero))


def _battn_kernel(q_ref, kp_ref, kc_ref, vp_ref, vc_ref, bias_ref, o_ref,
                  sa_sc, sb_sc, sc_sc, sd_sc, *, blk):
    i = pl.program_id(2)
    win = BAND_PAST + BAND_TILE
    qt = q_ref[0].T
    row = lax.broadcasted_iota(jnp.int32, qt.shape, 0)
    zero = jnp.zeros_like(qt)
    qz = (jnp.where(row < DB, qt, zero), jnp.where(row >= DB, qt, zero))
    ones_rows = jnp.ones((SUM_ROWS, win), BF16)
    key_row = lax.broadcasted_iota(jnp.int32, (win, BAND_TILE), 0)
    out_row = lax.broadcasted_iota(jnp.int32, (LANES, BAND_TILE), 0)

    def window(prev_ref, cur_ref, lo):
        cur = cur_ref[0, max(lo - BAND_PAST, 0):lo + BAND_TILE, :]
        return jnp.concatenate([prev_ref[0, lo:BAND_PAST, :], cur], axis=0) if lo < BAND_PAST else cur

    def scores(item, dst):
        lo, e = item
        dst[:, 0:BAND_TILE] = _dot(window(kp_ref, kc_ref, lo), qz[e][:, lo:lo + BAND_TILE])

    def softmax_pv(item, src):
        lo, e = item
        s = src[:, 0:BAND_TILE] + bias_ref[e, :, 0:BAND_TILE]
        n_prev = BAND_PAST - lo
        if n_prev > 0:
            s = jnp.where((i > 0) | (key_row >= n_prev), s, NEG_INF)
        pb = jnp.exp2(s - jnp.max(s, axis=0, keepdims=True)).astype(BF16)
        vt = jnp.concatenate([window(vp_ref, vc_ref, lo).T, ones_rows], axis=0)
        pv = _dot(vt, pb)
        return pv[0:LANES] / pv[LANES:LANES + 1]

    n_tiles = blk // BAND_TILE
    bufs = ((sa_sc, sb_sc), (sc_sc, sd_sc))
    tiles = []
    for e in range(2):
        scores((0, e), bufs[0][e])
    for nt in range(n_tiles):
        if nt + 1 < n_tiles:
            for e in range(2):
                scores(((nt + 1) * BAND_TILE, e), bufs[(nt + 1) % 2][e])
        outs = [softmax_pv((nt * BAND_TILE, e), bufs[nt % 2][e]) for e in range(2)]
        tiles.append(jnp.where(out_row < DB, outs[0], outs[1]))
    o_ref[0] = jnp.concatenate(tiles, axis=1).T


def _band_attention_prompt(qb, kbb, vbb, bias_t):
    b, s, _ = qb.shape
    blk = min(BAND_BLOCK, s)
    ratio = blk // BAND_PAST
    assert s % blk == 0 and blk % BAND_PAST == 0 and BAND_PAST % BAND_TILE == 0
    cur = pl.BlockSpec((1, blk, LANES), lambda pr, bb, i: (bb, i, pr))
    prev = pl.BlockSpec((1, BAND_PAST, LANES),
                        lambda pr, bb, i: (bb, jnp.maximum(i * ratio - 1, 0), pr))
    return pl.pallas_call(
        functools.partial(_battn_kernel, blk=blk),
        grid=(H_B // 2, b, s // blk),
        in_specs=[cur, prev, cur, prev, cur,
                  pl.BlockSpec((2,) + bias_t.shape[1:], lambda pr, bb, i: (pr, 0, 0))],
        out_specs=cur,
        out_shape=jax.ShapeDtypeStruct((b, s, W_B), F32),
        scratch_shapes=[pltpu.VMEM((bias_t.shape[1], BAND_TILE + LANES), F32)] * 4,
        compiler_params=pltpu.CompilerParams(
            dimension_semantics=("parallel", "parallel", "parallel"), vmem_limit_bytes=VMEM_LIMIT),
        name="band_attention",
    )(qb, kbb, kbb, vbb, vbb, bias_t)


def _battn_sample_kernel(q_ref, kc_ref, kn_ref, vc_ref, vn_ref, biasc_ref, biasn_ref, o_ref,
                         kpad_sc, vpad_sc, *, t):
    kpad_sc[...] = jnp.zeros(kpad_sc.shape, BF16)
    vpad_sc[...] = jnp.zeros(vpad_sc.shape, BF16)
    kpad_sc[0:t, :] = kn_ref[0]
    vpad_sc[0:t, :] = vn_ref[0]
    pieces = []
    for pr in range(H_B // 2):
        cols = slice(pr * LANES, (pr + 1) * LANES)
        lane, qz = _pair_masks(q_ref[0, :, cols])
        kc = kc_ref[0, 0, :, cols].astype(BF16)
        vc = vc_ref[0, 0, :, cols].astype(BF16)
        kn = kpad_sc[:, cols]
        vn = vpad_sc[:, cols]
        outs = []
        for e in range(2):
            hd = 2 * pr + e
            s_c = _dot_nt(qz[e], kc) + biasc_ref[hd]
            s_n = _dot_nt(qz[e], kn) + biasn_ref[hd]
            m = jnp.maximum(jnp.max(s_c, axis=1, keepdims=True), jnp.max(s_n, axis=1, keepdims=True))
            p_c = jnp.exp2(s_c - m)
            p_n = jnp.exp2(s_n - m)
            l = jnp.sum(p_c, axis=1, keepdims=True) + jnp.sum(p_n, axis=1, keepdims=True)
            outs.append((_dot(p_c.astype(BF16), vc) + _dot(p_n.astype(BF16), vn)) / l)
        pieces.append(jnp.where(lane < DB, outs[0], outs[1]))
    o_ref[0] = jnp.concatenate(pieces, axis=1)


def _band_attention_sample(qb, kbb, vbb, cache_k, cache_v, layer, bias_c, bias_n):
    b, t, _ = qb.shape
    keep = cache_k.shape[2]
    new = pl.BlockSpec((1, t, W_B), lambda bb: (bb, 0, 0))
    old = pl.BlockSpec((1, 1, keep, W_B), lambda bb: (layer, bb, 0, 0))
    return pl.pallas_call(
        functools.partial(_battn_sample_kernel, t=t),
        grid=(b,),
        in_specs=[new, old, new, old, new, _const_spec(bias_c.shape), _const_spec(bias_n.shape)],
        out_specs=new,
        out_shape=jax.ShapeDtypeStruct((b, t, W_B), F32),
        scratch_shapes=[pltpu.VMEM((LANES, W_B), BF16), pltpu.VMEM((LANES, W_B), BF16)],
        compiler_params=pltpu.CompilerParams(dimension_semantics=("parallel",),
                                             vmem_limit_bytes=VMEM_LIMIT),
        name="band_attention_sample",
    )(qb, cache_k, kbb, cache_v, vbb, bias_c, bias_n)


def _band_bias(rel_bias, q0, nq, k0, nk, n_valid):
    qpos = q0 + np.arange(nq)[:, None]
    kidx = np.arange(nk)[None, :]
    kpos = k0 + kidx
    dchunk = qpos // CHUNK - kpos // CHUNK
    visible = (kpos >= 0) & (dchunk >= 0) & (dchunk <= BAND_CHUNKS) & (kidx < n_valid)
    dist = np.arange(q0 - k0 - (nk - 1), q0 - k0 + nq)
    vals = rel_bias.astype(F32)[:, np.clip(dist, -REL_CLIP, REL_CLIP) + REL_CLIP]
    w_pad = jnp.pad(vals[:, ::-1], ((0, 0), (0, 1)))
    width = nq + nk - 1
    skew = jnp.tile(w_pad, (1, nq))[:, :nq * width].reshape(-1, nq, width)
    return jnp.where(jnp.asarray(visible)[None], skew[:, :, nq - 1:nq - 1 + nk], NEG_INF)


def _merge_ffn_kernel(x_ref, oa_ref, ob_ref, g1_ref, gs_ref, wa_ref, wb_ref, wg_ref, wo_ref,
                      g2_ref, w1_ref, w2_ref, y_ref):
    x = x_ref[...]
    hb = _rms_rows(x, g1_ref[...]).astype(BF16)
    oa = oa_ref[...]
    heads = [_rms_rows(oa[:, c * DVA:(c + 1) * DVA], 1.0) for c in range(H_A)]
    oan = (jnp.concatenate(heads, axis=1) * gs_ref[...]).astype(BF16)
    ya = _dot(oan, wa_ref[...])
    yb = _dot(ob_ref[...].astype(BF16), wb_ref[...])
    gates = jax.nn.sigmoid(_dot(hb, wg_ref[...]))
    mix = gates[:, 0:D_MODEL] * ya + gates[:, D_MODEL:2 * D_MODEL] * yb
    x1 = x + _dot(mix.astype(BF16), wo_ref[...])
    h2 = _rms_rows(x1, g2_ref[...]).astype(BF16)
    u = jnp.maximum(_dot(h2, w1_ref[...]), 0.0)
    y_ref[...] = x1 + _dot((u * u).astype(BF16), w2_ref[...])


def _merge_ffn(x2d, oa2d, ob2d, g1, gs, wa, wb, wg, wo, g2, w1, w2):
    n = x2d.shape[0]
    tm = min(PROJ_ROWS, n)
    assert n % tm == 0
    row = lambda width: pl.BlockSpec((tm, width), lambda i: (i, 0))
    consts = (g1, gs, wa, wb, wg, wo, g2, w1, w2)
    return pl.pallas_call(
        _merge_ffn_kernel,
        grid=(n // tm,),
        in_specs=[row(D_MODEL), row(W_A), row(W_B)] + [_const_spec(c.shape) for c in consts],
        out_specs=row(D_MODEL),
        out_shape=jax.ShapeDtypeStruct((n, D_MODEL), F32),
        compiler_params=pltpu.CompilerParams(dimension_semantics=("parallel",),
                                             vmem_limit_bytes=VMEM_LIMIT),
        name="merge_ffn",
    )(x2d, oa2d, ob2d, *consts)


def kernel(x_prompt, x_sample, cache_a_k, cache_a_v, cache_b_k, cache_b_v, norm1_g, w_in, qn_a_g, kn_a_g, qn_b_g, kn_b_g, lam_q1, lam_k1, lam_q2, lam_k2, subln_a_g, rel_bias_b, w_br_a, w_br_b, w_gate, w_out, norm2_g, w_ff1, w_ff2):
    bp, s, d = x_prompt.shape
    bs, t, _ = x_sample.shape
    depth = w_in.shape[0]
    p_len = cache_a_k.shape[2]
    b_keep = cache_b_k.shape[2]
    keep_p = min(BAND_PAST, s)
    cdt = cache_a_k.dtype

    slopes = jnp.exp2(-8.0 * jnp.arange(1, H_A + 1, dtype=F32) / H_A)
    group = np.arange(NORM_CHUNK) // DA
    ones = jnp.asarray(group[:, None] == group[None, :], BF16)
    row = lambda v, reps, scale=1.0: (jnp.tile(v.astype(F32), reps) * scale)[None, :]
    cache_bk = cache_b_k.reshape(depth, bs, b_keep, W_B)
    cache_bv = cache_b_v.reshape(depth, bs, b_keep, W_B)

    xp = x_prompt.astype(F32).reshape(bp * s, d)
    xs = x_sample.astype(F32).reshape(bs * t, d)
    a_caches_p = ()
    a_caches_s = ()
    b_outs = [[] for _ in range(4)]
    for l in range(depth):
        lam_init = 0.8 - 0.6 * math.exp(-0.3 * l)
        lamv = jnp.stack([lam_q1[l], lam_k1[l], lam_q2[l], lam_k2[l]]).astype(F32)
        g1 = row(norm1_g[l], 1)
        proj_consts = (g1, w_in[l].astype(BF16), ones,
                       row(qn_a_g[l], 2 * H_A, DA ** -0.5 * LOG2E), row(kn_a_g[l], 2 * H_A),
                       row(qn_b_g[l], H_B, DB ** -0.5 * LOG2E), row(kn_b_g[l], H_B))
        rel_l2 = rel_bias_b[l].astype(F32) * LOG2E
        tail = (g1, row(subln_a_g[l], H_A, 1.0 - lam_init), w_br_a[l].astype(BF16),
                w_br_b[l].astype(BF16), w_gate[l].astype(BF16), w_out[l].astype(BF16),
                row(norm2_g[l], 1), w_ff1[l].astype(BF16), w_ff2[l].astype(BF16))

        qa, kaf, kab, vaf, vab, qb, kbf, kbb, vbf, vbb = _projection(xp, *proj_consts, l, depth, a_caches_p)
        a_caches_p = (kaf, vaf)
        r3 = lambda a, n: a.reshape(n, -1, a.shape[-1])
        by_head = lambda a, n: a.reshape(H_A, n, -1, DVA)
        oa = _diff_attention_prompt(r3(qa, bp), by_head(kab, bp), by_head(vab, bp), slopes, lamv, lam_init)
        win = BAND_PAST + BAND_TILE
        bias_t = _band_bias(rel_l2, BAND_PAST, BAND_TILE, 0, win, win).transpose(0, 2, 1)
        bias_t = jnp.pad(bias_t, ((0, 0), (0, 0), (0, LANES)))
        ob = _band_attention_prompt(r3(qb, bp), r3(kbb, bp), r3(vbb, bp), bias_t)
        xp = _merge_ffn(xp, oa.reshape(bp * s, W_A), ob.reshape(bp * s, W_B), *tail)
        b_outs[0].append(r3(kbf, bp)[:, s - keep_p:].reshape(bp, keep_p, H_B, DB))
        b_outs[1].append(r3(vbf, bp)[:, s - keep_p:].reshape(bp, keep_p, H_B, DB))

        qa, kaf, kab, vaf, vab, qb, kbf, kbb, vbf, vbb = _projection(xs, *proj_consts, l, depth, a_caches_s)
        a_caches_s = (kaf, vaf)
        oa = _diff_attention_sample(r3(qa, bs), by_head(kab, bs), by_head(vab, bs), cache_a_k, cache_a_v, l,
                                    slopes, lamv, lam_init)
        bias_c = _band_bias(rel_l2, p_len, t, p_len - b_keep, b_keep, b_keep)
        bias_n = _band_bias(rel_l2, p_len, t, p_len, LANES, t)
        ob = _band_attention_sample(r3(qb, bs), r3(kbb, bs), r3(vbb, bs), cache_bk, cache_bv, l,
                                    bias_c, bias_n)
        xs = _merge_ffn(xs, oa.reshape(bs * t, W_A), ob.reshape(bs * t, W_B), *tail)
        b_outs[2].append(kbf.reshape(bs, t, H_B, DB))
        b_outs[3].append(vbf.reshape(bs, t, H_B, DB))

    bk_p, bv_p, bk_s, bv_s = [jnp.stack(o).astype(cdt) for o in b_outs]
    ak_p, av_p = [a.reshape(depth, bp, s, H_A, DVA).astype(cdt) for a in a_caches_p]
    ak_s, av_s = [a.reshape(depth, bs, t, H_A, DVA).astype(cdt) for a in a_caches_s]
    return (xp.reshape(bp, s, d).astype(x_prompt.dtype), xs.reshape(bs, t, d).astype(x_sample.dtype),
            ak_p, av_p, bk_p, bv_p, ak_s, av_s, bk_s, bv_s)
```
